```python
import jax, jax.numpy as jnp
from jax import lax
import numpy as np

D_MODEL = 1024
BATCH = 32
SEQ = 2048
DEPTH = 1

HEAD_DIM = 64
ROPE_THETA = 10000.0
EPS = 1e-6
GRID_W = 64
DIL_PAIRS = ((128, 1), (512, 4), (2048, 16))
N_GROUPS_A = len(DIL_PAIRS)
HEADS_PER_GROUP_A = 4
A_QBLOCK = 64
N_Q_HEADS_B = 8
N_KV_HEADS_B = 2
B_QBLOCK = 128
N_BRANCHES = 2
A_QKV_WIDTH = N_GROUPS_A * 3 * HEADS_PER_GROUP_A * HEAD_DIM
A_OUT_WIDTH = HEADS_PER_GROUP_A * HEAD_DIM
B_Q_WIDTH = N_Q_HEADS_B * HEAD_DIM
B_KV_WIDTH = 2 * N_KV_HEADS_B * HEAD_DIM
IN_WIDTH = A_QKV_WIDTH + B_Q_WIDTH + B_KV_WIDTH + N_BRANCHES * D_MODEL
N_EXPERTS = 64
N_EXPERT_GROUPS = 8
TOPK_GROUPS = 4
TOP_K = 8
EXPERT_FF = D_MODEL // 4
ROUTED_SCALE = 2.5
MOE_BLOCK = 128

kernel_name = 'hybrid_dilated_gqa2d_gated_moe_encoder'


def rms_norm(x, g):
    xf = x.astype(jnp.float32)
    y = xf * lax.rsqrt(jnp.mean(xf * xf, axis=-1, keepdims=True) + EPS)
    return (y * g.astype(jnp.float32)).astype(x.dtype)


def rope_tables(pos, dim):
    inv = ROPE_THETA ** (-jnp.arange(0, dim, 2, dtype=jnp.float32) / dim)
    ang = pos.astype(jnp.float32)[:, None] * inv[None, :]
    ang = jnp.concatenate([ang, ang], axis=-1)
    return jnp.cos(ang), jnp.sin(ang)


def apply_rope(x, cos, sin):
    s, dim = cos.shape
    bshape = (s,) + (1,) * (x.ndim - 3) + (dim,)
    xf = x.astype(jnp.float32)
    x1, x2 = xf[..., : dim // 2], xf[..., dim // 2:]
    rot = jnp.concatenate([-x2, x1], axis=-1)
    return (xf * cos.reshape(bshape) + rot * sin.reshape(bshape)).astype(x.dtype)


def apply_rope_2d(x, cos_r, sin_r, cos_c, sin_c):
    half = HEAD_DIM // 2
    return jnp.concatenate([apply_rope(x[..., :half], cos_r, sin_r),
                            apply_rope(x[..., half:], cos_c, sin_c)], axis=-1)


def _split_blocks(t, block):
    b, s = t.shape[:2]
    return t.reshape((b, s // block, block) + t.shape[2:]).swapaxes(0, 1)


def _merge_blocks(t):
    nb, b, blk = t.shape[:3]
    return t.swapaxes(0, 1).reshape((b, nb * blk) + t.shape[3:])


def _dilated_group(q, k, v, dilation, radius):
    b, s, h, hd = q.shape
    pad = dilation * radius
    offs = dilation * jnp.arange(-radius, radius + 1)
    k_p = jnp.pad(k, ((0, 0), (pad, pad), (0, 0), (0, 0)))
    v_p = jnp.pad(v, ((0, 0), (pad, pad), (0, 0), (0, 0)))
    q_blocks = _split_blocks(q, A_QBLOCK)
    starts = jnp.arange(s // A_QBLOCK) * A_QBLOCK
    scale = HEAD_DIM ** -0.5

    def block(args):
        qb, s0 = args
        kpos = s0 + jnp.arange(A_QBLOCK)[:, None] + offs[None, :]
        valid = (kpos >= 0) & (kpos < s)
        kb = k_p[:, kpos + pad]
        vb = v_p[:, kpos + pad].astype(jnp.float32)
        sc = jnp.einsum('bqhd,bqjhd->bqhj', qb, kb, preferred_element_type=jnp.float32) * scale
        sc = jnp.where(valid[None, :, None, :], sc, -jnp.inf)
        m = sc.max(axis=-1)
        p = jnp.exp(sc - m[..., None])
        den = p.sum(axis=-1)
        o = jnp.einsum('bqhj,bqjhd->bqhd', p, vb) / den[..., None]
        return o, m, den

    o, m, den = lax.map(block, (q_blocks, starts))
    return _merge_blocks(o), _merge_blocks(m), _merge_blocks(den)


def dilated_attention(q, k, v):
    b, s = q.shape[:2]
    outs, maxes, dens = [], [], []
    for g, (window, dilation) in enumerate(DIL_PAIRS):
        o, m, den = _dilated_group(q[:, :, g], k[:, :, g], v[:, :, g], dilation, window // (2 * dilation))
        outs.append(o); maxes.append(m); dens.append(den)
    o = jnp.stack(outs)
    m = jnp.stack(maxes)
    den = jnp.stack(dens)
    wts = den * jnp.exp(m - m.max(axis=0, keepdims=True))
    out = (wts[..., None] * o).sum(axis=0) / wts.sum(axis=0)[..., None]
    return out.reshape(b, s, HEADS_PER_GROUP_A * HEAD_DIM).astype(q.dtype)


def gqa_attention(q, k, v):
    b, s, hq, hd = q.shape
    hkv = k.shape[2]
    rep = hq // hkv
    q_blocks = _split_blocks(q.reshape(b, s, hkv, rep, hd), B_QBLOCK)
    vf = v.astype(jnp.float32)
    scale = HEAD_DIM ** -0.5

    def block(qb):
        sc = jnp.einsum('bqgrd,bkgd->bgrqk', qb, k, preferred_element_type=jnp.float32) * scale
        p = jax.nn.softmax(sc, axis=-1)
        return jnp.einsum('bgrqk,bkgd->bqgrd', p, vf)

    o = _merge_blocks(lax.map(block, q_blocks))
    return o.reshape(b, s, hq * hd).astype(q.dtype)


def moe_ffn(xn, w_router, router_bias, we_gate, we_up, we_down, ws_gate, ws_up, ws_down):
    b, s, d = xn.shape
    n = b * s
    xf = xn.reshape(n, d)
    scores = jax.nn.sigmoid(jnp.dot(xf, w_router, preferred_element_type=jnp.float32))
    sel = scores + router_bias.astype(jnp.float32)
    gsel = sel.reshape(n, N_EXPERT_GROUPS, N_EXPERTS // N_EXPERT_GROUPS)
    gscore = lax.top_k(gsel, 2)[0].sum(axis=-1)
    _, gidx = lax.top_k(gscore, TOPK_GROUPS)
    gmask = (gidx[..., None] == jnp.arange(N_EXPERT_GROUPS)).any(axis=-2)
    emask = jnp.repeat(gmask, N_EXPERTS // N_EXPERT_GROUPS, axis=-1)
    _, eidx = lax.top_k(jnp.where(emask, sel, -jnp.inf), TOP_K)
    w = jnp.take_along_axis(scores, eidx, axis=-1)
    w = w / w.sum(axis=-1, keepdims=True) * ROUTED_SCALE
    nk = n * TOP_K
    flat_e = eidx.reshape(-1)
    flat_tok = jnp.arange(nk, dtype=jnp.int32) // TOP_K
    flat_w = w.reshape(-1)
    order = jnp.argsort(flat_e)
    se, stok, sw = flat_e[order], flat_tok[order], flat_w[order]
    counts = jnp.bincount(flat_e, length=N_EXPERTS)
    padded = (counts + MOE_BLOCK - 1) // MOE_BLOCK * MOE_BLOCK
    start = jnp.cumsum(counts) - counts
    pend = jnp.cumsum(padded)
    pstart = pend - padded
    dest = pstart[se] + jnp.arange(nk) - start[se]
    n_blocks = -(-(nk + N_EXPERTS * (MOE_BLOCK - 1)) // MOE_BLOCK)
    p_len = n_blocks * MOE_BLOCK
    buf_tok = jnp.zeros((p_len,), jnp.int32).at[dest].set(stok)
    buf_w = jnp.zeros((p_len,), jnp.float32).at[dest].set(sw)
    block_e = jnp.minimum(jnp.searchsorted(pend, jnp.arange(n_blocks) * MOE_BLOCK, side='right'),
                          N_EXPERTS - 1)
    shared = jnp.dot(jax.nn.silu(jnp.dot(xf, ws_gate)) * jnp.dot(xf, ws_up), ws_down).astype(jnp.float32)

    def step(acc, inp):
        tok, wb, e = inp
        xb = xf[tok]
        hb = jax.nn.silu(jnp.dot(xb, we_gate[e])) * jnp.dot(xb, we_up[e])
        yb = jnp.dot(hb, we_down[e]).astype(jnp.float32)
        return acc.at[tok].add(wb[:, None] * yb), None

    out, _ = lax.scan(step, shared, (buf_tok.reshape(n_blocks, MOE_BLOCK),
                                     buf_w.reshape(n_blocks, MOE_BLOCK), block_e))
    return out.astype(xn.dtype).reshape(b, s, d)


def setup_inputs(seed: int = 0) -> dict:
    key = jax.random.key(seed)
    ks = jax.random.split(key, 20)

    def nrm(k, shape, scale):
        return jax.random.normal(k, shape, jnp.float32) * scale

    return {
        'x': nrm(ks[0], (BATCH, SEQ, D_MODEL), 1.0),
        'norm1_g': 1.0 + nrm(ks[1], (DEPTH, D_MODEL), 0.02),
        'w_in': nrm(ks[2], (DEPTH, D_MODEL, IN_WIDTH), D_MODEL ** -0.5),
        'q_norm_a': 1.0 + nrm(ks[3], (DEPTH, HEAD_DIM), 0.02),
        'k_norm_a': 1.0 + nrm(ks[4], (DEPTH, HEAD_DIM), 0.02),
        'q_norm_b': 1.0 + nrm(ks[5], (DEPTH, HEAD_DIM), 0.02),
        'k_norm_b': 1.0 + nrm(ks[6], (DEPTH, HEAD_DIM), 0.02),
        'w_branch_a': nrm(ks[7], (DEPTH, A_OUT_WIDTH, D_MODEL), A_OUT_WIDTH ** -0.5),
        'w_branch_b': nrm(ks[8], (DEPTH, B_Q_WIDTH, D_MODEL), B_Q_WIDTH ** -0.5),
        'w_out': nrm(ks[9], (DEPTH, D_MODEL, D_MODEL), D_MODEL ** -0.5),
        'norm2_g': 1.0 + nrm(ks[10], (DEPTH, D_MODEL), 0.02),
        'w_router': nrm(ks[11], (DEPTH, D_MODEL, N_EXPERTS), D_MODEL ** -0.5),
        'router_bias': nrm(ks[12], (DEPTH, N_EXPERTS), 0.01),
        'we_gate': nrm(ks[13], (DEPTH, N_EXPERTS, D_MODEL, EXPERT_FF), D_MODEL ** -0.5),
        'we_up': nrm(ks[14], (DEPTH, N_EXPERTS, D_MODEL, EXPERT_FF), D_MODEL ** -0.5),
        'we_down': nrm(ks[15], (DEPTH, N_EXPERTS, EXPERT_FF, D_MODEL), EXPERT_FF ** -0.5),
        'ws_gate': nrm(ks[16], (DEPTH, D_MODEL, EXPERT_FF), D_MODEL ** -0.5),
        'ws_up': nrm(ks[17], (DEPTH, D_MODEL, EXPERT_FF), D_MODEL ** -0.5),
        'ws_down': nrm(ks[18], (DEPTH, EXPERT_FF, D_MODEL), EXPERT_FF ** -0.5),
    }


def reference(x, norm1_g, w_in, q_norm_a, k_norm_a, q_norm_b, k_norm_b, w_branch_a, w_branch_b,
              w_out, norm2_g, w_router, router_bias, we_gate, we_up, we_down, ws_gate, ws_up, ws_down):
    b, s, d = x.shape
    rows = s // GRID_W
    pos = jnp.arange(s)
    row_idx = jnp.repeat(jnp.arange(rows), GRID_W)
    col_idx = jnp.tile(jnp.arange(GRID_W), rows)
    cos1, sin1 = rope_tables(pos, HEAD_DIM)
    cos_r, sin_r = rope_tables(row_idx, HEAD_DIM // 2)
    cos_c, sin_c = rope_tables(col_idx, HEAD_DIM // 2)
    o1 = A_QKV_WIDTH
    o2 = o1 + B_Q_WIDTH
    o3 = o2 + B_KV_WIDTH

    h = x
    for l in range(DEPTH):
        xn = rms_norm(h, norm1_g[l])
        proj = jnp.dot(xn, w_in[l])
        a_qkv = proj[..., :o1].reshape(b, s, N_GROUPS_A, 3, HEADS_PER_GROUP_A, HEAD_DIM)
        qa = apply_rope(rms_norm(a_qkv[:, :, :, 0], q_norm_a[l]), cos1, sin1)
        ka = apply_rope(rms_norm(a_qkv[:, :, :, 1], k_norm_a[l]), cos1, sin1)
        va = a_qkv[:, :, :, 2]
        ya = dilated_attention(qa, ka, va)
        qb = proj[..., o1:o2].reshape(b, s, N_Q_HEADS_B, HEAD_DIM)
        kvb = proj[..., o2:o3].reshape(b, s, 2, N_KV_HEADS_B, HEAD_DIM)
        qb = apply_rope_2d(rms_norm(qb, q_norm_b[l]), cos_r, sin_r, cos_c, sin_c)
        kb = apply_rope_2d(rms_norm(kvb[:, :, 0], k_norm_b[l]), cos_r, sin_r, cos_c, sin_c)
        vb = kvb[:, :, 1]
        yb = gqa_attention(qb, kb, vb)
        gates = jax.nn.sigmoid(proj[..., o3:].reshape(b, s, N_BRANCHES, d))
        merged = gates[:, :, 0] * jnp.dot(ya, w_branch_a[l]) + gates[:, :, 1] * jnp.dot(yb, w_branch_b[l])
        h = h + jnp.dot(merged, w_out[l])
        h = h + moe_ffn(rms_norm(h, norm2_g[l]), w_router[l], router_bias[l], we_gate[l], we_up[l],
                        we_down[l], ws_gate[l], ws_up[l], ws_down[l])
    return h
```

```python
import functools

import jax
import jax.numpy as jnp
from jax import lax
from jax.experimental import pallas as pl
from jax.experimental.pallas import tpu as pltpu

F32 = jnp.float32
BF16 = jnp.bfloat16
I32 = jnp.int32
U32 = jnp.uint32

D_MODEL = 1024
HEAD_DIM = 64
ROPE_THETA = 10000.0
EPS = 1e-6
GRID_W = 64
DILATIONS = (1, 4, 16)
RADIUS = 64
N_GROUPS_A = 3
A_GROUP_W = 768
A_OUT_W = 256
B_Q_W = 512
B_KV_W = 256
QKV_W = N_GROUPS_A * A_GROUP_W + B_Q_W + B_KV_W
N_EXPERTS = 64
N_EXPERT_GROUPS = 8
TOPK_GROUPS = 4
TOP_K = 8
EXPERT_FF = 256
ROUTED_SCALE = 2.5

LANES = 128
NEG_BIG = -1e30

TM_PROJ = 512
TQ_B = 256
TK_B = 512
TM_MERGE = 256
BM_EXPERT = 256
TM_COMBINE = 256
VMEM_LIMIT = 56 * 1024 * 1024


def _cparams(sem):
    return pltpu.CompilerParams(dimension_semantics=sem, vmem_limit_bytes=VMEM_LIMIT)


def _rope_tables(pos, dim):
    inv = ROPE_THETA ** (-jnp.arange(0, dim, 2, dtype=F32) / dim)
    ang = pos.astype(F32)[:, None] * inv[None, :]
    ang = jnp.concatenate([ang, ang], axis=-1)
    return jnp.cos(ang), jnp.sin(ang)


def _lane_tables(cos_h, sin_h, half):
    cos2 = jnp.concatenate([cos_h, cos_h], axis=-1)
    sin2 = jnp.concatenate([sin_h, sin_h], axis=-1)
    first = (jnp.arange(LANES) % (2 * half)) < half
    s_left = jnp.where(first[None, :], -sin2, 0.0)
    s_right = jnp.where(first[None, :], 0.0, sin2)
    return jnp.stack([cos2, s_left, s_right]).astype(F32)


def _proj_kernel(x_ref, g1_ref, w_ref, gain_ref, taba_ref, tabb_ref, bd_ref,
                 a0_ref, a1_ref, a2_ref, qb_ref, kvb_ref, pbuf_ref):
    x = x_ref[...]
    ms = jnp.mean(x * x, axis=-1, keepdims=True)
    xn = (x * lax.rsqrt(ms + EPS) * g1_ref[...]).astype(BF16)
    bd = bd_ref[...]
    tm = x.shape[0]

    def norm_rope(y, col, tab_ref, shift):
        sq = y * y
        hi = sq.astype(BF16)
        lo = (sq - hi.astype(F32)).astype(BF16)
        ss = jnp.dot(hi, bd, preferred_element_type=F32) + jnp.dot(lo, bd, preferred_element_type=F32)
        yn = y * lax.rsqrt(ss * (1.0 / HEAD_DIM) + EPS) * gain_ref[:, col:col + LANES]
        return (yn * tab_ref[0] + pltpu.roll(yn, LANES - shift, 1) * tab_ref[1]
                + pltpu.roll(yn, shift, 1) * tab_ref[2])

    for g, d in enumerate(DILATIONS):
        base = g * A_GROUP_W
        r = jnp.dot(xn, w_ref[:, base:base + A_GROUP_W], preferred_element_type=F32)
        for c in range(A_GROUP_W // LANES):
            y = r[:, c * LANES:(c + 1) * LANES]
            if c < 4:
                y = norm_rope(y, base + c * LANES, taba_ref, HEAD_DIM // 2)
            if d == 1:
                a0_ref[:, c * LANES:(c + 1) * LANES] = y.astype(BF16)
            else:
                pbuf_ref[c] = y
        if d > 1:
            out_ref = a1_ref if g == 1 else a2_ref
            rows = tm // d
            for res in range(d):
                for c in range(A_GROUP_W // LANES):
                    col = res * A_GROUP_W + c * LANES
                    out_ref[:, col:col + LANES] = pbuf_ref[c, pl.ds(res, rows, stride=d), :].astype(BF16)

    base = N_GROUPS_A * A_GROUP_W
    r = jnp.dot(xn, w_ref[:, base:base + B_Q_W + B_KV_W], preferred_element_type=F32)
    for c in range((B_Q_W + B_KV_W) // LANES):
        y = r[:, c * LANES:(c + 1) * LANES]
        if c < 5:
            y = norm_rope(y, base + c * LANES, tabb_ref, HEAD_DIM // 4)
        if c < 4:
            qb_ref[:, c * LANES:(c + 1) * LANES] = y.astype(BF16)
        else:
            kvb_ref[:, (c - 4) * LANES:(c - 3) * LANES] = y.astype(BF16)


def _proj_call(x2d, g1, wqkv, gain, taba, tabb, bd, seq):
    n = x2d.shape[0]
    tm = TM_PROJ
    tiles_per_seq = seq // tm
    full = lambda shape: pl.BlockSpec(shape, lambda i: (0,) * len(shape))
    tab_spec = pl.BlockSpec((3, tm, LANES), lambda i: (0, i % tiles_per_seq, 0))
    return pl.pallas_call(
        _proj_kernel,
        grid=(n // tm,),
        in_specs=[
            pl.BlockSpec((tm, D_MODEL), lambda i: (i, 0)),
            full((1, D_MODEL)),
            full((D_MODEL, QKV_W)),
            full((1, QKV_W)),
            tab_spec, tab_spec,
            full((LANES, LANES)),
        ],
        out_specs=[
            pl.BlockSpec((tm, A_GROUP_W), lambda i: (i, 0)),
            pl.BlockSpec((tm // 4, 4 * A_GROUP_W), lambda i: (i, 0)),
            pl.BlockSpec((tm // 16, 16 * A_GROUP_W), lambda i: (i, 0)),
            pl.BlockSpec((tm, B_Q_W), lambda i: (i, 0)),
            pl.BlockSpec((tm, B_KV_W), lambda i: (i, 0)),
        ],
        out_shape=[
            jax.ShapeDtypeStruct((n, A_GROUP_W), BF16),
            jax.ShapeDtypeStruct((n // 4, 4 * A_GROUP_W), BF16),
            jax.ShapeDtypeStruct((n // 16, 16 * A_GROUP_W), BF16),
            jax.ShapeDtypeStruct((n, B_Q_W), BF16),
            jax.ShapeDtypeStruct((n, B_KV_W), BF16),
        ],
        scratch_shapes=[pltpu.VMEM((A_GROUP_W // LANES, tm, LANES), F32)],
        compiler_params=_cparams(("arbitrary",)),
        name="proj_qkv",
    )(x2d, g1, wqkv, gain, taba, tabb, bd)


def _attn_a_kernel(a0_ref, a1_ref, a2_ref, ya_ref, acc_ref, m_ref, l_ref):
    tq = 128
    lane = lax.broadcasted_iota(I32, (tq, LANES), 1)
    low = lane < HEAD_DIM

    def tile(ref, col0, length, i, rows, first):
        win = min(2 * tq, length)
        if isinstance(i, int):
            q0 = i * tq
            ws = max(0, min(q0 - RADIUS, length - win))
        else:
            q0 = pl.multiple_of(i * tq, tq)
            ws = pl.multiple_of(jnp.clip(q0 - RADIUS, 0, length - win), RADIUS)
        qpos = q0 + lax.broadcasted_iota(I32, (tq, win), 0)
        kpos = ws + lax.broadcasted_iota(I32, (tq, win), 1)
        valid = jnp.abs(qpos - kpos) <= RADIUS
        for hp in range(2):
            q2 = ref[0, pl.ds(q0, tq), col0 + hp * LANES:col0 + (hp + 1) * LANES]
            k2 = ref[0, pl.ds(ws, win), col0 + 256 + hp * LANES:col0 + 256 + (hp + 1) * LANES]
            v2 = ref[0, pl.ds(ws, win), col0 + 512 + hp * LANES:col0 + 512 + (hp + 1) * LANES]
            pv, mm, ll = [], [], []
            for h in range(2):
                qm = jnp.where(low if h == 0 else ~low, q2, jnp.zeros_like(q2))
                s = lax.dot_general(qm, k2, (((1,), (1,)), ((), ())), preferred_element_type=F32)
                s = jnp.where(valid, s, NEG_BIG)
                m = jnp.max(s, axis=-1, keepdims=True)
                p = jnp.exp(s - m)
                ll.append(jnp.sum(p, axis=-1, keepdims=True))
                mm.append(m)
                pv.append(jnp.dot(p.astype(BF16), v2, preferred_element_type=F32))
            acc_t = jnp.where(low, pv[0], pv[1])
            m_t = jnp.where(low, mm[0], mm[1])
            l_t = jnp.where(low, ll[0], ll[1])
            if first:
                acc_ref[hp, rows, :] = acc_t
                m_ref[hp, rows, :] = m_t
                l_ref[hp, rows, :] = l_t
            else:
                m_o = m_ref[hp, rows, :]
                m_n = jnp.maximum(m_o, m_t)
                a = jnp.exp(m_o - m_n)
                b = jnp.exp(m_t - m_n)
                acc_ref[hp, rows, :] = a * acc_ref[hp, rows, :] + b * acc_t
                l_ref[hp, rows, :] = a * l_ref[hp, rows, :] + b * l_t
                m_ref[hp, rows, :] = m_n

    seq = a0_ref.shape[1]

    def g0_body(i, carry):
        tile(a0_ref, 0, seq, i, pl.ds(pl.multiple_of(i * tq, tq), tq), True)
        return carry
    lax.fori_loop(0, seq // tq, g0_body, 0)

    len1 = seq // 4
    for res in range(4):
        def g1_body(i, carry, res=res):
            tile(a1_ref, res * A_GROUP_W, len1, i, pl.ds(res + 4 * i * tq, tq, stride=4), False)
            return carry
        lax.fori_loop(0, len1 // tq, g1_body, 0)

    len2 = seq // 16
    for res in range(16):
        tile(a2_ref, res * A_GROUP_W, len2, 0, pl.ds(res, tq, stride=16), False)

    for hp in range(2):
        ya_ref[0, :, hp * LANES:(hp + 1) * LANES] = (acc_ref[hp] / l_ref[hp]).astype(BF16)


def _attn_a_call(a0, a1, a2):
    b, seq, _ = a0.shape
    return pl.pallas_call(
        _attn_a_kernel,
        grid=(b,),
        in_specs=[
            pl.BlockSpec((1, seq, A_GROUP_W), lambda i: (i, 0, 0)),
            pl.BlockSpec((1, seq // 4, 4 * A_GROUP_W), lambda i: (i, 0, 0)),
            pl.BlockSpec((1, seq // 16, 16 * A_GROUP_W), lambda i: (i, 0, 0)),
        ],
        out_specs=pl.BlockSpec((1, seq, A_OUT_W), lambda i: (i, 0, 0)),
        out_shape=jax.ShapeDtypeStruct((b, seq, A_OUT_W), BF16),
        scratch_shapes=[pltpu.VMEM((A_OUT_W // LANES, seq, LANES), F32)] * 3,
        compiler_params=_cparams(("arbitrary",)),
        name="attn_a",
    )(a0, a1, a2)


def _attn_b_kernel(q_ref, kv_ref, o_ref):
    tq = q_ref.shape[1]
    seq = kv_ref.shape[1]
    lane = lax.broadcasted_iota(I32, (tq, LANES), 1)
    low = lane < HEAD_DIM
    q2 = q_ref[0]
    zero = jnp.zeros_like(q2)
    qs = jnp.concatenate([jnp.where(low, q2, zero), jnp.where(low, zero, q2)], axis=0)
    m = jnp.full((2 * tq, 1), NEG_BIG, F32)
    l = jnp.zeros((2 * tq, 1), F32)
    acc = jnp.zeros((2 * tq, LANES), F32)
    for c in range(seq // TK_B):
        k2 = kv_ref[0, c * TK_B:(c + 1) * TK_B, 0:LANES]
        v2 = kv_ref[0, c * TK_B:(c + 1) * TK_B, LANES:2 * LANES]
        s = lax.dot_general(qs, k2, (((1,), (1,)), ((), ())), preferred_element_type=F32)
        m_n = jnp.maximum(m, jnp.max(s, axis=-1, keepdims=True))
        p = jnp.exp(s - m_n)
        a = jnp.exp(m - m_n)
        l = a * l + jnp.sum(p, axis=-1, keepdims=True)
        acc = a * acc + jnp.dot(p.astype(BF16), v2, preferred_element_type=F32)
        m = m_n
    o = acc / l
    o_ref[0] = jnp.where(low, o[:tq], o[tq:]).astype(BF16)


def _attn_b_call(qb, kvb):
    b, seq, _ = qb.shape
    return pl.pallas_call(
        _attn_b_kernel,
        grid=(b, B_Q_W // LANES, seq // TQ_B),
        in_specs=[
            pl.BlockSpec((1, TQ_B, LANES), lambda i, p, j: (i, j, p)),
            pl.BlockSpec((1, seq, B_KV_W), lambda i, p, j: (i, 0, 0)),
        ],
        out_specs=pl.BlockSpec((1, TQ_B, LANES), lambda i, p, j: (i, j, p)),
        out_shape=jax.ShapeDtypeStruct((b, seq, B_Q_W), BF16),
        compiler_params=_cparams(("arbitrary",) * 3),
        name="attn_b",
    )(qb, kvb)


def _pack_bf16_pairs(v):
    half = v.shape[1] // 2
    lo = lax.bitcast_convert_type(v[:, :half].astype(BF16).astype(F32), U32)
    hi = lax.bitcast_convert_type(v[:, half:].astype(BF16).astype(F32), U32)
    return (lo >> 16) | (hi & jnp.uint32(0xFFFF0000))


def _unpack_bf16_pairs(p):
    lo = lax.bitcast_convert_type(p << 16, F32)
    hi = lax.bitcast_convert_type(p & jnp.uint32(0xFFFF0000), F32)
    return lo, hi


def _merge_kernel(x_ref, ya_ref, yb_ref, g1_ref, wg_ref, wba_ref, wbb_ref, wo_ref, g2_ref,
                  wrh_ref, wrl_ref, bias_ref, wsgu_ref, wsd_ref, tri_ref,
                  base_ref, xp_ref, eidx_ref, rank_ref, wgt_ref, cnt_ref, carry_ref):
    step = pl.program_id(0)

    @pl.when(step == 0)
    def _():
        carry_ref[...] = jnp.zeros_like(carry_ref)

    x = x_ref[...]
    tm = x.shape[0]
    ms = jnp.mean(x * x, axis=-1, keepdims=True)
    xn = (x * lax.rsqrt(ms + EPS) * g1_ref[...]).astype(BF16)
    gates = jax.nn.sigmoid(jnp.dot(xn, wg_ref[...], preferred_element_type=F32))
    pa = jnp.dot(ya_ref[...], wba_ref[...], preferred_element_type=F32)
    pb = jnp.dot(yb_ref[...], wbb_ref[...], preferred_element_type=F32)
    merged = gates[:, :D_MODEL] * pa + gates[:, D_MODEL:] * pb
    h = x + jnp.dot(merged.astype(BF16), wo_ref[...], preferred_element_type=F32)

    ms2 = jnp.mean(h * h, axis=-1, keepdims=True)
    xn2 = h * lax.rsqrt(ms2 + EPS) * g2_ref[...]
    xn2b = xn2.astype(BF16)
    xp_ref[...] = _pack_bf16_pairs(xn2)

    gu = jnp.dot(xn2b, wsgu_ref[...], preferred_element_type=F32)
    hs = (jax.nn.silu(gu[:, :EXPERT_FF]) * gu[:, EXPERT_FF:]).astype(BF16)
    base_ref[...] = h + jnp.dot(hs, wsd_ref[...], preferred_element_type=F32)

    xlo = (xn2 - xn2b.astype(F32)).astype(BF16)
    logits = (jnp.dot(xn2b, wrh_ref[...], preferred_element_type=F32)
              + jnp.dot(xlo, wrh_ref[...], preferred_element_type=F32)
              + jnp.dot(xn2b, wrl_ref[...], preferred_element_type=F32))
    scores = jax.nn.sigmoid(logits.T[:N_EXPERTS, :])
    sel = scores + bias_ref[...]

    gsz = N_EXPERTS // N_EXPERT_GROUPS
    iota8 = lax.broadcasted_iota(I32, (gsz, tm), 0).astype(F32)
    ninf = jnp.float32(-jnp.inf)
    blocks, gs = [], []
    for g in range(N_EXPERT_GROUPS):
        blk = sel[g * gsz:(g + 1) * gsz, :]
        m1 = jnp.max(blk, axis=0, keepdims=True)
        first = jnp.min(jnp.where(blk == m1, iota8, float(gsz)), axis=0, keepdims=True)
        m2 = jnp.max(jnp.where(iota8 == first, ninf, blk), axis=0, keepdims=True)
        blocks.append(blk)
        gs.append(m1 + m2)
    cur = jnp.concatenate(gs, axis=0)
    gmask = jnp.zeros((N_EXPERT_GROUPS, tm), F32)
    for _ in range(TOPK_GROUPS):
        mx = jnp.max(cur, axis=0, keepdims=True)
        fi = jnp.min(jnp.where(cur == mx, iota8, float(N_EXPERT_GROUPS)), axis=0, keepdims=True)
        pick = iota8 == fi
        gmask = jnp.where(pick, 1.0, gmask)
        cur = jnp.where(pick, ninf, cur)
    cur = jnp.concatenate(
        [jnp.where(gmask[g:g + 1, :] > 0.5, blocks[g], ninf) for g in range(N_EXPERT_GROUPS)], axis=0)

    iota64 = lax.broadcasted_iota(I32, (N_EXPERTS, tm), 0).astype(F32)
    idxs, wks = [], []
    onehot = jnp.zeros((N_EXPERTS, tm), F32)
    for _ in range(TOP_K):
        mx = jnp.max(cur, axis=0, keepdims=True)
        fi = jnp.min(jnp.where(cur == mx, iota64, float(N_EXPERTS)), axis=0, keepdims=True)
        pick = iota64 == fi
        wks.append(jnp.sum(jnp.where(pick, scores, 0.0), axis=0, keepdims=True))
        idxs.append(fi)
        onehot = jnp.where(pick, 1.0, onehot)
        cur = jnp.where(pick, ninf, cur)
    wk = jnp.concatenate(wks, axis=0)
    wgt_ref[...] = wk / jnp.sum(wk, axis=0, keepdims=True) * ROUTED_SCALE
    eidx_ref[...] = jnp.concatenate(idxs, axis=0).astype(I32)

    before = jnp.dot(onehot.astype(BF16), tri_ref[...], preferred_element_type=F32) + carry_ref[:, 0:1]
    rank_ref[...] = jnp.concatenate(
        [jnp.sum(jnp.where(iota64 == fi, before, 0.0), axis=0, keepdims=True) for fi in idxs],
        axis=0).astype(I32)
    carry_ref[...] = carry_ref[...] + jnp.sum(onehot, axis=1, keepdims=True)
    cnt_ref[...] = carry_ref[...]


def _merge_call(x2d, ya, yb, g1, wg, wba, wbb, wo, g2, wrh, wrl, bias, wsgu, wsd, tri):
    n = x2d.shape[0]
    tm = TM_MERGE
    full = lambda shape: pl.BlockSpec(shape, lambda i: (0,) * len(shape))
    row = lambda w: pl.BlockSpec((tm, w), lambda i: (i, 0))
    col = lambda: pl.BlockSpec((TOP_K, tm), lambda i: (0, i))
    return pl.pallas_call(
        _merge_kernel,
        grid=(n // tm,),
        in_specs=[
            row(D_MODEL), row(A_OUT_W), row(B_Q_W),
            full((1, D_MODEL)), full((D_MODEL, 2 * D_MODEL)),
            full((A_OUT_W, D_MODEL)), full((B_Q_W, D_MODEL)), full((D_MODEL, D_MODEL)),
            full((1, D_MODEL)),
            full((D_MODEL, LANES)), full((D_MODEL, LANES)), full((N_EXPERTS, 1)),
            full((D_MODEL, 2 * EXPERT_FF)), full((EXPERT_FF, D_MODEL)),
            full((tm, tm)),
        ],
        out_specs=[row(D_MODEL), row(D_MODEL // 2), col(), col(), col(), full((N_EXPERTS, LANES))],
        out_shape=[
            jax.ShapeDtypeStruct((n, D_MODEL), F32),
            jax.ShapeDtypeStruct((n, D_MODEL // 2), U32),
            jax.ShapeDtypeStruct((TOP_K, n), I32),
            jax.ShapeDtypeStruct((TOP_K, n), I32),
            jax.ShapeDtypeStruct((TOP_K, n), F32),
            jax.ShapeDtypeStruct((N_EXPERTS, LANES), F32),
        ],
        scratch_shapes=[pltpu.VMEM((N_EXPERTS, LANES), F32)],
        compiler_params=_cparams(("arbitrary",)),
        name="merge_router",
    )(x2d, ya, yb, g1, wg, wba, wbb, wo, g2, wrh, wrl, bias, wsgu, wsd, tri)


def _expert_kernel(be_ref, nb_ref, xs_ref, wgu_ref, wd_ref, ys_ref):
    @pl.when(pl.program_id(0) >= nb_ref[0])
    def _():
        ys_ref[...] = jnp.zeros_like(ys_ref)

    @pl.when(pl.program_id(0) < nb_ref[0])
    def _():
        lo, hi = _unpack_bf16_pairs(xs_ref[...])
        half = D_MODEL // 2
        gu = (jnp.dot(lo.astype(BF16), wgu_ref[0, :half, :], preferred_element_type=F32)
              + jnp.dot(hi.astype(BF16), wgu_ref[0, half:, :], preferred_element_type=F32))
        hb = (jax.nn.silu(gu[:, :EXPERT_FF]) * gu[:, EXPERT_FF:]).astype(BF16)
        ys_ref[...] = _pack_bf16_pairs(jnp.dot(hb, wd_ref[0], preferred_element_type=F32))


def _expert_call(block_e, n_used, xs, wgu, wd):
    p_len = xs.shape[0]
    bm = BM_EXPERT
    return pl.pallas_call(
        _expert_kernel,
        grid_spec=pltpu.PrefetchScalarGridSpec(
            num_scalar_prefetch=2,
            grid=(p_len // bm,),
            in_specs=[
                pl.BlockSpec((bm, D_MODEL // 2), lambda j, be, nb: (j, 0)),
                pl.BlockSpec((1, D_MODEL, 2 * EXPERT_FF), lambda j, be, nb: (be[j], 0, 0)),
                pl.BlockSpec((1, EXPERT_FF, D_MODEL), lambda j, be, nb: (be[j], 0, 0)),
            ],
            out_specs=pl.BlockSpec((bm, D_MODEL // 2), lambda j, be, nb: (j, 0)),
        ),
        out_shape=jax.ShapeDtypeStruct((p_len, D_MODEL // 2), U32),
        compiler_params=_cparams(("arbitrary",)),
        name="routed_experts",
    )(block_e, n_used, xs, wgu, wd)


def _combine_kernel(base_ref, g_ref, w_ref, o_ref):
    half = D_MODEL // 2
    lo_acc = base_ref[:, :half]
    hi_acc = base_ref[:, half:]
    for k in range(TOP_K):
        lo, hi = _unpack_bf16_pairs(g_ref[k])
        wk = w_ref[:, k:k + 1]
        lo_acc = lo_acc + wk * lo
        hi_acc = hi_acc + wk * hi
    o_ref[:, :half] = lo_acc
    o_ref[:, half:] = hi_acc


def _combine_call(base, gathered, w_t):
    n = base.shape[0]
    tm = TM_COMBINE
    return pl.pallas_call(
        _combine_kernel,
        grid=(n // tm,),
        in_specs=[
            pl.BlockSpec((tm, D_MODEL), lambda i: (i, 0)),
            pl.BlockSpec((TOP_K, tm, D_MODEL // 2), lambda i: (0, i, 0)),
            pl.BlockSpec((tm, TOP_K), lambda i: (i, 0)),
        ],
        out_specs=pl.BlockSpec((tm, D_MODEL), lambda i: (i, 0)),
        out_shape=jax.ShapeDtypeStruct((n, D_MODEL), F32),
        compiler_params=_cparams(("arbitrary",)),
        name="moe_combine",
    )(base, gathered, w_t)


def _prep_tables(seq):
    pos = jnp.arange(seq)
    cos1, sin1 = _rope_tables(pos, HEAD_DIM)
    cos_r, sin_r = _rope_tables(pos // GRID_W, HEAD_DIM // 2)
    cos_c, sin_c = _rope_tables(pos % GRID_W, HEAD_DIM // 2)
    taba = _lane_tables(cos1, sin1, HEAD_DIM // 2)
    tabb = _lane_tables(jnp.concatenate([cos_r, cos_c], -1), jnp.concatenate([sin_r, sin_c], -1),
                        HEAD_DIM // 4)
    return taba, tabb


def _layer(h, p, taba, tabb):
    b, seq, d = h.shape
    n = b * seq
    x2d = h.reshape(n, d)
    w_in = p["w_in"]
    o1 = N_GROUPS_A * A_GROUP_W
    o2 = o1 + B_Q_W
    o3 = o2 + B_KV_W
    pair_heads = jnp.array([0, 4, 1, 5, 2, 6, 3, 7])
    pair_cols = (pair_heads[:, None] * HEAD_DIM + jnp.arange(HEAD_DIM)[None, :]).reshape(-1)
    wqkv = jnp.concatenate([w_in[:, :o1], w_in[:, o1:o2][:, pair_cols], w_in[:, o2:o3]], axis=1).astype(BF16)
    scale = HEAD_DIM ** -0.5
    qa = jnp.tile(p["q_norm_a"] * scale, 4)
    ka = jnp.tile(p["k_norm_a"], 4)
    ones = jnp.ones((A_OUT_W,), F32)
    gain = jnp.concatenate([jnp.concatenate([qa, ka, ones])] * N_GROUPS_A
                           + [jnp.tile(p["q_norm_b"] * scale, 8), jnp.tile(p["k_norm_b"], 2),
                              jnp.ones((LANES,), F32)]).reshape(1, QKV_W).astype(F32)
    seg = jnp.arange(LANES) // HEAD_DIM
    bd = (seg[:, None] == seg[None, :]).astype(BF16)
    g1 = p["norm1_g"].reshape(1, d).astype(F32)

    a0, a1, a2, qb, kvb = _proj_call(x2d, g1, wqkv, gain, taba, tabb, bd, seq)
    ya = _attn_a_call(a0.reshape(b, seq, A_GROUP_W), a1.reshape(b, seq // 4, 4 * A_GROUP_W),
                      a2.reshape(b, seq // 16, 16 * A_GROUP_W))
    yb = _attn_b_call(qb.reshape(b, seq, B_Q_W), kvb.reshape(b, seq, B_KV_W))

    wg = w_in[:, o3:].astype(BF16)
    wbb = p["w_branch_b"][pair_cols, :].astype(BF16)
    wr = jnp.pad(p["w_router"], ((0, 0), (0, LANES - N_EXPERTS)))
    wrh = wr.astype(BF16)
    wrl = (wr - wrh.astype(F32)).astype(BF16)
    wsgu = jnp.concatenate([p["ws_gate"], p["ws_up"]], axis=1).astype(BF16)
    tm = TM_MERGE
    tri = (jnp.arange(tm)[:, None] < jnp.arange(tm)[None, :]).astype(BF16)
    base, xp, eidx, rank, wgt, cnt = _merge_call(
        x2d, ya.reshape(n, A_OUT_W), yb.reshape(n, B_Q_W), g1, wg,
        p["w_branch_a"].astype(BF16), wbb, p["w_out"].astype(BF16),
        p["norm2_g"].reshape(1, d).astype(F32), wrh, wrl,
        p["router_bias"].reshape(N_EXPERTS, 1).astype(F32), wsgu, p["ws_down"].astype(BF16), tri)

    bm = BM_EXPERT
    counts = cnt[:, 0].astype(I32)
    padded = (counts + bm - 1) // bm * bm
    pend = jnp.cumsum(padded)
    pstart = pend - padded
    n_blocks = (n * TOP_K + N_EXPERTS * (bm - 1)) // bm + 1
    p_len = n_blocks * bm
    onehot = eidx[:, :, None] == jnp.arange(N_EXPERTS)[None, None, :]
    dest = jnp.sum(jnp.where(onehot, pstart[None, None, :], 0), axis=-1) + rank
    block_e = jnp.minimum(jnp.searchsorted(pend, jnp.arange(n_blocks) * bm, side="right"),
                          N_EXPERTS - 1).astype(I32)
    n_used = (pend[-1] // bm).astype(I32).reshape(1)

    tok = jnp.broadcast_to(jnp.arange(n, dtype=I32)[None, :], (TOP_K, n))
    buf_tok = jnp.zeros((p_len,), I32).at[dest.reshape(-1)].set(tok.reshape(-1))
    xs = xp[buf_tok]

    wgu = jnp.concatenate([p["we_gate"], p["we_up"]], axis=-1).astype(BF16)
    ys = _expert_call(block_e, n_used, xs, wgu, p["we_down"].astype(BF16))

    gathered = ys[dest]
    out = _combine_call(base, gathered, wgt.T)
    return out.reshape(b, seq, d)


def kernel(x, norm1_g, w_in, q_norm_a, k_norm_a, q_norm_b, k_norm_b, w_branch_a, w_branch_b, w_out,
           norm2_g, w_router, router_bias, we_gate, we_up, we_down, ws_gate, ws_up, ws_down):
    params = dict(norm1_g=norm1_g, w_in=w_in, q_norm_a=q_norm_a, k_norm_a=k_norm_a, q_norm_b=q_norm_b,
                  k_norm_b=k_norm_b, w_branch_a=w_branch_a, w_branch_b=w_branch_b, w_out=w_out,
                  norm2_g=norm2_g, w_router=w_router, router_bias=router_bias, we_gate=we_gate,
                  we_up=we_up, we_down=we_down, ws_gate=ws_gate, ws_up=ws_up, ws_down=ws_down)
    taba, tabb = _prep_tables(x.shape[1])
    h = x
    for l in range(norm1_g.shape[0]):
        h = _layer(h, {k: v[l] for k, v in params.items()}, taba, tabb)
    return h
```

```python
import functools

import jax
import jax.numpy as jnp
from jax import lax
from jax.experimental import pallas as pl
from jax.experimental.pallas import tpu as pltpu
from jax.experimental.pallas import tpu_sc as plsc

F32 = jnp.float32
BF16 = jnp.bfloat16
I32 = jnp.int32
U32 = jnp.uint32

D_MODEL = 1024
HEAD_DIM = 64
ROPE_THETA = 10000.0
EPS = 1e-6
GRID_W = 64
DILATIONS = (1, 4, 16)
RADIUS = 64
N_GROUPS_A = 3
A_GROUP_W = 768
A_OUT_W = 256
B_Q_W = 512
B_KV_W = 256
QKV_W = N_GROUPS_A * A_GROUP_W + B_Q_W + B_KV_W
N_EXPERTS = 64
N_EXPERT_GROUPS = 8
TOPK_GROUPS = 4
TOP_K = 8
EXPERT_FF = 256
ROUTED_SCALE = 2.5

LANES = 128
NEG_BIG = -1e30

TM_PROJ = 512
TQ_B = 256
TK_B = 512
TM_MERGE = 256
BM_EXPERT = 256
TM_COMBINE = 256
VMEM_LIMIT = 56 * 1024 * 1024


def _cparams(sem):
    return pltpu.CompilerParams(dimension_semantics=sem, vmem_limit_bytes=VMEM_LIMIT)


def _rope_tables(pos, dim):
    inv = ROPE_THETA ** (-jnp.arange(0, dim, 2, dtype=F32) / dim)
    ang = pos.astype(F32)[:, None] * inv[None, :]
    ang = jnp.concatenate([ang, ang], axis=-1)
    return jnp.cos(ang), jnp.sin(ang)


def _lane_tables(cos_h, sin_h, half):
    cos2 = jnp.concatenate([cos_h, cos_h], axis=-1)
    sin2 = jnp.concatenate([sin_h, sin_h], axis=-1)
    first = (jnp.arange(LANES) % (2 * half)) < half
    s_left = jnp.where(first[None, :], -sin2, 0.0)
    s_right = jnp.where(first[None, :], 0.0, sin2)
    return jnp.stack([cos2, s_left, s_right]).astype(F32)


def _proj_kernel(x_ref, g1_ref, w_ref, gain_ref, taba_ref, tabb_ref, bd_ref,
                 a0_ref, a1_ref, a2_ref, qb_ref, kvb_ref, pbuf_ref):
    x = x_ref[...]
    ms = jnp.mean(x * x, axis=-1, keepdims=True)
    xn = (x * lax.rsqrt(ms + EPS) * g1_ref[...]).astype(BF16)
    bd = bd_ref[...]
    tm = x.shape[0]

    def norm_rope(y, col, tab_ref, shift):
        sq = y * y
        hi = sq.astype(BF16)
        lo = (sq - hi.astype(F32)).astype(BF16)
        ss = jnp.dot(hi, bd, preferred_element_type=F32) + jnp.dot(lo, bd, preferred_element_type=F32)
        yn = y * lax.rsqrt(ss * (1.0 / HEAD_DIM) + EPS) * gain_ref[:, col:col + LANES]
        return (yn * tab_ref[0] + pltpu.roll(yn, LANES - shift, 1) * tab_ref[1]
                + pltpu.roll(yn, shift, 1) * tab_ref[2])

    for g, d in enumerate(DILATIONS):
        base = g * A_GROUP_W
        r = jnp.dot(xn, w_ref[:, base:base + A_GROUP_W], preferred_element_type=F32)
        for c in range(A_GROUP_W // LANES):
            y = r[:, c * LANES:(c + 1) * LANES]
            if c < 4:
                y = norm_rope(y, base + c * LANES, taba_ref, HEAD_DIM // 2)
            if d == 1:
                a0_ref[:, c * LANES:(c + 1) * LANES] = y.astype(BF16)
            else:
                pbuf_ref[c] = y
        if d > 1:
            out_ref = a1_ref if g == 1 else a2_ref
            rows = tm // d
            for res in range(d):
                for c in range(A_GROUP_W // LANES):
                    col = res * A_GROUP_W + c * LANES
                    out_ref[:, col:col + LANES] = pbuf_ref[c, pl.ds(res, rows, stride=d), :].astype(BF16)

    base = N_GROUPS_A * A_GROUP_W
    r = jnp.dot(xn, w_ref[:, base:base + B_Q_W + B_KV_W], preferred_element_type=F32)
    for c in range((B_Q_W + B_KV_W) // LANES):
        y = r[:, c * LANES:(c + 1) * LANES]
        if c < 5:
            y = norm_rope(y, base + c * LANES, tabb_ref, HEAD_DIM // 4)
        if c < 4:
            qb_ref[:, c * LANES:(c + 1) * LANES] = y.astype(BF16)
        else:
            kvb_ref[:, (c - 4) * LANES:(c - 3) * LANES] = y.astype(BF16)


def _proj_call(x2d, g1, wqkv, gain, taba, tabb, bd, seq):
    n = x2d.shape[0]
    tm = TM_PROJ
    tiles_per_seq = seq // tm
    full = lambda shape: pl.BlockSpec(shape, lambda i: (0,) * len(shape))
    tab_spec = pl.BlockSpec((3, tm, LANES), lambda i: (0, i % tiles_per_seq, 0))
    return pl.pallas_call(
        _proj_kernel,
        grid=(n // tm,),
        in_specs=[
            pl.BlockSpec((tm, D_MODEL), lambda i: (i, 0)),
            full((1, D_MODEL)),
            full((D_MODEL, QKV_W)),
            full((1, QKV_W)),
            tab_spec, tab_spec,
            full((LANES, LANES)),
        ],
        out_specs=[
            pl.BlockSpec((tm, A_GROUP_W), lambda i: (i, 0)),
            pl.BlockSpec((tm // 4, 4 * A_GROUP_W), lambda i: (i, 0)),
            pl.BlockSpec((tm // 16, 16 * A_GROUP_W), lambda i: (i, 0)),
            pl.BlockSpec((tm, B_Q_W), lambda i: (i, 0)),
            pl.BlockSpec((tm, B_KV_W), lambda i: (i, 0)),
        ],
        out_shape=[
            jax.ShapeDtypeStruct((n, A_GROUP_W), BF16),
            jax.ShapeDtypeStruct((n // 4, 4 * A_GROUP_W), BF16),
            jax.ShapeDtypeStruct((n // 16, 16 * A_GROUP_W), BF16),
            jax.ShapeDtypeStruct((n, B_Q_W), BF16),
            jax.ShapeDtypeStruct((n, B_KV_W), BF16),
        ],
        scratch_shapes=[pltpu.VMEM((A_GROUP_W // LANES, tm, LANES), F32)],
        compiler_params=_cparams(("arbitrary",)),
        name="proj_qkv",
    )(x2d, g1, wqkv, gain, taba, tabb, bd)


def _attn_a_kernel(a0_ref, a1_ref, a2_ref, ya_ref, acc_ref, m_ref, l_ref):
    tq = 128
    lane = lax.broadcasted_iota(I32, (tq, LANES), 1)
    low = lane < HEAD_DIM

    def tile(ref, col0, length, i, rows, first):
        win = min(2 * tq, length)
        if isinstance(i, int):
            q0 = i * tq
            ws = max(0, min(q0 - RADIUS, length - win))
        else:
            q0 = pl.multiple_of(i * tq, tq)
            ws = pl.multiple_of(jnp.clip(q0 - RADIUS, 0, length - win), RADIUS)
        qpos = q0 + lax.broadcasted_iota(I32, (tq, win), 0)
        kpos = ws + lax.broadcasted_iota(I32, (tq, win), 1)
        valid = jnp.abs(qpos - kpos) <= RADIUS
        for hp in range(2):
            q2 = ref[0, pl.ds(q0, tq), col0 + hp * LANES:col0 + (hp + 1) * LANES]
            k2 = ref[0, pl.ds(ws, win), col0 + 256 + hp * LANES:col0 + 256 + (hp + 1) * LANES]
            v2 = ref[0, pl.ds(ws, win), col0 + 512 + hp * LANES:col0 + 512 + (hp + 1) * LANES]
            pv, mm, ll = [], [], []
            for h in range(2):
                qm = jnp.where(low if h == 0 else ~low, q2, jnp.zeros_like(q2))
                s = lax.dot_general(qm, k2, (((1,), (1,)), ((), ())), preferred_element_type=F32)
                s = jnp.where(valid, s, NEG_BIG)
                m = jnp.max(s, axis=-1, keepdims=True)
                p = jnp.exp(s - m)
                ll.append(jnp.sum(p, axis=-1, keepdims=True))
                mm.append(m)
                pv.append(jnp.dot(p.astype(BF16), v2, preferred_element_type=F32))
            acc_t = jnp.where(low, pv[0], pv[1])
            m_t = jnp.where(low, mm[0], mm[1])
            l_t = jnp.where(low, ll[0], ll[1])
            if first:
                acc_ref[hp, rows, :] = acc_t
                m_ref[hp, rows, :] = m_t
                l_ref[hp, rows, :] = l_t
            else:
                m_o = m_ref[hp, rows, :]
                m_n = jnp.maximum(m_o, m_t)
                a = jnp.exp(m_o - m_n)
                b = jnp.exp(m_t - m_n)
                acc_ref[hp, rows, :] = a * acc_ref[hp, rows, :] + b * acc_t
                l_ref[hp, rows, :] = a * l_ref[hp, rows, :] + b * l_t
                m_ref[hp, rows, :] = m_n

    seq = a0_ref.shape[1]

    def g0_body(i, carry):
        tile(a0_ref, 0, seq, i, pl.ds(pl.multiple_of(i * tq, tq), tq), True)
        return carry
    lax.fori_loop(0, seq // tq, g0_body, 0)

    len1 = seq // 4
    for res in range(4):
        def g1_body(i, carry, res=res):
            tile(a1_ref, res * A_GROUP_W, len1, i, pl.ds(res + 4 * i * tq, tq, stride=4), False)
            return carry
        lax.fori_loop(0, len1 // tq, g1_body, 0)

    len2 = seq // 16
    for res in range(16):
        tile(a2_ref, res * A_GROUP_W, len2, 0, pl.ds(res, tq, stride=16), False)

    for hp in range(2):
        ya_ref[0, :, hp * LANES:(hp + 1) * LANES] = (acc_ref[hp] / l_ref[hp]).astype(BF16)


def _attn_a_call(a0, a1, a2):
    b, seq, _ = a0.shape
    return pl.pallas_call(
        _attn_a_kernel,
        grid=(b,),
        in_specs=[
            pl.BlockSpec((1, seq, A_GROUP_W), lambda i: (i, 0, 0)),
            pl.BlockSpec((1, seq // 4, 4 * A_GROUP_W), lambda i: (i, 0, 0)),
            pl.BlockSpec((1, seq // 16, 16 * A_GROUP_W), lambda i: (i, 0, 0)),
        ],
        out_specs=pl.BlockSpec((1, seq, A_OUT_W), lambda i: (i, 0, 0)),
        out_shape=jax.ShapeDtypeStruct((b, seq, A_OUT_W), BF16),
        scratch_shapes=[pltpu.VMEM((A_OUT_W // LANES, seq, LANES), F32)] * 3,
        compiler_params=_cparams(("arbitrary",)),
        name="attn_a",
    )(a0, a1, a2)


def _attn_b_kernel(q_ref, kv_ref, o_ref):
    tq = q_ref.shape[1]
    seq = kv_ref.shape[1]
    lane = lax.broadcasted_iota(I32, (tq, LANES), 1)
    low = lane < HEAD_DIM
    q2 = q_ref[0]
    zero = jnp.zeros_like(q2)
    qs = jnp.concatenate([jnp.where(low, q2, zero), jnp.where(low, zero, q2)], axis=0)
    m = jnp.full((2 * tq, 1), NEG_BIG, F32)
    l = jnp.zeros((2 * tq, 1), F32)
    acc = jnp.zeros((2 * tq, LANES), F32)
    for c in range(seq // TK_B):
        k2 = kv_ref[0, c * TK_B:(c + 1) * TK_B, 0:LANES]
        v2 = kv_ref[0, c * TK_B:(c + 1) * TK_B, LANES:2 * LANES]
        s = lax.dot_general(qs, k2, (((1,), (1,)), ((), ())), preferred_element_type=F32)
        m_n = jnp.maximum(m, jnp.max(s, axis=-1, keepdims=True))
        p = jnp.exp(s - m_n)
        a = jnp.exp(m - m_n)
        l = a * l + jnp.sum(p, axis=-1, keepdims=True)
        acc = a * acc + jnp.dot(p.astype(BF16), v2, preferred_element_type=F32)
        m = m_n
    o = acc / l
    o_ref[0] = jnp.where(low, o[:tq], o[tq:]).astype(BF16)


def _attn_b_call(qb, kvb):
    b, seq, _ = qb.shape
    return pl.pallas_call(
        _attn_b_kernel,
        grid=(b, B_Q_W // LANES, seq // TQ_B),
        in_specs=[
            pl.BlockSpec((1, TQ_B, LANES), lambda i, p, j: (i, j, p)),
            pl.BlockSpec((1, seq, B_KV_W), lambda i, p, j: (i, 0, 0)),
        ],
        out_specs=pl.BlockSpec((1, TQ_B, LANES), lambda i, p, j: (i, j, p)),
        out_shape=jax.ShapeDtypeStruct((b, seq, B_Q_W), BF16),
        compiler_params=_cparams(("arbitrary",) * 3),
        name="attn_b",
    )(qb, kvb)


def _pack_bf16_pairs(v):
    half = v.shape[1] // 2
    lo = lax.bitcast_convert_type(v[:, :half].astype(BF16).astype(F32), U32)
    hi = lax.bitcast_convert_type(v[:, half:].astype(BF16).astype(F32), U32)
    return (lo >> 16) | (hi & jnp.uint32(0xFFFF0000))


def _unpack_bf16_pairs(p):
    lo = lax.bitcast_convert_type(p << 16, F32)
    hi = lax.bitcast_convert_type(p & jnp.uint32(0xFFFF0000), F32)
    return lo, hi


def _merge_kernel(x_ref, ya_ref, yb_ref, g1_ref, wg_ref, wba_ref, wbb_ref, wo_ref, g2_ref,
                  wrh_ref, wrl_ref, bias_ref, wsgu_ref, wsd_ref, tri_ref,
                  base_ref, xp_ref, eidx_ref, rank_ref, wgt_ref, cnt_ref, carry_ref):
    step = pl.program_id(0)

    @pl.when(step == 0)
    def _():
        carry_ref[...] = jnp.zeros_like(carry_ref)

    x = x_ref[...]
    tm = x.shape[0]
    ms = jnp.mean(x * x, axis=-1, keepdims=True)
    xn = (x * lax.rsqrt(ms + EPS) * g1_ref[...]).astype(BF16)
    gates = jax.nn.sigmoid(jnp.dot(xn, wg_ref[...], preferred_element_type=F32))
    pa = jnp.dot(ya_ref[...], wba_ref[...], preferred_element_type=F32)
    pb = jnp.dot(yb_ref[...], wbb_ref[...], preferred_element_type=F32)
    merged = gates[:, :D_MODEL] * pa + gates[:, D_MODEL:] * pb
    h = x + jnp.dot(merged.astype(BF16), wo_ref[...], preferred_element_type=F32)

    ms2 = jnp.mean(h * h, axis=-1, keepdims=True)
    xn2 = h * lax.rsqrt(ms2 + EPS) * g2_ref[...]
    xn2b = xn2.astype(BF16)
    xp_ref[...] = _pack_bf16_pairs(xn2)

    gu = jnp.dot(xn2b, wsgu_ref[...], preferred_element_type=F32)
    hs = (jax.nn.silu(gu[:, :EXPERT_FF]) * gu[:, EXPERT_FF:]).astype(BF16)
    base_ref[...] = h + jnp.dot(hs, wsd_ref[...], preferred_element_type=F32)

    xlo = (xn2 - xn2b.astype(F32)).astype(BF16)
    logits = (jnp.dot(xn2b, wrh_ref[...], preferred_element_type=F32)
              + jnp.dot(xlo, wrh_ref[...], preferred_element_type=F32)
              + jnp.dot(xn2b, wrl_ref[...], preferred_element_type=F32))
    scores = jax.nn.sigmoid(logits.T[:N_EXPERTS, :])
    sel = scores + bias_ref[...]

    gsz = N_EXPERTS // N_EXPERT_GROUPS
    iota8 = lax.broadcasted_iota(I32, (gsz, tm), 0).astype(F32)
    ninf = jnp.float32(-jnp.inf)
    blocks, gs = [], []
    for g in range(N_EXPERT_GROUPS):
        blk = sel[g * gsz:(g + 1) * gsz, :]
        m1 = jnp.max(blk, axis=0, keepdims=True)
        first = jnp.min(jnp.where(blk == m1, iota8, float(gsz)), axis=0, keepdims=True)
        m2 = jnp.max(jnp.where(iota8 == first, ninf, blk), axis=0, keepdims=True)
        blocks.append(blk)
        gs.append(m1 + m2)
    cur = jnp.concatenate(gs, axis=0)
    gmask = jnp.zeros((N_EXPERT_GROUPS, tm), F32)
    for _ in range(TOPK_GROUPS):
        mx = jnp.max(cur, axis=0, keepdims=True)
        fi = jnp.min(jnp.where(cur == mx, iota8, float(N_EXPERT_GROUPS)), axis=0, keepdims=True)
        pick = iota8 == fi
        gmask = jnp.where(pick, 1.0, gmask)
        cur = jnp.where(pick, ninf, cur)
    cur = jnp.concatenate(
        [jnp.where(gmask[g:g + 1, :] > 0.5, blocks[g], ninf) for g in range(N_EXPERT_GROUPS)], axis=0)

    iota64 = lax.broadcasted_iota(I32, (N_EXPERTS, tm), 0).astype(F32)
    idxs, wks = [], []
    onehot = jnp.zeros((N_EXPERTS, tm), F32)
    for _ in range(TOP_K):
        mx = jnp.max(cur, axis=0, keepdims=True)
        fi = jnp.min(jnp.where(cur == mx, iota64, float(N_EXPERTS)), axis=0, keepdims=True)
        pick = iota64 == fi
        wks.append(jnp.sum(jnp.where(pick, scores, 0.0), axis=0, keepdims=True))
        idxs.append(fi)
        onehot = jnp.where(pick, 1.0, onehot)
        cur = jnp.where(pick, ninf, cur)
    wk = jnp.concatenate(wks, axis=0)
    wgt_ref[...] = wk / jnp.sum(wk, axis=0, keepdims=True) * ROUTED_SCALE
    eidx_ref[...] = jnp.concatenate(idxs, axis=0).astype(I32)

    before = jnp.dot(onehot.astype(BF16), tri_ref[...], preferred_element_type=F32) + carry_ref[:, 0:1]
    rank_ref[...] = jnp.concatenate(
        [jnp.sum(jnp.where(iota64 == fi, before, 0.0), axis=0, keepdims=True) for fi in idxs],
        axis=0).astype(I32)
    carry_ref[...] = carry_ref[...] + jnp.sum(onehot, axis=1, keepdims=True)
    cnt_ref[...] = carry_ref[...]


def _merge_call(x2d, ya, yb, g1, wg, wba, wbb, wo, g2, wrh, wrl, bias, wsgu, wsd, tri):
    n = x2d.shape[0]
    tm = TM_MERGE
    full = lambda shape: pl.BlockSpec(shape, lambda i: (0,) * len(shape))
    row = lambda w: pl.BlockSpec((tm, w), lambda i: (i, 0))
    col = lambda: pl.BlockSpec((TOP_K, tm), lambda i: (0, i))
    return pl.pallas_call(
        _merge_kernel,
        grid=(n // tm,),
        in_specs=[
            row(D_MODEL), row(A_OUT_W), row(B_Q_W),
            full((1, D_MODEL)), full((D_MODEL, 2 * D_MODEL)),
            full((A_OUT_W, D_MODEL)), full((B_Q_W, D_MODEL)), full((D_MODEL, D_MODEL)),
            full((1, D_MODEL)),
            full((D_MODEL, LANES)), full((D_MODEL, LANES)), full((N_EXPERTS, 1)),
            full((D_MODEL, 2 * EXPERT_FF)), full((EXPERT_FF, D_MODEL)),
            full((tm, tm)),
        ],
        out_specs=[row(D_MODEL), row(D_MODEL // 2), col(), col(), col(), full((N_EXPERTS, LANES))],
        out_shape=[
            jax.ShapeDtypeStruct((n, D_MODEL), F32),
            jax.ShapeDtypeStruct((n, D_MODEL // 2), U32),
            jax.ShapeDtypeStruct((TOP_K, n), I32),
            jax.ShapeDtypeStruct((TOP_K, n), I32),
            jax.ShapeDtypeStruct((TOP_K, n), F32),
            jax.ShapeDtypeStruct((N_EXPERTS, LANES), F32),
        ],
        scratch_shapes=[pltpu.VMEM((N_EXPERTS, LANES), F32)],
        compiler_params=_cparams(("arbitrary",)),
        name="merge_router",
    )(x2d, ya, yb, g1, wg, wba, wbb, wo, g2, wrh, wrl, bias, wsgu, wsd, tri)


def _expert_kernel(be_ref, nb_ref, xs_ref, wgu_ref, wd_ref, ys_ref):
    @pl.when(pl.program_id(0) >= nb_ref[0])
    def _():
        ys_ref[...] = jnp.zeros_like(ys_ref)

    @pl.when(pl.program_id(0) < nb_ref[0])
    def _():
        lo, hi = _unpack_bf16_pairs(xs_ref[...])
        half = D_MODEL // 2
        gu = (jnp.dot(lo.astype(BF16), wgu_ref[0, :half, :], preferred_element_type=F32)
              + jnp.dot(hi.astype(BF16), wgu_ref[0, half:, :], preferred_element_type=F32))
        hb = (jax.nn.silu(gu[:, :EXPERT_FF]) * gu[:, EXPERT_FF:]).astype(BF16)
        ys_ref[...] = _pack_bf16_pairs(jnp.dot(hb, wd_ref[0], preferred_element_type=F32))


def _expert_call(block_e, n_used, xs, wgu, wd):
    p_len = xs.shape[0]
    bm = BM_EXPERT
    return pl.pallas_call(
        _expert_kernel,
        grid_spec=pltpu.PrefetchScalarGridSpec(
            num_scalar_prefetch=2,
            grid=(p_len // bm,),
            in_specs=[
                pl.BlockSpec((bm, D_MODEL // 2), lambda j, be, nb: (j, 0)),
                pl.BlockSpec((1, D_MODEL, 2 * EXPERT_FF), lambda j, be, nb: (be[j], 0, 0)),
                pl.BlockSpec((1, EXPERT_FF, D_MODEL), lambda j, be, nb: (be[j], 0, 0)),
            ],
            out_specs=pl.BlockSpec((bm, D_MODEL // 2), lambda j, be, nb: (j, 0)),
        ),
        out_shape=jax.ShapeDtypeStruct((p_len, D_MODEL // 2), U32),
        compiler_params=_cparams(("arbitrary",)),
        name="routed_experts",
    )(block_e, n_used, xs, wgu, wd)


SC_CORES = 2
SC_SUBCORES = 16
SC_WORKERS = SC_CORES * SC_SUBCORES
SC_CHUNK = 128


def _sc_mesh():
    return plsc.VectorSubcoreMesh(core_axis_name="c", subcore_axis_name="s",
                                  num_cores=SC_CORES, num_subcores=SC_SUBCORES)


def _dispatch_rows(xp, dest3, p_len):
    n, width = xp.shape
    n_chunks = dest3.shape[0]
    per_worker = n_chunks // SC_WORKERS

    @functools.partial(
        pl.kernel, mesh=_sc_mesh(),
        out_type=jax.ShapeDtypeStruct((p_len, width), xp.dtype),
        scratch_types=[pltpu.VMEM((TOP_K, SC_CHUNK), I32), pltpu.VMEM((SC_CHUNK, width), xp.dtype),
                       pltpu.SemaphoreType.DMA],
        name="sc_dispatch")
    def body(xp_hbm, dest_hbm, xs_hbm, idx_v, rows_v, sem):
        wid = lax.axis_index("s") * SC_CORES + lax.axis_index("c")

        @pl.loop(0, per_worker)
        def _(j):
            chunk = wid * per_worker + j
            pltpu.sync_copy(dest_hbm.at[chunk], idx_v)
            pltpu.sync_copy(xp_hbm.at[pl.ds(chunk * SC_CHUNK, SC_CHUNK)], rows_v)
            copies = [pltpu.async_copy(rows_v, xs_hbm.at[idx_v.at[k]], sem) for k in range(TOP_K)]
            for cp in copies:
                cp.wait()

    return body(xp, dest3)


def _gather_rows(ys, dest3):
    width = ys.shape[1]
    n_chunks = dest3.shape[0]
    n = n_chunks * SC_CHUNK
    per_worker = n_chunks // SC_WORKERS

    @functools.partial(
        pl.kernel, mesh=_sc_mesh(),
        out_type=jax.ShapeDtypeStruct((TOP_K, n, width), ys.dtype),
        scratch_types=[pltpu.VMEM((TOP_K, SC_CHUNK), I32), pltpu.VMEM((SC_CHUNK, width), ys.dtype),
                       pltpu.SemaphoreType.DMA],
        name="sc_gather")
    def body(ys_hbm, dest_hbm, out_hbm, idx_v, rows_v, sem):
        wid = lax.axis_index("s") * SC_CORES + lax.axis_index("c")

        @pl.loop(0, per_worker)
        def _(j):
            chunk = wid * per_worker + j
            pltpu.sync_copy(dest_hbm.at[chunk], idx_v)
            for k in range(TOP_K):
                pltpu.async_copy(ys_hbm.at[idx_v.at[k]], rows_v, sem).wait()
                pltpu.sync_copy(rows_v, out_hbm.at[k, pl.ds(chunk * SC_CHUNK, SC_CHUNK)])

    return body(ys, dest3)


def _combine_kernel(base_ref, g_ref, w_ref, o_ref):
    half = D_MODEL // 2
    lo_acc = base_ref[:, :half]
    hi_acc = base_ref[:, half:]
    for k in range(TOP_K):
        lo, hi = _unpack_bf16_pairs(g_ref[k])
        wk = w_ref[:, k:k + 1]
        lo_acc = lo_acc + wk * lo
        hi_acc = hi_acc + wk * hi
    o_ref[:, :half] = lo_acc
    o_ref[:, half:] = hi_acc


def _combine_call(base, gathered, w_t):
    n = base.shape[0]
    tm = TM_COMBINE
    return pl.pallas_call(
        _combine_kernel,
        grid=(n // tm,),
        in_specs=[
            pl.BlockSpec((tm, D_MODEL), lambda i: (i, 0)),
            pl.BlockSpec((TOP_K, tm, D_MODEL // 2), lambda i: (0, i, 0)),
            pl.BlockSpec((tm, TOP_K), lambda i: (i, 0)),
        ],
        out_specs=pl.BlockSpec((tm, D_MODEL), lambda i: (i, 0)),
        out_shape=jax.ShapeDtypeStruct((n, D_MODEL), F32),
        compiler_params=_cparams(("arbitrary",)),
        name="moe_combine",
    )(base, gathered, w_t)


def _prep_tables(seq):
    pos = jnp.arange(seq)
    cos1, sin1 = _rope_tables(pos, HEAD_DIM)
    cos_r, sin_r = _rope_tables(pos // GRID_W, HEAD_DIM // 2)
    cos_c, sin_c = _rope_tables(pos % GRID_W, HEAD_DIM // 2)
    taba = _lane_tables(cos1, sin1, HEAD_DIM // 2)
    tabb = _lane_tables(jnp.concatenate([cos_r, cos_c], -1), jnp.concatenate([sin_r, sin_c], -1),
                        HEAD_DIM // 4)
    return taba, tabb


def _layer(h, p, taba, tabb):
    b, seq, d = h.shape
    n = b * seq
    x2d = h.reshape(n, d)
    w_in = p["w_in"]
    o1 = N_GROUPS_A * A_GROUP_W
    o2 = o1 + B_Q_W
    o3 = o2 + B_KV_W
    pair_heads = jnp.array([0, 4, 1, 5, 2, 6, 3, 7])
    pair_cols = (pair_heads[:, None] * HEAD_DIM + jnp.arange(HEAD_DIM)[None, :]).reshape(-1)
    wqkv = jnp.concatenate([w_in[:, :o1], w_in[:, o1:o2][:, pair_cols], w_in[:, o2:o3]], axis=1).astype(BF16)
    scale = HEAD_DIM ** -0.5
    qa = jnp.tile(p["q_norm_a"] * scale, 4)
    ka = jnp.tile(p["k_norm_a"], 4)
    ones = jnp.ones((A_OUT_W,), F32)
    gain = jnp.concatenate([jnp.concatenate([qa, ka, ones])] * N_GROUPS_A
                           + [jnp.tile(p["q_norm_b"] * scale, 8), jnp.tile(p["k_norm_b"], 2),
                              jnp.ones((LANES,), F32)]).reshape(1, QKV_W).astype(F32)
    seg = jnp.arange(LANES) // HEAD_DIM
    bd = (seg[:, None] == seg[None, :]).astype(BF16)
    g1 = p["norm1_g"].reshape(1, d).astype(F32)

    a0, a1, a2, qb, kvb = _proj_call(x2d, g1, wqkv, gain, taba, tabb, bd, seq)
    ya = _attn_a_call(a0.reshape(b, seq, A_GROUP_W), a1.reshape(b, seq // 4, 4 * A_GROUP_W),
                      a2.reshape(b, seq // 16, 16 * A_GROUP_W))
    yb = _attn_b_call(qb.reshape(b, seq, B_Q_W), kvb.reshape(b, seq, B_KV_W))

    wg = w_in[:, o3:].astype(BF16)
    wbb = p["w_branch_b"][pair_cols, :].astype(BF16)
    wr = jnp.pad(p["w_router"], ((0, 0), (0, LANES - N_EXPERTS)))
    wrh = wr.astype(BF16)
    wrl = (wr - wrh.astype(F32)).astype(BF16)
    wsgu = jnp.concatenate([p["ws_gate"], p["ws_up"]], axis=1).astype(BF16)
    tm = TM_MERGE
    tri = (jnp.arange(tm)[:, None] < jnp.arange(tm)[None, :]).astype(BF16)
    base, xp, eidx, rank, wgt, cnt = _merge_call(
        x2d, ya.reshape(n, A_OUT_W), yb.reshape(n, B_Q_W), g1, wg,
        p["w_branch_a"].astype(BF16), wbb, p["w_out"].astype(BF16),
        p["norm2_g"].reshape(1, d).astype(F32), wrh, wrl,
        p["router_bias"].reshape(N_EXPERTS, 1).astype(F32), wsgu, p["ws_down"].astype(BF16), tri)

    bm = BM_EXPERT
    counts = cnt[:, 0].astype(I32)
    padded = (counts + bm - 1) // bm * bm
    pend = jnp.cumsum(padded)
    pstart = pend - padded
    n_blocks = (n * TOP_K + N_EXPERTS * (bm - 1)) // bm + 1
    p_len = n_blocks * bm
    onehot = eidx[:, :, None] == jnp.arange(N_EXPERTS)[None, None, :]
    dest = jnp.sum(jnp.where(onehot, pstart[None, None, :], 0), axis=-1) + rank
    block_start = jnp.arange(n_blocks, dtype=I32) * bm
    block_e = jnp.minimum(jnp.sum(pend[None, :] <= block_start[:, None], axis=1), N_EXPERTS - 1).astype(I32)
    n_used = (pend[-1] // bm).astype(I32).reshape(1)
    dest3 = dest.reshape(TOP_K, n // SC_CHUNK, SC_CHUNK).transpose(1, 0, 2)

    xs = _dispatch_rows(xp, dest3, p_len)
    wgu = jnp.concatenate([p["we_gate"], p["we_up"]], axis=-1).astype(BF16)
    ys = _expert_call(block_e, n_used, xs, wgu, p["we_down"].astype(BF16))
    gathered = _gather_rows(ys, dest3)
    out = _combine_call(base, gathered, wgt.T)
    return out.reshape(b, seq, d)


def kernel(x, norm1_g, w_in, q_norm_a, k_norm_a, q_norm_b, k_norm_b, w_branch_a, w_branch_b, w_out,
           norm2_g, w_router, router_bias, we_gate, we_up, we_down, ws_gate, ws_up, ws_down):
    params = dict(norm1_g=norm1_g, w_in=w_in, q_norm_a=q_norm_a, k_norm_a=k_norm_a, q_norm_b=q_norm_b,
                  k_norm_b=k_norm_b, w_branch_a=w_branch_a, w_branch_b=w_branch_b, w_out=w_out,
                  norm2_g=norm2_g, w_router=w_router, router_bias=router_bias, we_gate=we_gate,
                  we_up=we_up, we_down=we_down, ws_gate=ws_gate, ws_up=ws_up, ws_down=ws_down)
    taba, tabb = _prep_tables(x.shape[1])
    h = x
    for l in range(norm1_g.shape[0]):
        h = _layer(h, {k: v[l] for k, v in params.items()}, taba, tabb)
    return h
```

```python
import functools

import jax
import jax.numpy as jnp
from jax import lax
from jax.experimental import pallas as pl
from jax.experimental.pallas import tpu as pltpu
from jax.experimental.pallas import tpu_sc as plsc

F32 = jnp.float32
BF16 = jnp.bfloat16
I32 = jnp.int32
U32 = jnp.uint32

D_MODEL = 1024
HEAD_DIM = 64
ROPE_THETA = 10000.0
EPS = 1e-6
GRID_W = 64
DILATIONS = (1, 4, 16)
RADIUS = 64
N_GROUPS_A = 3
A_GROUP_W = 768
A_OUT_W = 256
B_Q_W = 512
B_KV_W = 256
QKV_W = N_GROUPS_A * A_GROUP_W + B_Q_W + B_KV_W
N_EXPERTS = 64
N_EXPERT_GROUPS = 8
TOPK_GROUPS = 4
TOP_K = 8
EXPERT_FF = 256
ROUTED_SCALE = 2.5

LANES = 128
NEG_BIG = -1e30
Q_SCALE = HEAD_DIM ** -0.5 * 1.4426950408889634

TM_PROJ = 512
TQ_B = 256
TK_B = 512
TM_MERGE = 256
BM_EXPERT = 512
TM_COMBINE = 256
VMEM_LIMIT = 56 * 1024 * 1024


def _cparams(sem):
    return pltpu.CompilerParams(dimension_semantics=sem, vmem_limit_bytes=VMEM_LIMIT)


def _rope_tables(pos, dim):
    inv = ROPE_THETA ** (-jnp.arange(0, dim, 2, dtype=F32) / dim)
    ang = pos.astype(F32)[:, None] * inv[None, :]
    ang = jnp.concatenate([ang, ang], axis=-1)
    return jnp.cos(ang), jnp.sin(ang)


def _lane_tables(cos_h, sin_h, half):
    cos2 = jnp.concatenate([cos_h, cos_h], axis=-1)
    sin2 = jnp.concatenate([sin_h, sin_h], axis=-1)
    first = (jnp.arange(LANES) % (2 * half)) < half
    s_left = jnp.where(first[None, :], -sin2, 0.0)
    s_right = jnp.where(first[None, :], 0.0, sin2)
    return jnp.stack([cos2, s_left, s_right]).astype(F32)


def _proj_kernel(x_ref, g1_ref, w_ref, gain_ref, taba_ref, tabb_ref, bd_ref,
                 a0_ref, a1_ref, a2_ref, qb_ref, kvb_ref, pbuf_ref):
    x = x_ref[...]
    ms = jnp.mean(x * x, axis=-1, keepdims=True)
    xn = (x * lax.rsqrt(ms + EPS) * g1_ref[...]).astype(BF16)
    bd = bd_ref[...]
    tm = x.shape[0]

    def norm_rope(y, col, tab_ref, shift):
        sq = y * y
        hi = sq.astype(BF16)
        lo = (sq - hi.astype(F32)).astype(BF16)
        ss = jnp.dot(hi, bd, preferred_element_type=F32) + jnp.dot(lo, bd, preferred_element_type=F32)
        yn = y * lax.rsqrt(ss * (1.0 / HEAD_DIM) + EPS) * gain_ref[:, col:col + LANES]
        return (yn * tab_ref[0] + pltpu.roll(yn, LANES - shift, 1) * tab_ref[1]
                + pltpu.roll(yn, shift, 1) * tab_ref[2])

    for g, d in enumerate(DILATIONS):
        base = g * A_GROUP_W
        r = jnp.dot(xn, w_ref[:, base:base + A_GROUP_W], preferred_element_type=F32)
        for c in range(A_GROUP_W // LANES):
            y = r[:, c * LANES:(c + 1) * LANES]
            if c < 4:
                y = norm_rope(y, base + c * LANES, taba_ref, HEAD_DIM // 2)
            if d == 1:
                a0_ref[:, c * LANES:(c + 1) * LANES] = y.astype(BF16)
            else:
                pbuf_ref[c] = y
        if d > 1:
            out_ref = a1_ref if g == 1 else a2_ref
            rows = tm // d
            for res in range(d):
                for c in range(A_GROUP_W // LANES):
                    col = res * A_GROUP_W + c * LANES
                    out_ref[:, col:col + LANES] = pbuf_ref[c, pl.ds(res, rows, stride=d), :].astype(BF16)

    base = N_GROUPS_A * A_GROUP_W
    r = jnp.dot(xn, w_ref[:, base:base + B_Q_W + B_KV_W], preferred_element_type=F32)
    for c in range((B_Q_W + B_KV_W) // LANES):
        y = r[:, c * LANES:(c + 1) * LANES]
        if c < 5:
            y = norm_rope(y, base + c * LANES, tabb_ref, HEAD_DIM // 4)
        if c < 4:
            qb_ref[:, c * LANES:(c + 1) * LANES] = y.astype(BF16)
        else:
            kvb_ref[:, (c - 4) * LANES:(c - 3) * LANES] = y.astype(BF16)


def _proj_call(x2d, g1, wqkv, gain, taba, tabb, bd, seq):
    n = x2d.shape[0]
    tm = TM_PROJ
    tiles_per_seq = seq // tm
    full = lambda shape: pl.BlockSpec(shape, lambda i: (0,) * len(shape))
    tab_spec = pl.BlockSpec((3, tm, LANES), lambda i: (0, i % tiles_per_seq, 0))
    return pl.pallas_call(
        _proj_kernel,
        grid=(n // tm,),
        in_specs=[
            pl.BlockSpec((tm, D_MODEL), lambda i: (i, 0)),
            full((1, D_MODEL)),
            full((D_MODEL, QKV_W)),
            full((1, QKV_W)),
            tab_spec, tab_spec,
            full((LANES, LANES)),
        ],
        out_specs=[
            pl.BlockSpec((tm, A_GROUP_W), lambda i: (i, 0)),
            pl.BlockSpec((tm // 4, 4 * A_GROUP_W), lambda i: (i, 0)),
            pl.BlockSpec((tm // 16, 16 * A_GROUP_W), lambda i: (i, 0)),
            pl.BlockSpec((tm, B_Q_W), lambda i: (i, 0)),
            pl.BlockSpec((tm, B_KV_W), lambda i: (i, 0)),
        ],
        out_shape=[
            jax.ShapeDtypeStruct((n, A_GROUP_W), BF16),
            jax.ShapeDtypeStruct((n // 4, 4 * A_GROUP_W), BF16),
            jax.ShapeDtypeStruct((n // 16, 16 * A_GROUP_W), BF16),
            jax.ShapeDtypeStruct((n, B_Q_W), BF16),
            jax.ShapeDtypeStruct((n, B_KV_W), BF16),
        ],
        scratch_shapes=[pltpu.VMEM((A_GROUP_W // LANES, tm, LANES), F32)],
        compiler_params=_cparams(("arbitrary",)),
        name="proj_qkv",
    )(x2d, g1, wqkv, gain, taba, tabb, bd)


def _attn_a_kernel(a0_ref, a1_ref, a2_ref, ya_ref, acc_ref, m_ref, l_ref):
    tq = 128
    lane = lax.broadcasted_iota(I32, (tq, LANES), 1)
    low = lane < HEAD_DIM

    def tile(g, ref, col0, length, i, rows):
        win = min(2 * tq, length)
        if isinstance(i, int):
            q0 = i * tq
            ws = max(0, min(q0 - RADIUS, length - win))
        else:
            q0 = pl.multiple_of(i * tq, tq)
            ws = pl.multiple_of(jnp.clip(q0 - RADIUS, 0, length - win), RADIUS)
        qrow = lax.broadcasted_iota(I32, (2 * tq, win), 0)
        qpos = q0 + jnp.where(qrow >= tq, qrow - tq, qrow)
        kpos = ws + lax.broadcasted_iota(I32, (2 * tq, win), 1)
        valid = jnp.abs(qpos - kpos) <= RADIUS
        for hp in range(2):
            q2 = ref[0, pl.ds(q0, tq), col0 + hp * LANES:col0 + (hp + 1) * LANES]
            k2 = ref[0, pl.ds(ws, win), col0 + 256 + hp * LANES:col0 + 256 + (hp + 1) * LANES]
            v2 = ref[0, pl.ds(ws, win), col0 + 512 + hp * LANES:col0 + 512 + (hp + 1) * LANES]
            zero = jnp.zeros_like(q2)
            qs = jnp.concatenate([jnp.where(low, q2, zero), jnp.where(low, zero, q2)], axis=0)
            s = lax.dot_general(qs, k2, (((1,), (1,)), ((), ())), preferred_element_type=F32)
            s = jnp.where(valid, s, NEG_BIG)
            m = jnp.max(s, axis=-1, keepdims=True)
            p = jnp.exp2(s - m)
            l = jnp.sum(p, axis=-1, keepdims=True)
            pv = jnp.dot(p.astype(BF16), v2, preferred_element_type=F32)
            slot = 2 * g + hp
            acc_ref[slot, rows, :] = jnp.where(low, pv[:tq], pv[tq:])
            m_ref[slot, rows, :] = jnp.where(low, m[:tq], m[tq:])
            l_ref[slot, rows, :] = jnp.where(low, l[:tq], l[tq:])

    seq = a0_ref.shape[1]

    def g0_body(i, carry):
        tile(0, a0_ref, 0, seq, i, pl.ds(pl.multiple_of(i * tq, tq), tq))
        return carry
    lax.fori_loop(0, seq // tq, g0_body, 0, unroll=2)

    len1 = seq // 4
    for res in range(4):
        def g1_body(i, carry, res=res):
            tile(1, a1_ref, res * A_GROUP_W, len1, i, pl.ds(res + 4 * i * tq, tq, stride=4))
            return carry
        lax.fori_loop(0, len1 // tq, g1_body, 0, unroll=2)

    len2 = seq // 16
    for res in range(16):
        tile(2, a2_ref, res * A_GROUP_W, len2, 0, pl.ds(res, tq, stride=16))

    rc = 256

    def merge_body(j, carry):
        rows = pl.ds(pl.multiple_of(j * rc, rc), rc)
        for hp in range(2):
            ms = [m_ref[2 * g + hp, rows, :] for g in range(N_GROUPS_A)]
            mx = jnp.maximum(jnp.maximum(ms[0], ms[1]), ms[2])
            num = jnp.zeros((rc, LANES), F32)
            den = jnp.zeros((rc, LANES), F32)
            for g in range(N_GROUPS_A):
                e = jnp.exp2(ms[g] - mx)
                num = num + e * acc_ref[2 * g + hp, rows, :]
                den = den + e * l_ref[2 * g + hp, rows, :]
            ya_ref[0, rows, hp * LANES:(hp + 1) * LANES] = (num / den).astype(BF16)
        return carry
    lax.fori_loop(0, seq // rc, merge_body, 0)


def _attn_a_call(a0, a1, a2):
    b, seq, _ = a0.shape
    return pl.pallas_call(
        _attn_a_kernel,
        grid=(b,),
        in_specs=[
            pl.BlockSpec((1, seq, A_GROUP_W), lambda i: (i, 0, 0)),
            pl.BlockSpec((1, seq // 4, 4 * A_GROUP_W), lambda i: (i, 0, 0)),
            pl.BlockSpec((1, seq // 16, 16 * A_GROUP_W), lambda i: (i, 0, 0)),
        ],
        out_specs=pl.BlockSpec((1, seq, A_OUT_W), lambda i: (i, 0, 0)),
        out_shape=jax.ShapeDtypeStruct((b, seq, A_OUT_W), BF16),
        scratch_shapes=[pltpu.VMEM((N_GROUPS_A * A_OUT_W // LANES, seq, LANES), F32)] * 3,
        compiler_params=_cparams(("arbitrary",)),
        name="attn_a",
    )(a0, a1, a2)


def _attn_b_kernel(q_ref, kv_ref, o_ref):
    tq = q_ref.shape[1]
    seq = kv_ref.shape[1]
    low = lax.broadcasted_iota(I32, (tq, LANES), 1) < HEAD_DIM
    low_k = lax.broadcasted_iota(I32, (TK_B, LANES), 1) < HEAD_DIM
    for pr in range(B_Q_W // LANES):
        q2 = q_ref[0, :, pr * LANES:(pr + 1) * LANES]
        zero = jnp.zeros_like(q2)
        qs = jnp.concatenate([jnp.where(low, q2, zero), jnp.where(low, zero, q2)], axis=0)
        m = jnp.full((2 * tq, 1), NEG_BIG, F32)
        acc_a = jnp.zeros((tq, LANES), F32)
        acc_b = jnp.zeros((tq, LANES), F32)
        for c in range(seq // TK_B):
            k2 = kv_ref[0, c * TK_B:(c + 1) * TK_B, 0:LANES]
            v2 = kv_ref[0, c * TK_B:(c + 1) * TK_B, LANES:2 * LANES]
            one = jnp.ones_like(v2)
            va = jnp.where(low_k, v2, one)
            vb = jnp.where(low_k, one, v2)
            s = lax.dot_general(qs, k2, (((1,), (1,)), ((), ())), preferred_element_type=F32)
            m_n = jnp.maximum(m, jnp.max(s, axis=-1, keepdims=True))
            p = jnp.exp2(s - m_n).astype(BF16)
            a = jnp.exp2(m - m_n)
            acc_a = a[:tq] * acc_a + jnp.dot(p[:tq], va, preferred_element_type=F32)
            acc_b = a[tq:] * acc_b + jnp.dot(p[tq:], vb, preferred_element_type=F32)
            m = m_n
        oa = acc_a / pltpu.roll(acc_a, HEAD_DIM, 1)
        ob = acc_b / pltpu.roll(acc_b, HEAD_DIM, 1)
        o_ref[0, :, pr * LANES:(pr + 1) * LANES] = jnp.where(low, oa, ob).astype(BF16)


def _attn_b_call(qb, kvb):
    b, seq, _ = qb.shape
    return pl.pallas_call(
        _attn_b_kernel,
        grid=(b, seq // TQ_B),
        in_specs=[
            pl.BlockSpec((1, TQ_B, B_Q_W), lambda i, j: (i, j, 0)),
            pl.BlockSpec((1, seq, B_KV_W), lambda i, j: (i, 0, 0)),
        ],
        out_specs=pl.BlockSpec((1, TQ_B, B_Q_W), lambda i, j: (i, j, 0)),
        out_shape=jax.ShapeDtypeStruct((b, seq, B_Q_W), BF16),
        compiler_params=_cparams(("arbitrary",) * 2),
        name="attn_b",
    )(qb, kvb)


def _pack_bf16_pairs(v):
    half = v.shape[1] // 2
    lo = lax.bitcast_convert_type(v[:, :half].astype(BF16).astype(F32), U32)
    hi = lax.bitcast_convert_type(v[:, half:].astype(BF16).astype(F32), U32)
    return (lo >> 16) | (hi & jnp.uint32(0xFFFF0000))


def _unpack_bf16_pairs(p):
    lo = lax.bitcast_convert_type(p << 16, F32)
    hi = lax.bitcast_convert_type(p & jnp.uint32(0xFFFF0000), F32)
    return lo, hi


def _merge_kernel(x_ref, ya_ref, yb_ref, g1_ref, wg_ref, wba_ref, wbb_ref, wo_ref, g2_ref,
                  wrh_ref, wrl_ref, bias_ref, wsgu_ref, wsd_ref, tri_ref,
                  base_ref, xp_ref, eidx_ref, rank_ref, wgt_ref, cnt_ref, carry_ref):
    step = pl.program_id(0)

    @pl.when(step == 0)
    def _():
        carry_ref[...] = jnp.zeros_like(carry_ref)

    x = x_ref[...]
    tm = x.shape[0]
    ms = jnp.mean(x * x, axis=-1, keepdims=True)
    xn = (x * lax.rsqrt(ms + EPS) * g1_ref[...]).astype(BF16)
    gates = jax.nn.sigmoid(jnp.dot(xn, wg_ref[...], preferred_element_type=F32))
    pa = jnp.dot(ya_ref[...], wba_ref[...], preferred_element_type=F32)
    pb = jnp.dot(yb_ref[...], wbb_ref[...], preferred_element_type=F32)
    merged = gates[:, :D_MODEL] * pa + gates[:, D_MODEL:] * pb
    h = x + jnp.dot(merged.astype(BF16), wo_ref[...], preferred_element_type=F32)

    ms2 = jnp.mean(h * h, axis=-1, keepdims=True)
    xn2 = h * lax.rsqrt(ms2 + EPS) * g2_ref[...]
    xn2b = xn2.astype(BF16)
    xp_ref[...] = _pack_bf16_pairs(xn2)

    gu = jnp.dot(xn2b, wsgu_ref[...], preferred_element_type=F32)
    hs = (jax.nn.silu(gu[:, :EXPERT_FF]) * gu[:, EXPERT_FF:]).astype(BF16)
    base_ref[...] = h + jnp.dot(hs, wsd_ref[...], preferred_element_type=F32)

    xlo = (xn2 - xn2b.astype(F32)).astype(BF16)
    logits = (jnp.dot(xn2b, wrh_ref[...], preferred_element_type=F32)
              + jnp.dot(xlo, wrh_ref[...], preferred_element_type=F32)
              + jnp.dot(xn2b, wrl_ref[...], preferred_element_type=F32))
    scores = jax.nn.sigmoid(logits.T[:N_EXPERTS, :])
    sel = scores + bias_ref[...]

    gsz = N_EXPERTS // N_EXPERT_GROUPS
    iota8 = lax.broadcasted_iota(I32, (gsz, tm), 0).astype(F32)
    ninf = jnp.float32(-jnp.inf)
    blocks, gs = [], []
    for g in range(N_EXPERT_GROUPS):
        blk = sel[g * gsz:(g + 1) * gsz, :]
        m1 = jnp.max(blk, axis=0, keepdims=True)
        first = jnp.min(jnp.where(blk == m1, iota8, float(gsz)), axis=0, keepdims=True)
        m2 = jnp.max(jnp.where(iota8 == first, ninf, blk), axis=0, keepdims=True)
        blocks.append(blk)
        gs.append(m1 + m2)
    cur = jnp.concatenate(gs, axis=0)
    gmask = jnp.zeros((N_EXPERT_GROUPS, tm), F32)
    for _ in range(TOPK_GROUPS):
        mx = jnp.max(cur, axis=0, keepdims=True)
        fi = jnp.min(jnp.where(cur == mx, iota8, float(N_EXPERT_GROUPS)), axis=0, keepdims=True)
        pick = iota8 == fi
        gmask = jnp.where(pick, 1.0, gmask)
        cur = jnp.where(pick, ninf, cur)
    cur = jnp.concatenate(
        [jnp.where(gmask[g:g + 1, :] > 0.5, blocks[g], ninf) for g in range(N_EXPERT_GROUPS)], axis=0)

    iota64 = lax.broadcasted_iota(I32, (N_EXPERTS, tm), 0).astype(F32)
    idxs, wks = [], []
    onehot = jnp.zeros((N_EXPERTS, tm), F32)
    for _ in range(TOP_K):
        mx = jnp.max(cur, axis=0, keepdims=True)
        fi = jnp.min(jnp.where(cur == mx, iota64, float(N_EXPERTS)), axis=0, keepdims=True)
        pick = iota64 == fi
        wks.append(jnp.sum(jnp.where(pick, scores, 0.0), axis=0, keepdims=True))
        idxs.append(fi)
        onehot = jnp.where(pick, 1.0, onehot)
        cur = jnp.where(pick, ninf, cur)
    wk = jnp.concatenate(wks, axis=0)
    wgt_ref[...] = wk / jnp.sum(wk, axis=0, keepdims=True) * ROUTED_SCALE
    eidx_ref[...] = jnp.concatenate(idxs, axis=0).astype(I32)

    before = jnp.dot(onehot.astype(BF16), tri_ref[...], preferred_element_type=F32) + carry_ref[:, 0:1]
    rank_ref[...] = jnp.concatenate(
        [jnp.sum(jnp.where(iota64 == fi, before, 0.0), axis=0, keepdims=True) for fi in idxs],
        axis=0).astype(I32)
    carry_ref[...] = carry_ref[...] + jnp.sum(onehot, axis=1, keepdims=True)
    cnt_ref[...] = carry_ref[...]


def _merge_call(x2d, ya, yb, g1, wg, wba, wbb, wo, g2, wrh, wrl, bias, wsgu, wsd, tri):
    n = x2d.shape[0]
    tm = TM_MERGE
    full = lambda shape: pl.BlockSpec(shape, lambda i: (0,) * len(shape))
    row = lambda w: pl.BlockSpec((tm, w), lambda i: (i, 0))
    col = lambda: pl.BlockSpec((TOP_K, tm), lambda i: (0, i))
    return pl.pallas_call(
        _merge_kernel,
        grid=(n // tm,),
        in_specs=[
            row(D_MODEL), row(A_OUT_W), row(B_Q_W),
            full((1, D_MODEL)), full((D_MODEL, 2 * D_MODEL)),
            full((A_OUT_W, D_MODEL)), full((B_Q_W, D_MODEL)), full((D_MODEL, D_MODEL)),
            full((1, D_MODEL)),
            full((D_MODEL, LANES)), full((D_MODEL, LANES)), full((N_EXPERTS, 1)),
            full((D_MODEL, 2 * EXPERT_FF)), full((EXPERT_FF, D_MODEL)),
            full((tm, tm)),
        ],
        out_specs=[row(D_MODEL), row(D_MODEL // 2), col(), col(), col(), full((N_EXPERTS, LANES))],
        out_shape=[
            jax.ShapeDtypeStruct((n, D_MODEL), F32),
            jax.ShapeDtypeStruct((n, D_MODEL // 2), U32),
            jax.ShapeDtypeStruct((TOP_K, n), I32),
            jax.ShapeDtypeStruct((TOP_K, n), I32),
            jax.ShapeDtypeStruct((TOP_K, n), F32),
            jax.ShapeDtypeStruct((N_EXPERTS, LANES), F32),
        ],
        scratch_shapes=[pltpu.VMEM((N_EXPERTS, LANES), F32)],
        compiler_params=_cparams(("arbitrary",)),
        name="merge_router",
    )(x2d, ya, yb, g1, wg, wba, wbb, wo, g2, wrh, wrl, bias, wsgu, wsd, tri)


def _expert_kernel(be_ref, nb_ref, xs_ref, wgu_ref, wd_ref, ys_ref):
    @pl.when(pl.program_id(0) >= nb_ref[0])
    def _():
        ys_ref[...] = jnp.zeros_like(ys_ref)

    @pl.when(pl.program_id(0) < nb_ref[0])
    def _():
        lo, hi = _unpack_bf16_pairs(xs_ref[...])
        half = D_MODEL // 2
        gu = (jnp.dot(lo.astype(BF16), wgu_ref[0, :half, :], preferred_element_type=F32)
              + jnp.dot(hi.astype(BF16), wgu_ref[0, half:, :], preferred_element_type=F32))
        hb = (jax.nn.silu(gu[:, :EXPERT_FF]) * gu[:, EXPERT_FF:]).astype(BF16)
        ys_ref[...] = _pack_bf16_pairs(jnp.dot(hb, wd_ref[0], preferred_element_type=F32))


def _expert_call(block_e, n_used, xs, wgu, wd):
    p_len = xs.shape[0]
    bm = BM_EXPERT
    return pl.pallas_call(
        _expert_kernel,
        grid_spec=pltpu.PrefetchScalarGridSpec(
            num_scalar_prefetch=2,
            grid=(p_len // bm,),
            in_specs=[
                pl.BlockSpec((bm, D_MODEL // 2), lambda j, be, nb: (j, 0)),
                pl.BlockSpec((1, D_MODEL, 2 * EXPERT_FF), lambda j, be, nb: (be[j], 0, 0)),
                pl.BlockSpec((1, EXPERT_FF, D_MODEL), lambda j, be, nb: (be[j], 0, 0)),
            ],
            out_specs=pl.BlockSpec((bm, D_MODEL // 2), lambda j, be, nb: (j, 0)),
        ),
        out_shape=jax.ShapeDtypeStruct((p_len, D_MODEL // 2), U32),
        compiler_params=_cparams(("arbitrary",)),
        name="routed_experts",
    )(block_e, n_used, xs, wgu, wd)


SC_CORES = 2
SC_SUBCORES = 16
SC_WORKERS = SC_CORES * SC_SUBCORES
SC_CHUNK = 128


def _sc_mesh():
    return plsc.VectorSubcoreMesh(core_axis_name="c", subcore_axis_name="s",
                                  num_cores=SC_CORES, num_subcores=SC_SUBCORES)


def _dispatch_rows(xp, dest3, p_len):
    n, width = xp.shape
    n_chunks = dest3.shape[0]
    per_worker = n_chunks // SC_WORKERS

    @functools.partial(
        pl.kernel, mesh=_sc_mesh(),
        out_type=jax.ShapeDtypeStruct((p_len, width), xp.dtype),
        scratch_types=[pltpu.VMEM((TOP_K, SC_CHUNK), I32), pltpu.VMEM((SC_CHUNK, width), xp.dtype),
                       pltpu.SemaphoreType.DMA],
        name="sc_dispatch")
    def body(xp_hbm, dest_hbm, xs_hbm, idx_v, rows_v, sem):
        wid = lax.axis_index("s") * SC_CORES + lax.axis_index("c")

        @pl.loop(0, per_worker)
        def _(j):
            chunk = wid * per_worker + j
            pltpu.sync_copy(dest_hbm.at[chunk], idx_v)
            pltpu.sync_copy(xp_hbm.at[pl.ds(chunk * SC_CHUNK, SC_CHUNK)], rows_v)
            copies = [pltpu.async_copy(rows_v, xs_hbm.at[idx_v.at[k]], sem) for k in range(TOP_K)]
            for cp in copies:
                cp.wait()

    return body(xp, dest3)


def _gather_rows(ys, dest3):
    width = ys.shape[1]
    n_chunks = dest3.shape[0]
    n = n_chunks * SC_CHUNK
    per_worker = n_chunks // SC_WORKERS

    @functools.partial(
        pl.kernel, mesh=_sc_mesh(),
        out_type=jax.ShapeDtypeStruct((TOP_K, n, width), ys.dtype),
        scratch_types=[pltpu.VMEM((TOP_K, SC_CHUNK), I32), pltpu.VMEM((SC_CHUNK, width), ys.dtype),
                       pltpu.SemaphoreType.DMA],
        name="sc_gather")
    def body(ys_hbm, dest_hbm, out_hbm, idx_v, rows_v, sem):
        wid = lax.axis_index("s") * SC_CORES + lax.axis_index("c")

        @pl.loop(0, per_worker)
        def _(j):
            chunk = wid * per_worker + j
            pltpu.sync_copy(dest_hbm.at[chunk], idx_v)
            for k in range(TOP_K):
                pltpu.async_copy(ys_hbm.at[idx_v.at[k]], rows_v, sem).wait()
                pltpu.sync_copy(rows_v, out_hbm.at[k, pl.ds(chunk * SC_CHUNK, SC_CHUNK)])

    return body(ys, dest3)


def _combine_kernel(base_ref, g_ref, w_ref, o_ref):
    half = D_MODEL // 2
    lo_acc = base_ref[:, :half]
    hi_acc = base_ref[:, half:]
    for k in range(TOP_K):
        lo, hi = _unpack_bf16_pairs(g_ref[k])
        wk = w_ref[:, k:k + 1]
        lo_acc = lo_acc + wk * lo
        hi_acc = hi_acc + wk * hi
    o_ref[:, :half] = lo_acc
    o_ref[:, half:] = hi_acc


def _combine_call(base, gathered, w_t):
    n = base.shape[0]
    tm = TM_COMBINE
    return pl.pallas_call(
        _combine_kernel,
        grid=(n // tm,),
        in_specs=[
            pl.BlockSpec((tm, D_MODEL), lambda i: (i, 0)),
            pl.BlockSpec((TOP_K, tm, D_MODEL // 2), lambda i: (0, i, 0)),
            pl.BlockSpec((tm, TOP_K), lambda i: (i, 0)),
        ],
        out_specs=pl.BlockSpec((tm, D_MODEL), lambda i: (i, 0)),
        out_shape=jax.ShapeDtypeStruct((n, D_MODEL), F32),
        compiler_params=_cparams(("arbitrary",)),
        name="moe_combine",
    )(base, gathered, w_t)


def _prep_tables(seq):
    pos = jnp.arange(seq)
    cos1, sin1 = _rope_tables(pos, HEAD_DIM)
    cos_r, sin_r = _rope_tables(pos // GRID_W, HEAD_DIM // 2)
    cos_c, sin_c = _rope_tables(pos % GRID_W, HEAD_DIM // 2)
    taba = _lane_tables(cos1, sin1, HEAD_DIM // 2)
    tabb = _lane_tables(jnp.concatenate([cos_r, cos_c], -1), jnp.concatenate([sin_r, sin_c], -1),
                        HEAD_DIM // 4)
    return taba, tabb


def _layer(h, p, taba, tabb):
    b, seq, d = h.shape
    n = b * seq
    x2d = h.reshape(n, d)
    w_in = p["w_in"]
    o1 = N_GROUPS_A * A_GROUP_W
    o2 = o1 + B_Q_W
    o3 = o2 + B_KV_W
    pair_heads = jnp.array([0, 4, 1, 5, 2, 6, 3, 7])
    pair_cols = (pair_heads[:, None] * HEAD_DIM + jnp.arange(HEAD_DIM)[None, :]).reshape(-1)
    wqkv = jnp.concatenate([w_in[:, :o1], w_in[:, o1:o2][:, pair_cols], w_in[:, o2:o3]], axis=1).astype(BF16)
    scale = Q_SCALE
    qa = jnp.tile(p["q_norm_a"] * scale, 4)
    ka = jnp.tile(p["k_norm_a"], 4)
    ones = jnp.ones((A_OUT_W,), F32)
    gain = jnp.concatenate([jnp.concatenate([qa, ka, ones])] * N_GROUPS_A
                           + [jnp.tile(p["q_norm_b"] * scale, 8), jnp.tile(p["k_norm_b"], 2),
                              jnp.ones((LANES,), F32)]).reshape(1, QKV_W).astype(F32)
    seg = jnp.arange(LANES) // HEAD_DIM
    bd = (seg[:, None] == seg[None, :]).astype(BF16)
    g1 = p["norm1_g"].reshape(1, d).astype(F32)

    a0, a1, a2, qb, kvb = _proj_call(x2d, g1, wqkv, gain, taba, tabb, bd, seq)
    ya = _attn_a_call(a0.reshape(b, seq, A_GROUP_W), a1.reshape(b, seq // 4, 4 * A_GROUP_W),
                      a2.reshape(b, seq // 16, 16 * A_GROUP_W))
    yb = _attn_b_call(qb.reshape(b, seq, B_Q_W), kvb.reshape(b, seq, B_KV_W))

    wg = w_in[:, o3:].astype(BF16)
    wbb = p["w_branch_b"][pair_cols, :].astype(BF16)
    wr = jnp.pad(p["w_router"], ((0, 0), (0, LANES - N_EXPERTS)))
    wrh = wr.astype(BF16)
    wrl = (wr - wrh.astype(F32)).astype(BF16)
    wsgu = jnp.concatenate([p["ws_gate"], p["ws_up"]], axis=1).astype(BF16)
    tm = TM_MERGE
    tri = (jnp.arange(tm)[:, None] < jnp.arange(tm)[None, :]).astype(BF16)
    base, xp, eidx, rank, wgt, cnt = _merge_call(
        x2d, ya.reshape(n, A_OUT_W), yb.reshape(n, B_Q_W), g1, wg,
        p["w_branch_a"].astype(BF16), wbb, p["w_out"].astype(BF16),
        p["norm2_g"].reshape(1, d).astype(F32), wrh, wrl,
        p["router_bias"].reshape(N_EXPERTS, 1).astype(F32), wsgu, p["ws_down"].astype(BF16), tri)

    bm = BM_EXPERT
    counts = cnt[:, 0].astype(I32)
    padded = (counts + bm - 1) // bm * bm
    pend = jnp.cumsum(padded)
    pstart = pend - padded
    n_blocks = (n * TOP_K + N_EXPERTS * (bm - 1)) // bm + 1
    p_len = n_blocks * bm
    onehot = eidx[:, :, None] == jnp.arange(N_EXPERTS)[None, None, :]
    dest = jnp.sum(jnp.where(onehot, pstart[None, None, :], 0), axis=-1) + rank
    block_start = jnp.arange(n_blocks, dtype=I32) * bm
    block_e = jnp.minimum(jnp.sum(pend[None, :] <= block_start[:, None], axis=1), N_EXPERTS - 1).astype(I32)
    n_used = (pend[-1] // bm).astype(I32).reshape(1)
    dest3 = dest.reshape(TOP_K, n // SC_CHUNK, SC_CHUNK).transpose(1, 0, 2)

    xs = _dispatch_rows(xp, dest3, p_len)
    wgu = jnp.concatenate([p["we_gate"], p["we_up"]], axis=-1).astype(BF16)
    ys = _expert_call(block_e, n_used, xs, wgu, p["we_down"].astype(BF16))
    gathered = _gather_rows(ys, dest3)
    out = _combine_call(base, gathered, wgt.T)
    return out.reshape(b, seq, d)


def kernel(x, norm1_g, w_in, q_norm_a, k_norm_a, q_norm_b, k_norm_b, w_branch_a, w_branch_b, w_out,
           norm2_g, w_router, router_bias, we_gate, we_up, we_down, ws_gate, ws_up, ws_down):
    params = dict(norm1_g=norm1_g, w_in=w_in, q_norm_a=q_norm_a, k_norm_a=k_norm_a, q_norm_b=q_norm_b,
                  k_norm_b=k_norm_b, w_branch_a=w_branch_a, w_branch_b=w_branch_b, w_out=w_out,
                  norm2_g=norm2_g, w_router=w_router, router_bias=router_bias, we_gate=we_gate,
                  we_up=we_up, we_down=we_down, ws_gate=ws_gate, ws_up=ws_up, ws_down=ws_down)
    taba, tabb = _prep_tables(x.shape[1])
    h = x
    for l in range(norm1_g.shape[0]):
        h = _layer(h, {k: v[l] for k, v in params.items()}, taba, tabb)
    return h
```

```python
import functools

import jax
import jax.numpy as jnp
from jax import lax
from jax.experimental import pallas as pl
from jax.experimental.pallas import tpu as pltpu
from jax.experimental.pallas import tpu_sc as plsc

F32 = jnp.float32
BF16 = jnp.bfloat16
I32 = jnp.int32
U32 = jnp.uint32

D_MODEL = 1024
HEAD_DIM = 64
ROPE_THETA = 10000.0
EPS = 1e-6
GRID_W = 64
DILATIONS = (1, 4, 16)
RADIUS = 64
N_GROUPS_A = 3
A_GROUP_W = 768
A_OUT_W = 256
B_Q_W = 512
B_KV_W = 256
QKV_W = N_GROUPS_A * A_GROUP_W + B_Q_W + B_KV_W
N_EXPERTS = 64
N_EXPERT_GROUPS = 8
TOPK_GROUPS = 4
TOP_K = 8
EXPERT_FF = 256
ROUTED_SCALE = 2.5

LANES = 128
NEG_BIG = -1e30
Q_SCALE = HEAD_DIM ** -0.5 * 1.4426950408889634

TM_PROJ = 512
TQ_B = 256
TK_B = 512
TM_MERGE = 256
BM_EXPERT = 512
TM_COMBINE = 256
MOE_CHUNKS = 2
VMEM_LIMIT = 56 * 1024 * 1024


def _cparams(sem):
    return pltpu.CompilerParams(dimension_semantics=sem, vmem_limit_bytes=VMEM_LIMIT)


def _rope_tables(pos, dim):
    inv = ROPE_THETA ** (-jnp.arange(0, dim, 2, dtype=F32) / dim)
    ang = pos.astype(F32)[:, None] * inv[None, :]
    ang = jnp.concatenate([ang, ang], axis=-1)
    return jnp.cos(ang), jnp.sin(ang)


def _lane_tables(cos_h, sin_h, half):
    cos2 = jnp.concatenate([cos_h, cos_h], axis=-1)
    sin2 = jnp.concatenate([sin_h, sin_h], axis=-1)
    first = (jnp.arange(LANES) % (2 * half)) < half
    s_left = jnp.where(first[None, :], -sin2, 0.0)
    s_right = jnp.where(first[None, :], 0.0, sin2)
    return jnp.stack([cos2, s_left, s_right]).astype(F32)


def _proj_kernel(x_ref, g1_ref, w_ref, gain_ref, taba_ref, tabb_ref, bd_ref,
                 a0_ref, a1_ref, a2_ref, qb_ref, kvb_ref, pbuf_ref):
    x = x_ref[...]
    ms = jnp.mean(x * x, axis=-1, keepdims=True)
    xn = (x * lax.rsqrt(ms + EPS) * g1_ref[...]).astype(BF16)
    bd = bd_ref[...]
    tm = x.shape[0]

    def norm_rope(y, col, tab_ref, shift):
        sq = y * y
        hi = sq.astype(BF16)
        lo = (sq - hi.astype(F32)).astype(BF16)
        ss = jnp.dot(hi, bd, preferred_element_type=F32) + jnp.dot(lo, bd, preferred_element_type=F32)
        yn = y * lax.rsqrt(ss * (1.0 / HEAD_DIM) + EPS) * gain_ref[:, col:col + LANES]
        return (yn * tab_ref[0] + pltpu.roll(yn, LANES - shift, 1) * tab_ref[1]
                + pltpu.roll(yn, shift, 1) * tab_ref[2])

    for g, d in enumerate(DILATIONS):
        base = g * A_GROUP_W
        r = jnp.dot(xn, w_ref[:, base:base + A_GROUP_W], preferred_element_type=F32)
        for c in range(A_GROUP_W // LANES):
            y = r[:, c * LANES:(c + 1) * LANES]
            if c < 4:
                y = norm_rope(y, base + c * LANES, taba_ref, HEAD_DIM // 2)
            if d == 1:
                a0_ref[:, c * LANES:(c + 1) * LANES] = y.astype(BF16)
            else:
                pbuf_ref[c] = y
        if d > 1:
            out_ref = a1_ref if g == 1 else a2_ref
            rows = tm // d
            for res in range(d):
                for c in range(A_GROUP_W // LANES):
                    col = res * A_GROUP_W + c * LANES
                    out_ref[:, col:col + LANES] = pbuf_ref[c, pl.ds(res, rows, stride=d), :].astype(BF16)

    base = N_GROUPS_A * A_GROUP_W
    r = jnp.dot(xn, w_ref[:, base:base + B_Q_W + B_KV_W], preferred_element_type=F32)
    for c in range((B_Q_W + B_KV_W) // LANES):
        y = r[:, c * LANES:(c + 1) * LANES]
        if c < 5:
            y = norm_rope(y, base + c * LANES, tabb_ref, HEAD_DIM // 4)
        if c < 4:
            qb_ref[:, c * LANES:(c + 1) * LANES] = y.astype(BF16)
        else:
            kvb_ref[:, (c - 4) * LANES:(c - 3) * LANES] = y.astype(BF16)


def _proj_call(x2d, g1, wqkv, gain, taba, tabb, bd, seq):
    n = x2d.shape[0]
    tm = TM_PROJ
    tiles_per_seq = seq // tm
    full = lambda shape: pl.BlockSpec(shape, lambda i: (0,) * len(shape))
    tab_spec = pl.BlockSpec((3, tm, LANES), lambda i: (0, i % tiles_per_seq, 0))
    return pl.pallas_call(
        _proj_kernel,
        grid=(n // tm,),
        in_specs=[
            pl.BlockSpec((tm, D_MODEL), lambda i: (i, 0)),
            full((1, D_MODEL)),
            full((D_MODEL, QKV_W)),
            full((1, QKV_W)),
            tab_spec, tab_spec,
            full((LANES, LANES)),
        ],
        out_specs=[
            pl.BlockSpec((tm, A_GROUP_W), lambda i: (i, 0)),
            pl.BlockSpec((tm // 4, 4 * A_GROUP_W), lambda i: (i, 0)),
            pl.BlockSpec((tm // 16, 16 * A_GROUP_W), lambda i: (i, 0)),
            pl.BlockSpec((tm, B_Q_W), lambda i: (i, 0)),
            pl.BlockSpec((tm, B_KV_W), lambda i: (i, 0)),
        ],
        out_shape=[
            jax.ShapeDtypeStruct((n, A_GROUP_W), BF16),
            jax.ShapeDtypeStruct((n // 4, 4 * A_GROUP_W), BF16),
            jax.ShapeDtypeStruct((n // 16, 16 * A_GROUP_W), BF16),
            jax.ShapeDtypeStruct((n, B_Q_W), BF16),
            jax.ShapeDtypeStruct((n, B_KV_W), BF16),
        ],
        scratch_shapes=[pltpu.VMEM((A_GROUP_W // LANES, tm, LANES), F32)],
        compiler_params=_cparams(("arbitrary",)),
        name="proj_qkv",
    )(x2d, g1, wqkv, gain, taba, tabb, bd)


def _attn_a_kernel(a0_ref, a1_ref, a2_ref, ya_ref, acc_ref, m_ref, l_ref):
    tq = 128
    lane = lax.broadcasted_iota(I32, (tq, LANES), 1)
    low = lane < HEAD_DIM

    def tile(g, ref, col0, length, i, rows):
        win = min(2 * tq, length)
        if isinstance(i, int):
            q0 = i * tq
            ws = max(0, min(q0 - RADIUS, length - win))
        else:
            q0 = pl.multiple_of(i * tq, tq)
            ws = pl.multiple_of(jnp.clip(q0 - RADIUS, 0, length - win), RADIUS)
        qrow = lax.broadcasted_iota(I32, (2 * tq, win), 0)
        qpos = q0 + jnp.where(qrow >= tq, qrow - tq, qrow)
        kpos = ws + lax.broadcasted_iota(I32, (2 * tq, win), 1)
        valid = jnp.abs(qpos - kpos) <= RADIUS
        for hp in range(2):
            q2 = ref[0, pl.ds(q0, tq), col0 + hp * LANES:col0 + (hp + 1) * LANES]
            k2 = ref[0, pl.ds(ws, win), col0 + 256 + hp * LANES:col0 + 256 + (hp + 1) * LANES]
            v2 = ref[0, pl.ds(ws, win), col0 + 512 + hp * LANES:col0 + 512 + (hp + 1) * LANES]
            zero = jnp.zeros_like(q2)
            qs = jnp.concatenate([jnp.where(low, q2, zero), jnp.where(low, zero, q2)], axis=0)
            s = lax.dot_general(qs, k2, (((1,), (1,)), ((), ())), preferred_element_type=F32)
            s = jnp.where(valid, s, NEG_BIG)
            m = jnp.max(s, axis=-1, keepdims=True)
            p = jnp.exp2(s - m)
            l = jnp.sum(p, axis=-1, keepdims=True)
            pv = jnp.dot(p.astype(BF16), v2, preferred_element_type=F32)
            slot = 2 * g + hp
            acc_ref[slot, rows, :] = jnp.where(low, pv[:tq], pv[tq:])
            m_ref[slot, rows, :] = jnp.where(low, m[:tq], m[tq:])
            l_ref[slot, rows, :] = jnp.where(low, l[:tq], l[tq:])

    seq = a0_ref.shape[1]

    def g0_body(i, carry):
        tile(0, a0_ref, 0, seq, i, pl.ds(pl.multiple_of(i * tq, tq), tq))
        return carry
    lax.fori_loop(0, seq // tq, g0_body, 0, unroll=2)

    len1 = seq // 4
    for res in range(4):
        def g1_body(i, carry, res=res):
            tile(1, a1_ref, res * A_GROUP_W, len1, i, pl.ds(res + 4 * i * tq, tq, stride=4))
            return carry
        lax.fori_loop(0, len1 // tq, g1_body, 0, unroll=2)

    len2 = seq // 16
    for res in range(16):
        tile(2, a2_ref, res * A_GROUP_W, len2, 0, pl.ds(res, tq, stride=16))

    rc = 256

    def merge_body(j, carry):
        rows = pl.ds(pl.multiple_of(j * rc, rc), rc)
        for hp in range(2):
            ms = [m_ref[2 * g + hp, rows, :] for g in range(N_GROUPS_A)]
            mx = jnp.maximum(jnp.maximum(ms[0], ms[1]), ms[2])
            num = jnp.zeros((rc, LANES), F32)
            den = jnp.zeros((rc, LANES), F32)
            for g in range(N_GROUPS_A):
                e = jnp.exp2(ms[g] - mx)
                num = num + e * acc_ref[2 * g + hp, rows, :]
                den = den + e * l_ref[2 * g + hp, rows, :]
            ya_ref[0, rows, hp * LANES:(hp + 1) * LANES] = (num / den).astype(BF16)
        return carry
    lax.fori_loop(0, seq // rc, merge_body, 0)


def _attn_a_call(a0, a1, a2):
    b, seq, _ = a0.shape
    return pl.pallas_call(
        _attn_a_kernel,
        grid=(b,),
        in_specs=[
            pl.BlockSpec((1, seq, A_GROUP_W), lambda i: (i, 0, 0)),
            pl.BlockSpec((1, seq // 4, 4 * A_GROUP_W), lambda i: (i, 0, 0)),
            pl.BlockSpec((1, seq // 16, 16 * A_GROUP_W), lambda i: (i, 0, 0)),
        ],
        out_specs=pl.BlockSpec((1, seq, A_OUT_W), lambda i: (i, 0, 0)),
        out_shape=jax.ShapeDtypeStruct((b, seq, A_OUT_W), BF16),
        scratch_shapes=[pltpu.VMEM((N_GROUPS_A * A_OUT_W // LANES, seq, LANES), F32)] * 3,
        compiler_params=_cparams(("arbitrary",)),
        name="attn_a",
    )(a0, a1, a2)


def _attn_b_kernel(q_ref, kv_ref, o_ref):
    tq = q_ref.shape[1]
    seq = kv_ref.shape[1]
    low = lax.broadcasted_iota(I32, (tq, LANES), 1) < HEAD_DIM
    low_k = lax.broadcasted_iota(I32, (TK_B, LANES), 1) < HEAD_DIM
    for pr in range(B_Q_W // LANES):
        q2 = q_ref[0, :, pr * LANES:(pr + 1) * LANES]
        zero = jnp.zeros_like(q2)
        qs = jnp.concatenate([jnp.where(low, q2, zero), jnp.where(low, zero, q2)], axis=0)
        m = jnp.full((2 * tq, 1), NEG_BIG, F32)
        acc_a = jnp.zeros((tq, LANES), F32)
        acc_b = jnp.zeros((tq, LANES), F32)
        for c in range(seq // TK_B):
            k2 = kv_ref[0, c * TK_B:(c + 1) * TK_B, 0:LANES]
            v2 = kv_ref[0, c * TK_B:(c + 1) * TK_B, LANES:2 * LANES]
            one = jnp.ones_like(v2)
            va = jnp.where(low_k, v2, one)
            vb = jnp.where(low_k, one, v2)
            s = lax.dot_general(qs, k2, (((1,), (1,)), ((), ())), preferred_element_type=F32)
            m_n = jnp.maximum(m, jnp.max(s, axis=-1, keepdims=True))
            p = jnp.exp2(s - m_n).astype(BF16)
            a = jnp.exp2(m - m_n)
            acc_a = a[:tq] * acc_a + jnp.dot(p[:tq], va, preferred_element_type=F32)
            acc_b = a[tq:] * acc_b + jnp.dot(p[tq:], vb, preferred_element_type=F32)
            m = m_n
        oa = acc_a / pltpu.roll(acc_a, HEAD_DIM, 1)
        ob = acc_b / pltpu.roll(acc_b, HEAD_DIM, 1)
        o_ref[0, :, pr * LANES:(pr + 1) * LANES] = jnp.where(low, oa, ob).astype(BF16)


def _attn_b_call(qb, kvb):
    b, seq, _ = qb.shape
    return pl.pallas_call(
        _attn_b_kernel,
        grid=(b, seq // TQ_B),
        in_specs=[
            pl.BlockSpec((1, TQ_B, B_Q_W), lambda i, j: (i, j, 0)),
            pl.BlockSpec((1, seq, B_KV_W), lambda i, j: (i, 0, 0)),
        ],
        out_specs=pl.BlockSpec((1, TQ_B, B_Q_W), lambda i, j: (i, j, 0)),
        out_shape=jax.ShapeDtypeStruct((b, seq, B_Q_W), BF16),
        compiler_params=_cparams(("arbitrary",) * 2),
        name="attn_b",
    )(qb, kvb)


def _pack_bf16_pairs(v):
    half = v.shape[1] // 2
    lo = lax.bitcast_convert_type(v[:, :half].astype(BF16).astype(F32), U32)
    hi = lax.bitcast_convert_type(v[:, half:].astype(BF16).astype(F32), U32)
    return (lo >> 16) | (hi & jnp.uint32(0xFFFF0000))


def _unpack_bf16_pairs(p):
    lo = lax.bitcast_convert_type(p << 16, F32)
    hi = lax.bitcast_convert_type(p & jnp.uint32(0xFFFF0000), F32)
    return lo, hi


def _merge_kernel(x_ref, ya_ref, yb_ref, g1_ref, wg_ref, wba_ref, wbb_ref, wo_ref, g2_ref,
                  wrh_ref, wrl_ref, bias_ref, wsgu_ref, wsd_ref, tri_ref,
                  base_ref, xp_ref, eidx_ref, rank_ref, wgt_ref, cnt_ref, carry_ref):
    step = pl.program_id(0)

    @pl.when(step == 0)
    def _():
        carry_ref[...] = jnp.zeros_like(carry_ref)

    x = x_ref[...]
    tm = x.shape[0]
    ms = jnp.mean(x * x, axis=-1, keepdims=True)
    xn = (x * lax.rsqrt(ms + EPS) * g1_ref[...]).astype(BF16)
    gates = jax.nn.sigmoid(jnp.dot(xn, wg_ref[...], preferred_element_type=F32))
    pa = jnp.dot(ya_ref[...], wba_ref[...], preferred_element_type=F32)
    pb = jnp.dot(yb_ref[...], wbb_ref[...], preferred_element_type=F32)
    merged = gates[:, :D_MODEL] * pa + gates[:, D_MODEL:] * pb
    h = x + jnp.dot(merged.astype(BF16), wo_ref[...], preferred_element_type=F32)

    ms2 = jnp.mean(h * h, axis=-1, keepdims=True)
    xn2 = h * lax.rsqrt(ms2 + EPS) * g2_ref[...]
    xn2b = xn2.astype(BF16)
    xp_ref[...] = _pack_bf16_pairs(xn2)

    gu = jnp.dot(xn2b, wsgu_ref[...], preferred_element_type=F32)
    hs = (jax.nn.silu(gu[:, :EXPERT_FF]) * gu[:, EXPERT_FF:]).astype(BF16)
    base_ref[...] = h + jnp.dot(hs, wsd_ref[...], preferred_element_type=F32)

    xlo = (xn2 - xn2b.astype(F32)).astype(BF16)
    logits = (jnp.dot(xn2b, wrh_ref[...], preferred_element_type=F32)
              + jnp.dot(xlo, wrh_ref[...], preferred_element_type=F32)
              + jnp.dot(xn2b, wrl_ref[...], preferred_element_type=F32))
    scores = jax.nn.sigmoid(logits.T[:N_EXPERTS, :])
    sel = scores + bias_ref[...]

    gsz = N_EXPERTS // N_EXPERT_GROUPS
    iota8 = lax.broadcasted_iota(I32, (gsz, tm), 0).astype(F32)
    ninf = jnp.float32(-jnp.inf)
    blocks, gs = [], []
    for g in range(N_EXPERT_GROUPS):
        blk = sel[g * gsz:(g + 1) * gsz, :]
        m1 = jnp.max(blk, axis=0, keepdims=True)
        first = jnp.min(jnp.where(blk == m1, iota8, float(gsz)), axis=0, keepdims=True)
        m2 = jnp.max(jnp.where(iota8 == first, ninf, blk), axis=0, keepdims=True)
        blocks.append(blk)
        gs.append(m1 + m2)
    cur = jnp.concatenate(gs, axis=0)
    gmask = jnp.zeros((N_EXPERT_GROUPS, tm), F32)
    for _ in range(TOPK_GROUPS):
        mx = jnp.max(cur, axis=0, keepdims=True)
        fi = jnp.min(jnp.where(cur == mx, iota8, float(N_EXPERT_GROUPS)), axis=0, keepdims=True)
        pick = iota8 == fi
        gmask = jnp.where(pick, 1.0, gmask)
        cur = jnp.where(pick, ninf, cur)
    cur = jnp.concatenate(
        [jnp.where(gmask[g:g + 1, :] > 0.5, blocks[g], ninf) for g in range(N_EXPERT_GROUPS)], axis=0)

    iota64 = lax.broadcasted_iota(I32, (N_EXPERTS, tm), 0).astype(F32)
    idxs, wks = [], []
    onehot = jnp.zeros((N_EXPERTS, tm), F32)
    for _ in range(TOP_K):
        mx = jnp.max(cur, axis=0, keepdims=True)
        fi = jnp.min(jnp.where(cur == mx, iota64, float(N_EXPERTS)), axis=0, keepdims=True)
        pick = iota64 == fi
        wks.append(jnp.sum(jnp.where(pick, scores, 0.0), axis=0, keepdims=True))
        idxs.append(fi)
        onehot = jnp.where(pick, 1.0, onehot)
        cur = jnp.where(pick, ninf, cur)
    wk = jnp.concatenate(wks, axis=0)
    wgt_ref[...] = wk / jnp.sum(wk, axis=0, keepdims=True) * ROUTED_SCALE
    eidx_ref[...] = jnp.concatenate(idxs, axis=0).astype(I32)

    before = jnp.dot(onehot.astype(BF16), tri_ref[...], preferred_element_type=F32) + carry_ref[:, 0:1]
    rank_ref[...] = jnp.concatenate(
        [jnp.sum(jnp.where(iota64 == fi, before, 0.0), axis=0, keepdims=True) for fi in idxs],
        axis=0).astype(I32)
    carry_ref[...] = carry_ref[...] + jnp.sum(onehot, axis=1, keepdims=True)
    cnt_ref[...] = carry_ref[...]


def _merge_call(x2d, ya, yb, g1, wg, wba, wbb, wo, g2, wrh, wrl, bias, wsgu, wsd, tri, row0, n):
    tm = TM_MERGE
    off = row0 // tm
    full = lambda shape: pl.BlockSpec(shape, lambda i: (0,) * len(shape))
    row_in = lambda w: pl.BlockSpec((tm, w), lambda i: (i + off, 0))
    row = lambda w: pl.BlockSpec((tm, w), lambda i: (i, 0))
    col = lambda: pl.BlockSpec((TOP_K, tm), lambda i: (0, i))
    return pl.pallas_call(
        _merge_kernel,
        grid=(n // tm,),
        in_specs=[
            row_in(D_MODEL), row_in(A_OUT_W), row_in(B_Q_W),
            full((1, D_MODEL)), full((D_MODEL, 2 * D_MODEL)),
            full((A_OUT_W, D_MODEL)), full((B_Q_W, D_MODEL)), full((D_MODEL, D_MODEL)),
            full((1, D_MODEL)),
            full((D_MODEL, LANES)), full((D_MODEL, LANES)), full((N_EXPERTS, 1)),
            full((D_MODEL, 2 * EXPERT_FF)), full((EXPERT_FF, D_MODEL)),
            full((tm, tm)),
        ],
        out_specs=[row(D_MODEL), row(D_MODEL // 2), col(), col(), col(), full((N_EXPERTS, LANES))],
        out_shape=[
            jax.ShapeDtypeStruct((n, D_MODEL), F32),
            jax.ShapeDtypeStruct((n, D_MODEL // 2), U32),
            jax.ShapeDtypeStruct((TOP_K, n), I32),
            jax.ShapeDtypeStruct((TOP_K, n), I32),
            jax.ShapeDtypeStruct((TOP_K, n), F32),
            jax.ShapeDtypeStruct((N_EXPERTS, LANES), F32),
        ],
        scratch_shapes=[pltpu.VMEM((N_EXPERTS, LANES), F32)],
        compiler_params=_cparams(("arbitrary",)),
        name="merge_router",
    )(x2d, ya, yb, g1, wg, wba, wbb, wo, g2, wrh, wrl, bias, wsgu, wsd, tri)


EXPERT_SUBBLOCKS = 2


def _expert_kernel(be_ref, nb_ref, xs_ref, *refs):
    w_refs, ys_ref = refs[:-1], refs[-1]
    first = pl.program_id(0) * EXPERT_SUBBLOCKS

    @pl.when(first >= nb_ref[0])
    def _():
        ys_ref[...] = jnp.zeros_like(ys_ref)

    @pl.when(first < nb_ref[0])
    def _():
        half = D_MODEL // 2
        for sub in range(EXPERT_SUBBLOCKS):
            wgu_ref, wd_ref = w_refs[2 * sub], w_refs[2 * sub + 1]
            rows = slice(sub * BM_EXPERT, (sub + 1) * BM_EXPERT)
            lo, hi = _unpack_bf16_pairs(xs_ref[rows, :])
            gu = (jnp.dot(lo.astype(BF16), wgu_ref[0, :half, :], preferred_element_type=F32)
                  + jnp.dot(hi.astype(BF16), wgu_ref[0, half:, :], preferred_element_type=F32))
            hb = (jax.nn.silu(gu[:, :EXPERT_FF]) * gu[:, EXPERT_FF:]).astype(BF16)
            ys_ref[rows, :] = _pack_bf16_pairs(jnp.dot(hb, wd_ref[0], preferred_element_type=F32))


def _expert_call(block_e, n_used, xs, wgu, wd):
    p_len = xs.shape[0]
    nsub = EXPERT_SUBBLOCKS
    rows = nsub * BM_EXPERT
    w_specs, w_args = [], []
    for sub in range(nsub):
        w_specs += [
            pl.BlockSpec((1, D_MODEL, 2 * EXPERT_FF), lambda j, be, nb, sub=sub: (be[nsub * j + sub], 0, 0)),
            pl.BlockSpec((1, EXPERT_FF, D_MODEL), lambda j, be, nb, sub=sub: (be[nsub * j + sub], 0, 0)),
        ]
        w_args += [wgu, wd]
    return pl.pallas_call(
        _expert_kernel,
        grid_spec=pltpu.PrefetchScalarGridSpec(
            num_scalar_prefetch=2,
            grid=(p_len // rows,),
            in_specs=[pl.BlockSpec((rows, D_MODEL // 2), lambda j, be, nb: (j, 0))] + w_specs,
            out_specs=pl.BlockSpec((rows, D_MODEL // 2), lambda j, be, nb: (j, 0)),
        ),
        out_shape=jax.ShapeDtypeStruct((p_len, D_MODEL // 2), U32),
        compiler_params=_cparams(("arbitrary",)),
        name="routed_experts",
    )(block_e, n_used, xs, *w_args)


SC_CORES = 2
SC_SUBCORES = 16
SC_WORKERS = SC_CORES * SC_SUBCORES
SC_CHUNK = 128


def _sc_mesh():
    return plsc.VectorSubcoreMesh(core_axis_name="c", subcore_axis_name="s",
                                  num_cores=SC_CORES, num_subcores=SC_SUBCORES)


def _dispatch_rows(xp, dest3, p_len):
    n, width = xp.shape
    n_chunks = dest3.shape[0]
    per_worker = n_chunks // SC_WORKERS

    @functools.partial(
        pl.kernel, mesh=_sc_mesh(),
        out_type=jax.ShapeDtypeStruct((p_len, width), xp.dtype),
        scratch_types=[pltpu.VMEM((TOP_K, SC_CHUNK), I32), pltpu.VMEM((SC_CHUNK, width), xp.dtype),
                       pltpu.SemaphoreType.DMA],
        name="sc_dispatch")
    def body(xp_hbm, dest_hbm, xs_hbm, idx_v, rows_v, sem):
        wid = lax.axis_index("s") * SC_CORES + lax.axis_index("c")

        @pl.loop(0, per_worker)
        def _(j):
            chunk = wid * per_worker + j
            pltpu.sync_copy(dest_hbm.at[chunk], idx_v)
            pltpu.sync_copy(xp_hbm.at[pl.ds(chunk * SC_CHUNK, SC_CHUNK)], rows_v)
            copies = [pltpu.async_copy(rows_v, xs_hbm.at[idx_v.at[k]], sem) for k in range(TOP_K)]
            for cp in copies:
                cp.wait()

    return body(xp, dest3)


SC_GATHER_CHUNK = 64


def _gather_rows(ys, dest3):
    width = ys.shape[1]
    n_chunks, _, gc = dest3.shape
    n = n_chunks * gc
    per_worker = n_chunks // SC_WORKERS

    @functools.partial(
        pl.kernel, mesh=_sc_mesh(),
        out_type=jax.ShapeDtypeStruct((TOP_K, n, width), ys.dtype),
        scratch_types=[pltpu.VMEM((TOP_K, gc), I32),
                       pltpu.VMEM((gc, width), ys.dtype), pltpu.VMEM((gc, width), ys.dtype),
                       pltpu.SemaphoreType.DMA, pltpu.SemaphoreType.DMA,
                       pltpu.SemaphoreType.DMA, pltpu.SemaphoreType.DMA],
        name="sc_gather")
    def body(ys_hbm, dest_hbm, out_hbm, idx_v, buf0, buf1, gsem0, gsem1, wsem0, wsem1):
        wid = lax.axis_index("s") * SC_CORES + lax.axis_index("c")
        bufs, gsems, wsems = (buf0, buf1), (gsem0, gsem1), (wsem0, wsem1)

        @pl.loop(0, per_worker)
        def _(j):
            chunk = wid * per_worker + j
            pltpu.sync_copy(dest_hbm.at[chunk], idx_v)
            rows = pl.ds(chunk * gc, gc)
            gathers = [None] * TOP_K
            writes = [None] * TOP_K
            gathers[0] = pltpu.async_copy(ys_hbm.at[idx_v.at[0]], bufs[0], gsems[0])
            for k in range(TOP_K):
                gathers[k].wait()
                if k >= 1:
                    writes[k - 1].wait()
                if k + 1 < TOP_K:
                    nxt = (k + 1) % 2
                    gathers[k + 1] = pltpu.async_copy(ys_hbm.at[idx_v.at[k + 1]], bufs[nxt], gsems[nxt])
                writes[k] = pltpu.async_copy(bufs[k % 2], out_hbm.at[k, rows], wsems[k % 2])
            writes[TOP_K - 1].wait()

    return body(ys, dest3)


def _combine_kernel(base_ref, g_ref, w_ref, *refs):
    o_ref = refs[-1]
    half = D_MODEL // 2
    lo_acc = base_ref[:, :half]
    hi_acc = base_ref[:, half:]
    for k in range(TOP_K):
        lo, hi = _unpack_bf16_pairs(g_ref[k])
        wk = w_ref[:, k:k + 1]
        lo_acc = lo_acc + wk * lo
        hi_acc = hi_acc + wk * hi
    o_ref[:, :half] = lo_acc
    o_ref[:, half:] = hi_acc


def _combine_call(base, gathered, w_t, prev, row0, n_total):
    n = base.shape[0]
    tm = TM_COMBINE
    off = row0 // tm
    in_specs = [
        pl.BlockSpec((tm, D_MODEL), lambda i: (i, 0)),
        pl.BlockSpec((TOP_K, tm, D_MODEL // 2), lambda i: (0, i, 0)),
        pl.BlockSpec((tm, TOP_K), lambda i: (i, 0)),
    ]
    args = [base, gathered, w_t]
    aliases = {}
    if prev is not None:
        in_specs.append(pl.BlockSpec(memory_space=pl.ANY))
        args.append(prev)
        aliases = {3: 0}
    return pl.pallas_call(
        _combine_kernel,
        grid=(n // tm,),
        in_specs=in_specs,
        out_specs=pl.BlockSpec((tm, D_MODEL), lambda i: (i + off, 0)),
        out_shape=jax.ShapeDtypeStruct((n_total, D_MODEL), F32),
        input_output_aliases=aliases,
        compiler_params=_cparams(("arbitrary",)),
        name="moe_combine",
    )(*args)


def _prep_tables(seq):
    pos = jnp.arange(seq)
    cos1, sin1 = _rope_tables(pos, HEAD_DIM)
    cos_r, sin_r = _rope_tables(pos // GRID_W, HEAD_DIM // 2)
    cos_c, sin_c = _rope_tables(pos % GRID_W, HEAD_DIM // 2)
    taba = _lane_tables(cos1, sin1, HEAD_DIM // 2)
    tabb = _lane_tables(jnp.concatenate([cos_r, cos_c], -1), jnp.concatenate([sin_r, sin_c], -1),
                        HEAD_DIM // 4)
    return taba, tabb


def _layer(h, p, taba, tabb):
    b, seq, d = h.shape
    n = b * seq
    x2d = h.reshape(n, d)
    w_in = p["w_in"]
    o1 = N_GROUPS_A * A_GROUP_W
    o2 = o1 + B_Q_W
    o3 = o2 + B_KV_W
    pair_heads = jnp.array([0, 4, 1, 5, 2, 6, 3, 7])
    pair_cols = (pair_heads[:, None] * HEAD_DIM + jnp.arange(HEAD_DIM)[None, :]).reshape(-1)
    wqkv = jnp.concatenate([w_in[:, :o1], w_in[:, o1:o2][:, pair_cols], w_in[:, o2:o3]], axis=1).astype(BF16)
    scale = Q_SCALE
    qa = jnp.tile(p["q_norm_a"] * scale, 4)
    ka = jnp.tile(p["k_norm_a"], 4)
    ones = jnp.ones((A_OUT_W,), F32)
    gain = jnp.concatenate([jnp.concatenate([qa, ka, ones])] * N_GROUPS_A
                           + [jnp.tile(p["q_norm_b"] * scale, 8), jnp.tile(p["k_norm_b"], 2),
                              jnp.ones((LANES,), F32)]).reshape(1, QKV_W).astype(F32)
    seg = jnp.arange(LANES) // HEAD_DIM
    bd = (seg[:, None] == seg[None, :]).astype(BF16)
    g1 = p["norm1_g"].reshape(1, d).astype(F32)

    a0, a1, a2, qb, kvb = _proj_call(x2d, g1, wqkv, gain, taba, tabb, bd, seq)
    ya = _attn_a_call(a0.reshape(b, seq, A_GROUP_W), a1.reshape(b, seq // 4, 4 * A_GROUP_W),
                      a2.reshape(b, seq // 16, 16 * A_GROUP_W))
    yb = _attn_b_call(qb.reshape(b, seq, B_Q_W), kvb.reshape(b, seq, B_KV_W))

    wg = w_in[:, o3:].astype(BF16)
    wbb = p["w_branch_b"][pair_cols, :].astype(BF16)
    wr = jnp.pad(p["w_router"], ((0, 0), (0, LANES - N_EXPERTS)))
    wrh = wr.astype(BF16)
    wrl = (wr - wrh.astype(F32)).astype(BF16)
    wsgu = jnp.concatenate([p["ws_gate"], p["ws_up"]], axis=1).astype(BF16)
    tm = TM_MERGE
    tri = (jnp.arange(tm)[:, None] < jnp.arange(tm)[None, :]).astype(BF16)
    wba = p["w_branch_a"].astype(BF16)
    wo = p["w_out"].astype(BF16)
    g2 = p["norm2_g"].reshape(1, d).astype(F32)
    bias = p["router_bias"].reshape(N_EXPERTS, 1).astype(F32)
    wsd = p["ws_down"].astype(BF16)
    wgu = jnp.concatenate([p["we_gate"], p["we_up"]], axis=-1).astype(BF16)
    wd = p["we_down"].astype(BF16)
    ya2d = ya.reshape(n, A_OUT_W)
    yb2d = yb.reshape(n, B_Q_W)

    nc = n // MOE_CHUNKS
    bm = BM_EXPERT
    n_blocks = (nc * TOP_K + N_EXPERTS * (bm - 1)) // bm + 1
    n_blocks = -(-n_blocks // EXPERT_SUBBLOCKS) * EXPERT_SUBBLOCKS
    p_len = n_blocks * bm
    block_start = jnp.arange(n_blocks, dtype=I32) * bm
    out = None
    for c in range(MOE_CHUNKS):
        base, xp, eidx, rank, wgt, cnt = _merge_call(
            x2d, ya2d, yb2d, g1, wg, wba, wbb, wo, g2, wrh, wrl, bias, wsgu, wsd, tri, c * nc, nc)
        counts = cnt[:, 0].astype(I32)
        padded = (counts + bm - 1) // bm * bm
        pend = jnp.cumsum(padded)
        pstart = pend - padded
        onehot = eidx[:, :, None] == jnp.arange(N_EXPERTS)[None, None, :]
        dest = jnp.sum(jnp.where(onehot, pstart[None, None, :], 0), axis=-1) + rank
        block_e = jnp.minimum(jnp.sum(pend[None, :] <= block_start[:, None], axis=1),
                              N_EXPERTS - 1).astype(I32)
        n_used = (pend[-1] // bm).astype(I32).reshape(1)
        dest3 = dest.reshape(TOP_K, nc // SC_CHUNK, SC_CHUNK).transpose(1, 0, 2)
        dest3g = dest.reshape(TOP_K, nc // SC_GATHER_CHUNK, SC_GATHER_CHUNK).transpose(1, 0, 2)

        xs = _dispatch_rows(xp, dest3, p_len)
        ys = _expert_call(block_e, n_used, xs, wgu, wd)
        gathered = _gather_rows(ys, dest3g)
        out = _combine_call(base, gathered, wgt.T, out, c * nc, n)
    return out.reshape(b, seq, d)


def kernel(x, norm1_g, w_in, q_norm_a, k_norm_a, q_norm_b, k_norm_b, w_branch_a, w_branch_b, w_out,
           norm2_g, w_router, router_bias, we_gate, we_up, we_down, ws_gate, ws_up, ws_down):
    params = dict(norm1_g=norm1_g, w_in=w_in, q_norm_a=q_norm_a, k_norm_a=k_norm_a, q_norm_b=q_norm_b,
                  k_norm_b=k_norm_b, w_branch_a=w_branch_a, w_branch_b=w_branch_b, w_out=w_out,
                  norm2_g=norm2_g, w_router=w_router, router_bias=router_bias, we_gate=we_gate,
                  we_up=we_up, we_down=we_down, ws_gate=ws_gate, ws_up=ws_up, ws_down=ws_down)
    taba, tabb = _prep_tables(x.shape[1])
    h = x
    for l in range(norm1_g.shape[0]):
        h = _layer(h, {k: v[l] for k, v in params.items()}, taba, tabb)
    return h
```

```python
import functools

import jax
import jax.numpy as jnp
from jax import lax
from jax.experimental import pallas as pl
from jax.experimental.pallas import tpu as pltpu
from jax.experimental.pallas import tpu_sc as plsc

F32 = jnp.float32
BF16 = jnp.bfloat16
I32 = jnp.int32
U32 = jnp.uint32

D_MODEL = 1024
HEAD_DIM = 64
ROPE_THETA = 10000.0
EPS = 1e-6
GRID_W = 64
DILATIONS = (1, 4, 16)
RADIUS = 64
N_GROUPS_A = 3
A_GROUP_W = 768
A_OUT_W = 256
B_Q_W = 512
B_KV_W = 256
QKV_W = N_GROUPS_A * A_GROUP_W + B_Q_W + B_KV_W
N_EXPERTS = 64
N_EXPERT_GROUPS = 8
TOPK_GROUPS = 4
TOP_K = 8
EXPERT_FF = 256
ROUTED_SCALE = 2.5

LANES = 128
NEG_BIG = -1e30
Q_SCALE = HEAD_DIM ** -0.5 * 1.4426950408889634

TM_PROJ = 512
TQ_B = 256
TK_B = 512
TM_MERGE = 256
BM_EXPERT = 512
TM_COMBINE = 512
MOE_CHUNKS = 2
VMEM_LIMIT = 56 * 1024 * 1024


def _cparams(sem):
    return pltpu.CompilerParams(dimension_semantics=sem, vmem_limit_bytes=VMEM_LIMIT)


def _rope_tables(pos, dim):
    inv = ROPE_THETA ** (-jnp.arange(0, dim, 2, dtype=F32) / dim)
    ang = pos.astype(F32)[:, None] * inv[None, :]
    ang = jnp.concatenate([ang, ang], axis=-1)
    return jnp.cos(ang), jnp.sin(ang)


def _lane_tables(cos_h, sin_h, half):
    cos2 = jnp.concatenate([cos_h, cos_h], axis=-1)
    sin2 = jnp.concatenate([sin_h, sin_h], axis=-1)
    first = (jnp.arange(LANES) % (2 * half)) < half
    s_left = jnp.where(first[None, :], -sin2, 0.0)
    s_right = jnp.where(first[None, :], 0.0, sin2)
    return jnp.stack([cos2, s_left, s_right]).astype(F32)


TAB_AQ, TAB_AK, TAB_BQ, TAB_BK = range(4)


def _proj_kernel(x_ref, g1_ref, w_ref, tab_ref, bd_ref,
                 a0_ref, a1_ref, a2_ref, qb_ref, kvb_ref, pbuf_ref):
    x = x_ref[...]
    ms = jnp.mean(x * x, axis=-1, keepdims=True)
    xn = (x * lax.rsqrt(ms + EPS) * g1_ref[...]).astype(BF16)
    tm = x.shape[0]

    def norm_rope(y, t, shift):
        w = y.shape[1]
        ss = jnp.dot((y * y).astype(BF16), bd_ref[0:w, 0:w], preferred_element_type=F32)
        yn = y * lax.rsqrt(ss * (1.0 / HEAD_DIM) + EPS)
        out = []
        for i in range(w // LANES):
            z = yn[:, i * LANES:(i + 1) * LANES]
            out.append(z * tab_ref[t, 0] + pltpu.roll(z, LANES - shift, 1) * tab_ref[t, 1]
                       + pltpu.roll(z, shift, 1) * tab_ref[t, 2])
        return out

    for g, d in enumerate(DILATIONS):
        base = g * A_GROUP_W
        r = jnp.dot(xn, w_ref[:, base:base + A_GROUP_W], preferred_element_type=F32)
        chunks = (norm_rope(r[:, 0:2 * LANES], TAB_AQ, HEAD_DIM // 2)
                  + norm_rope(r[:, 2 * LANES:4 * LANES], TAB_AK, HEAD_DIM // 2)
                  + [r[:, 4 * LANES:5 * LANES], r[:, 5 * LANES:6 * LANES]])
        for c, y in enumerate(chunks):
            if d == 1:
                a0_ref[:, c * LANES:(c + 1) * LANES] = y.astype(BF16)
            else:
                pbuf_ref[c] = y
        if d > 1:
            out_ref = a1_ref if g == 1 else a2_ref
            rows = tm // d
            for res in range(d):
                for c in range(A_GROUP_W // LANES):
                    col = res * A_GROUP_W + c * LANES
                    out_ref[:, col:col + LANES] = pbuf_ref[c, pl.ds(res, rows, stride=d), :].astype(BF16)

    base = N_GROUPS_A * A_GROUP_W
    r = jnp.dot(xn, w_ref[:, base:base + B_Q_W + B_KV_W], preferred_element_type=F32)
    chunks = (norm_rope(r[:, 0:2 * LANES], TAB_BQ, HEAD_DIM // 4)
              + norm_rope(r[:, 2 * LANES:4 * LANES], TAB_BQ, HEAD_DIM // 4)
              + norm_rope(r[:, 4 * LANES:5 * LANES], TAB_BK, HEAD_DIM // 4)
              + [r[:, 5 * LANES:6 * LANES]])
    for c, y in enumerate(chunks):
        if c < 4:
            qb_ref[:, c * LANES:(c + 1) * LANES] = y.astype(BF16)
        else:
            kvb_ref[:, (c - 4) * LANES:(c - 3) * LANES] = y.astype(BF16)


def _proj_call(x2d, g1, wqkv, tabs, bd, seq):
    n = x2d.shape[0]
    tm = TM_PROJ
    tiles_per_seq = seq // tm
    full = lambda shape: pl.BlockSpec(shape, lambda i: (0,) * len(shape))
    return pl.pallas_call(
        _proj_kernel,
        grid=(n // tm,),
        in_specs=[
            pl.BlockSpec((tm, D_MODEL), lambda i: (i, 0)),
            full((1, D_MODEL)),
            full((D_MODEL, QKV_W)),
            pl.BlockSpec((4, 3, tm, LANES), lambda i: (0, 0, i % tiles_per_seq, 0)),
            full((2 * LANES, 2 * LANES)),
        ],
        out_specs=[
            pl.BlockSpec((tm, A_GROUP_W), lambda i: (i, 0)),
            pl.BlockSpec((tm // 4, 4 * A_GROUP_W), lambda i: (i, 0)),
            pl.BlockSpec((tm // 16, 16 * A_GROUP_W), lambda i: (i, 0)),
            pl.BlockSpec((tm, B_Q_W), lambda i: (i, 0)),
            pl.BlockSpec((tm, B_KV_W), lambda i: (i, 0)),
        ],
        out_shape=[
            jax.ShapeDtypeStruct((n, A_GROUP_W), BF16),
            jax.ShapeDtypeStruct((n // 4, 4 * A_GROUP_W), BF16),
            jax.ShapeDtypeStruct((n // 16, 16 * A_GROUP_W), BF16),
            jax.ShapeDtypeStruct((n, B_Q_W), BF16),
            jax.ShapeDtypeStruct((n, B_KV_W), BF16),
        ],
        scratch_shapes=[pltpu.VMEM((A_GROUP_W // LANES, tm, LANES), F32)],
        compiler_params=_cparams(("arbitrary",)),
        name="proj_qkv",
    )(x2d, g1, wqkv, tabs, bd)


def _attn_a_kernel(a0_ref, a1_ref, a2_ref, ya_ref, acc_ref, m_ref, l_ref):
    tq = 128
    lane = lax.broadcasted_iota(I32, (tq, LANES), 1)
    low = lane < HEAD_DIM

    def tile(g, ref, col0, length, i, rows):
        win = min(2 * tq, length)
        if isinstance(i, int):
            q0 = i * tq
            ws = max(0, min(q0 - RADIUS, length - win))
        else:
            q0 = pl.multiple_of(i * tq, tq)
            ws = pl.multiple_of(jnp.clip(q0 - RADIUS, 0, length - win), RADIUS)
        qrow = lax.broadcasted_iota(I32, (2 * tq, win), 0)
        qpos = q0 + jnp.where(qrow >= tq, qrow - tq, qrow)
        kpos = ws + lax.broadcasted_iota(I32, (2 * tq, win), 1)
        valid = jnp.abs(qpos - kpos) <= RADIUS
        for hp in range(2):
            q2 = ref[0, pl.ds(q0, tq), col0 + hp * LANES:col0 + (hp + 1) * LANES]
            k2 = ref[0, pl.ds(ws, win), col0 + 256 + hp * LANES:col0 + 256 + (hp + 1) * LANES]
            v2 = ref[0, pl.ds(ws, win), col0 + 512 + hp * LANES:col0 + 512 + (hp + 1) * LANES]
            zero = jnp.zeros_like(q2)
            qs = jnp.concatenate([jnp.where(low, q2, zero), jnp.where(low, zero, q2)], axis=0)
            s = lax.dot_general(qs, k2, (((1,), (1,)), ((), ())), preferred_element_type=F32)
            s = jnp.where(valid, s, NEG_BIG)
            m = jnp.max(s, axis=-1, keepdims=True)
            p = jnp.exp2(s - m)
            l = jnp.sum(p, axis=-1, keepdims=True)
            pv = jnp.dot(p.astype(BF16), v2, preferred_element_type=F32)
            slot = 2 * g + hp
            acc_ref[slot, rows, :] = jnp.where(low, pv[:tq], pv[tq:])
            m_ref[slot, rows, :] = jnp.where(low, m[:tq], m[tq:])
            l_ref[slot, rows, :] = jnp.where(low, l[:tq], l[tq:])

    seq = a0_ref.shape[1]

    def g0_body(i, carry):
        tile(0, a0_ref, 0, seq, i, pl.ds(pl.multiple_of(i * tq, tq), tq))
        return carry
    lax.fori_loop(0, seq // tq, g0_body, 0, unroll=4)

    len1 = seq // 4
    for res in range(4):
        def g1_body(i, carry, res=res):
            tile(1, a1_ref, res * A_GROUP_W, len1, i, pl.ds(res + 4 * i * tq, tq, stride=4))
            return carry
        lax.fori_loop(0, len1 // tq, g1_body, 0, unroll=4)

    len2 = seq // 16
    for res in range(16):
        tile(2, a2_ref, res * A_GROUP_W, len2, 0, pl.ds(res, tq, stride=16))

    rc = 256

    def merge_body(j, carry):
        rows = pl.ds(pl.multiple_of(j * rc, rc), rc)
        for hp in range(2):
            ms = [m_ref[2 * g + hp, rows, :] for g in range(N_GROUPS_A)]
            mx = jnp.maximum(jnp.maximum(ms[0], ms[1]), ms[2])
            num = jnp.zeros((rc, LANES), F32)
            den = jnp.zeros((rc, LANES), F32)
            for g in range(N_GROUPS_A):
                e = jnp.exp2(ms[g] - mx)
                num = num + e * acc_ref[2 * g + hp, rows, :]
                den = den + e * l_ref[2 * g + hp, rows, :]
            ya_ref[0, rows, hp * LANES:(hp + 1) * LANES] = (num / den).astype(BF16)
        return carry
    lax.fori_loop(0, seq // rc, merge_body, 0)


def _attn_a_call(a0, a1, a2):
    b, seq, _ = a0.shape
    return pl.pallas_call(
        _attn_a_kernel,
        grid=(b,),
        in_specs=[
            pl.BlockSpec((1, seq, A_GROUP_W), lambda i: (i, 0, 0)),
            pl.BlockSpec((1, seq // 4, 4 * A_GROUP_W), lambda i: (i, 0, 0)),
            pl.BlockSpec((1, seq // 16, 16 * A_GROUP_W), lambda i: (i, 0, 0)),
        ],
        out_specs=pl.BlockSpec((1, seq, A_OUT_W), lambda i: (i, 0, 0)),
        out_shape=jax.ShapeDtypeStruct((b, seq, A_OUT_W), BF16),
        scratch_shapes=[pltpu.VMEM((N_GROUPS_A * A_OUT_W // LANES, seq, LANES), F32)] * 3,
        compiler_params=_cparams(("arbitrary",)),
        name="attn_a",
    )(a0, a1, a2)


def _attn_b_kernel(q_ref, kv_ref, o_ref, va_ref, vb_ref):
    tq = q_ref.shape[1]
    seq = kv_ref.shape[1]
    low = lax.broadcasted_iota(I32, (tq, LANES), 1) < HEAD_DIM

    @pl.when(pl.program_id(1) == 0)
    def _():
        low_k = lax.broadcasted_iota(I32, (seq, LANES), 1) < HEAD_DIM
        v2 = kv_ref[0, :, LANES:2 * LANES]
        one = jnp.ones_like(v2)
        va_ref[...] = jnp.where(low_k, v2, one)
        vb_ref[...] = jnp.where(low_k, one, v2)

    for pr in range(B_Q_W // LANES):
        q2 = q_ref[0, :, pr * LANES:(pr + 1) * LANES]
        zero = jnp.zeros_like(q2)
        qs = jnp.concatenate([jnp.where(low, q2, zero), jnp.where(low, zero, q2)], axis=0)
        m = jnp.full((2 * tq, 1), NEG_BIG, F32)
        acc_a = jnp.zeros((tq, LANES), F32)
        acc_b = jnp.zeros((tq, LANES), F32)
        for c in range(seq // TK_B):
            k2 = kv_ref[0, c * TK_B:(c + 1) * TK_B, 0:LANES]
            va = va_ref[c * TK_B:(c + 1) * TK_B, :]
            vb = vb_ref[c * TK_B:(c + 1) * TK_B, :]
            s = lax.dot_general(qs, k2, (((1,), (1,)), ((), ())), preferred_element_type=F32)
            m_n = jnp.maximum(m, jnp.max(s, axis=-1, keepdims=True))
            p = jnp.exp2(s - m_n).astype(BF16)
            a = jnp.exp2(m - m_n)
            acc_a = a[:tq] * acc_a + jnp.dot(p[:tq], va, preferred_element_type=F32)
            acc_b = a[tq:] * acc_b + jnp.dot(p[tq:], vb, preferred_element_type=F32)
            m = m_n
        oa = acc_a / pltpu.roll(acc_a, HEAD_DIM, 1)
        ob = acc_b / pltpu.roll(acc_b, HEAD_DIM, 1)
        o_ref[0, :, pr * LANES:(pr + 1) * LANES] = jnp.where(low, oa, ob).astype(BF16)


def _attn_b_call(qb, kvb):
    b, seq, _ = qb.shape
    return pl.pallas_call(
        _attn_b_kernel,
        grid=(b, seq // TQ_B),
        in_specs=[
            pl.BlockSpec((1, TQ_B, B_Q_W), lambda i, j: (i, j, 0)),
            pl.BlockSpec((1, seq, B_KV_W), lambda i, j: (i, 0, 0)),
        ],
        out_specs=pl.BlockSpec((1, TQ_B, B_Q_W), lambda i, j: (i, j, 0)),
        out_shape=jax.ShapeDtypeStruct((b, seq, B_Q_W), BF16),
        scratch_shapes=[pltpu.VMEM((seq, LANES), BF16)] * 2,
        compiler_params=_cparams(("arbitrary",) * 2),
        name="attn_b",
    )(qb, kvb)


def _pack_bf16_pairs(v):
    half = v.shape[1] // 2
    lo = lax.bitcast_convert_type(v[:, :half].astype(BF16).astype(F32), U32)
    hi = lax.bitcast_convert_type(v[:, half:].astype(BF16).astype(F32), U32)
    return (lo >> 16) | (hi & jnp.uint32(0xFFFF0000))


def _unpack_bf16_pairs(p):
    lo = lax.bitcast_convert_type(p << 16, F32)
    hi = lax.bitcast_convert_type(p & jnp.uint32(0xFFFF0000), F32)
    return lo, hi


def _merge_kernel(x_ref, ya_ref, yb_ref, g1_ref, wg_ref, wba_ref, wbb_ref, wo_ref, g2_ref,
                  wrh_ref, wrl_ref, bias_ref, wsgu_ref, wsd_ref, tri_ref,
                  base_ref, xp_ref, eidx_ref, rank_ref, wgt_ref, cnt_ref, carry_ref):
    step = pl.program_id(0)

    @pl.when(step == 0)
    def _():
        carry_ref[...] = jnp.zeros_like(carry_ref)

    x = x_ref[...]
    tm = x.shape[0]
    ms = jnp.mean(x * x, axis=-1, keepdims=True)
    xn = (x * lax.rsqrt(ms + EPS) * g1_ref[...]).astype(BF16)
    gates = jax.nn.sigmoid(jnp.dot(xn, wg_ref[...], preferred_element_type=F32))
    pa = jnp.dot(ya_ref[...], wba_ref[...], preferred_element_type=F32)
    pb = jnp.dot(yb_ref[...], wbb_ref[...], preferred_element_type=F32)
    merged = gates[:, :D_MODEL] * pa + gates[:, D_MODEL:] * pb
    h = x + jnp.dot(merged.astype(BF16), wo_ref[...], preferred_element_type=F32)

    ms2 = jnp.mean(h * h, axis=-1, keepdims=True)
    xn2 = h * lax.rsqrt(ms2 + EPS) * g2_ref[...]
    xn2b = xn2.astype(BF16)
    xp_ref[...] = _pack_bf16_pairs(xn2)

    gu = jnp.dot(xn2b, wsgu_ref[...], preferred_element_type=F32)
    hs = (jax.nn.silu(gu[:, :EXPERT_FF]) * gu[:, EXPERT_FF:]).astype(BF16)
    base_ref[...] = h + jnp.dot(hs, wsd_ref[...], preferred_element_type=F32)

    xlo = (xn2 - xn2b.astype(F32)).astype(BF16)
    logits = (jnp.dot(xn2b, wrh_ref[...], preferred_element_type=F32)
              + jnp.dot(xlo, wrh_ref[...], preferred_element_type=F32)
              + jnp.dot(xn2b, wrl_ref[...], preferred_element_type=F32))
    scores = jax.nn.sigmoid(logits.T[:N_EXPERTS, :])
    sel = scores + bias_ref[...]

    gsz = N_EXPERTS // N_EXPERT_GROUPS
    iota8 = lax.broadcasted_iota(I32, (gsz, tm), 0).astype(F32)
    ninf = jnp.float32(-jnp.inf)
    blocks, gs = [], []
    for g in range(N_EXPERT_GROUPS):
        blk = sel[g * gsz:(g + 1) * gsz, :]
        m1 = jnp.max(blk, axis=0, keepdims=True)
        first = jnp.min(jnp.where(blk == m1, iota8, float(gsz)), axis=0, keepdims=True)
        m2 = jnp.max(jnp.where(iota8 == first, ninf, blk), axis=0, keepdims=True)
        blocks.append(blk)
        gs.append(m1 + m2)
    cur = jnp.concatenate(gs, axis=0)
    gmask = jnp.zeros((N_EXPERT_GROUPS, tm), F32)
    for _ in range(TOPK_GROUPS):
        mx = jnp.max(cur, axis=0, keepdims=True)
        fi = jnp.min(jnp.where(cur == mx, iota8, float(N_EXPERT_GROUPS)), axis=0, keepdims=True)
        pick = iota8 == fi
        gmask = jnp.where(pick, 1.0, gmask)
        cur = jnp.where(pick, ninf, cur)
    cur = jnp.concatenate(
        [jnp.where(gmask[g:g + 1, :] > 0.5, blocks[g], ninf) for g in range(N_EXPERT_GROUPS)], axis=0)

    iota64 = lax.broadcasted_iota(I32, (N_EXPERTS, tm), 0).astype(F32)
    idxs, wks = [], []
    onehot = jnp.zeros((N_EXPERTS, tm), F32)
    for _ in range(TOP_K):
        mx = jnp.max(cur, axis=0, keepdims=True)
        fi = jnp.min(jnp.where(cur == mx, iota64, float(N_EXPERTS)), axis=0, keepdims=True)
        pick = iota64 == fi
        wks.append(jnp.sum(jnp.where(pick, scores, 0.0), axis=0, keepdims=True))
        idxs.append(fi)
        onehot = jnp.where(pick, 1.0, onehot)
        cur = jnp.where(pick, ninf, cur)
    wk = jnp.concatenate(wks, axis=0)
    wgt_ref[...] = wk / jnp.sum(wk, axis=0, keepdims=True) * ROUTED_SCALE
    eidx_ref[...] = jnp.concatenate(idxs, axis=0).astype(I32)

    before = jnp.dot(onehot.astype(BF16), tri_ref[...], preferred_element_type=F32) + carry_ref[:, 0:1]
    rank_ref[...] = jnp.concatenate(
        [jnp.sum(jnp.where(iota64 == fi, before, 0.0), axis=0, keepdims=True) for fi in idxs],
        axis=0).astype(I32)
    carry_ref[...] = carry_ref[...] + jnp.sum(onehot, axis=1, keepdims=True)
    cnt_ref[...] = carry_ref[...]


def _merge_call(x2d, ya, yb, g1, wg, wba, wbb, wo, g2, wrh, wrl, bias, wsgu, wsd, tri, row0, n):
    tm = TM_MERGE
    off = row0 // tm
    full = lambda shape: pl.BlockSpec(shape, lambda i: (0,) * len(shape))
    row_in = lambda w: pl.BlockSpec((tm, w), lambda i: (i + off, 0))
    row = lambda w: pl.BlockSpec((tm, w), lambda i: (i, 0))
    col = lambda: pl.BlockSpec((TOP_K, tm), lambda i: (0, i))
    return pl.pallas_call(
        _merge_kernel,
        grid=(n // tm,),
        in_specs=[
            row_in(D_MODEL), row_in(A_OUT_W), row_in(B_Q_W),
            full((1, D_MODEL)), full((D_MODEL, 2 * D_MODEL)),
            full((A_OUT_W, D_MODEL)), full((B_Q_W, D_MODEL)), full((D_MODEL, D_MODEL)),
            full((1, D_MODEL)),
            full((D_MODEL, LANES)), full((D_MODEL, LANES)), full((N_EXPERTS, 1)),
            full((D_MODEL, 2 * EXPERT_FF)), full((EXPERT_FF, D_MODEL)),
            full((tm, tm)),
        ],
        out_specs=[row(D_MODEL), row(D_MODEL // 2), col(), col(), col(), full((N_EXPERTS, LANES))],
        out_shape=[
            jax.ShapeDtypeStruct((n, D_MODEL), F32),
            jax.ShapeDtypeStruct((n, D_MODEL // 2), U32),
            jax.ShapeDtypeStruct((TOP_K, n), I32),
            jax.ShapeDtypeStruct((TOP_K, n), I32),
            jax.ShapeDtypeStruct((TOP_K, n), F32),
            jax.ShapeDtypeStruct((N_EXPERTS, LANES), F32),
        ],
        scratch_shapes=[pltpu.VMEM((N_EXPERTS, LANES), F32)],
        compiler_params=_cparams(("arbitrary",)),
        name="merge_router",
    )(x2d, ya, yb, g1, wg, wba, wbb, wo, g2, wrh, wrl, bias, wsgu, wsd, tri)


EXPERT_SUBBLOCKS = 4


def _expert_kernel(be_ref, nb_ref, xs_ref, *refs):
    w_refs, ys_ref = refs[:-1], refs[-1]
    first = pl.program_id(0) * EXPERT_SUBBLOCKS

    @pl.when(first >= nb_ref[0])
    def _():
        ys_ref[...] = jnp.zeros_like(ys_ref)

    @pl.when(first < nb_ref[0])
    def _():
        half = D_MODEL // 2
        for sub in range(EXPERT_SUBBLOCKS):
            wg_ref, wu_ref, wd_ref = w_refs[3 * sub:3 * sub + 3]
            rows = slice(sub * BM_EXPERT, (sub + 1) * BM_EXPERT)
            lo, hi = _unpack_bf16_pairs(xs_ref[rows, :])
            lo = lo.astype(BF16)
            hi = hi.astype(BF16)
            gate = (jnp.dot(lo, wg_ref[0, :half, :], preferred_element_type=F32)
                    + jnp.dot(hi, wg_ref[0, half:, :], preferred_element_type=F32))
            up = (jnp.dot(lo, wu_ref[0, :half, :], preferred_element_type=F32)
                  + jnp.dot(hi, wu_ref[0, half:, :], preferred_element_type=F32))
            hb = (jax.nn.silu(gate) * up).astype(BF16)
            ys_ref[rows, :] = _pack_bf16_pairs(jnp.dot(hb, wd_ref[0], preferred_element_type=F32))


def _expert_call(block_e, n_used, xs, wg, wu, wd):
    p_len = xs.shape[0]
    nsub = EXPERT_SUBBLOCKS
    rows = nsub * BM_EXPERT
    w_specs, w_args = [], []
    for sub in range(nsub):
        pick = lambda j, be, nb, sub=sub: (be[nsub * j + sub], 0, 0)
        w_specs += [
            pl.BlockSpec((1, D_MODEL, EXPERT_FF), pick),
            pl.BlockSpec((1, D_MODEL, EXPERT_FF), pick),
            pl.BlockSpec((1, EXPERT_FF, D_MODEL), pick),
        ]
        w_args += [wg, wu, wd]
    return pl.pallas_call(
        _expert_kernel,
        grid_spec=pltpu.PrefetchScalarGridSpec(
            num_scalar_prefetch=2,
            grid=(p_len // rows,),
            in_specs=[pl.BlockSpec((rows, D_MODEL // 2), lambda j, be, nb: (j, 0))] + w_specs,
            out_specs=pl.BlockSpec((rows, D_MODEL // 2), lambda j, be, nb: (j, 0)),
        ),
        out_shape=jax.ShapeDtypeStruct((p_len, D_MODEL // 2), U32),
        compiler_params=_cparams(("arbitrary",)),
        name="routed_experts",
    )(block_e, n_used, xs, *w_args)


SC_CORES = 2
SC_SUBCORES = 16
SC_WORKERS = SC_CORES * SC_SUBCORES
SC_CHUNK = 128


def _sc_mesh():
    return plsc.VectorSubcoreMesh(core_axis_name="c", subcore_axis_name="s",
                                  num_cores=SC_CORES, num_subcores=SC_SUBCORES)


def _dispatch_rows(xp, dest3, p_len):
    n, width = xp.shape
    n_chunks = dest3.shape[0]
    per_worker = n_chunks // SC_WORKERS

    @functools.partial(
        pl.kernel, mesh=_sc_mesh(),
        out_type=jax.ShapeDtypeStruct((p_len, width), xp.dtype),
        scratch_types=[pltpu.VMEM((TOP_K, SC_CHUNK), I32), pltpu.VMEM((SC_CHUNK, width), xp.dtype),
                       pltpu.SemaphoreType.DMA],
        name="sc_dispatch")
    def body(xp_hbm, dest_hbm, xs_hbm, idx_v, rows_v, sem):
        wid = lax.axis_index("s") * SC_CORES + lax.axis_index("c")

        @pl.loop(0, per_worker)
        def _(j):
            chunk = wid * per_worker + j
            pltpu.sync_copy(dest_hbm.at[chunk], idx_v)
            pltpu.sync_copy(xp_hbm.at[pl.ds(chunk * SC_CHUNK, SC_CHUNK)], rows_v)
            copies = [pltpu.async_copy(rows_v, xs_hbm.at[idx_v.at[k]], sem) for k in range(TOP_K)]
            for cp in copies:
                cp.wait()

    return body(xp, dest3)


SC_GATHER_CHUNK = 64


def _gather_rows(ys, dest3):
    width = ys.shape[1]
    n_chunks, _, gc = dest3.shape
    n = n_chunks * gc
    per_worker = n_chunks // SC_WORKERS

    @functools.partial(
        pl.kernel, mesh=_sc_mesh(),
        out_type=jax.ShapeDtypeStruct((TOP_K, n, width), ys.dtype),
        scratch_types=[pltpu.VMEM((TOP_K, gc), I32),
                       pltpu.VMEM((gc, width), ys.dtype), pltpu.VMEM((gc, width), ys.dtype),
                       pltpu.SemaphoreType.DMA, pltpu.SemaphoreType.DMA,
                       pltpu.SemaphoreType.DMA, pltpu.SemaphoreType.DMA],
        name="sc_gather")
    def body(ys_hbm, dest_hbm, out_hbm, idx_v, buf0, buf1, gsem0, gsem1, wsem0, wsem1):
        wid = lax.axis_index("s") * SC_CORES + lax.axis_index("c")
        bufs, gsems, wsems = (buf0, buf1), (gsem0, gsem1), (wsem0, wsem1)

        @pl.loop(0, per_worker)
        def _(j):
            chunk = wid * per_worker + j
            pltpu.sync_copy(dest_hbm.at[chunk], idx_v)
            rows = pl.ds(chunk * gc, gc)
            gathers = [None] * TOP_K
            writes = [None] * TOP_K
            gathers[0] = pltpu.async_copy(ys_hbm.at[idx_v.at[0]], bufs[0], gsems[0])
            for k in range(TOP_K):
                gathers[k].wait()
                if k >= 1:
                    writes[k - 1].wait()
                if k + 1 < TOP_K:
                    nxt = (k + 1) % 2
                    gathers[k + 1] = pltpu.async_copy(ys_hbm.at[idx_v.at[k + 1]], bufs[nxt], gsems[nxt])
                writes[k] = pltpu.async_copy(bufs[k % 2], out_hbm.at[k, rows], wsems[k % 2])
            writes[TOP_K - 1].wait()

    return body(ys, dest3)


def _combine_kernel(base_ref, g_ref, w_ref, *refs):
    o_ref = refs[-1]
    half = D_MODEL // 2
    lo_acc = base_ref[:, :half]
    hi_acc = base_ref[:, half:]
    for k in range(TOP_K):
        lo, hi = _unpack_bf16_pairs(g_ref[k])
        wk = w_ref[:, k:k + 1]
        lo_acc = lo_acc + wk * lo
        hi_acc = hi_acc + wk * hi
    o_ref[:, :half] = lo_acc
    o_ref[:, half:] = hi_acc


def _combine_call(base, gathered, w_t, prev, row0, n_total):
    n = base.shape[0]
    tm = TM_COMBINE
    off = row0 // tm
    in_specs = [
        pl.BlockSpec((tm, D_MODEL), lambda i: (i, 0)),
        pl.BlockSpec((TOP_K, tm, D_MODEL // 2), lambda i: (0, i, 0)),
        pl.BlockSpec((tm, TOP_K), lambda i: (i, 0)),
    ]
    args = [base, gathered, w_t]
    aliases = {}
    if prev is not None:
        in_specs.append(pl.BlockSpec(memory_space=pl.ANY))
        args.append(prev)
        aliases = {3: 0}
    return pl.pallas_call(
        _combine_kernel,
        grid=(n // tm,),
        in_specs=in_specs,
        out_specs=pl.BlockSpec((tm, D_MODEL), lambda i: (i + off, 0)),
        out_shape=jax.ShapeDtypeStruct((n_total, D_MODEL), F32),
        input_output_aliases=aliases,
        compiler_params=_cparams(("arbitrary",)),
        name="moe_combine",
    )(*args)


def _fold_gain(tab, gain, half):
    g = jnp.tile(gain.astype(F32), LANES // HEAD_DIM)
    return jnp.stack([tab[0] * g, tab[1] * jnp.roll(g, LANES - half), tab[2] * jnp.roll(g, half)])


def _prep_tables(seq):
    pos = jnp.arange(seq)
    cos1, sin1 = _rope_tables(pos, HEAD_DIM)
    cos_r, sin_r = _rope_tables(pos // GRID_W, HEAD_DIM // 2)
    cos_c, sin_c = _rope_tables(pos % GRID_W, HEAD_DIM // 2)
    taba = _lane_tables(cos1, sin1, HEAD_DIM // 2)
    tabb = _lane_tables(jnp.concatenate([cos_r, cos_c], -1), jnp.concatenate([sin_r, sin_c], -1),
                        HEAD_DIM // 4)
    return taba, tabb


def _layer(h, p, taba, tabb):
    b, seq, d = h.shape
    n = b * seq
    x2d = h.reshape(n, d)
    w_in = p["w_in"]
    o1 = N_GROUPS_A * A_GROUP_W
    o2 = o1 + B_Q_W
    o3 = o2 + B_KV_W
    pair_heads = jnp.array([0, 4, 1, 5, 2, 6, 3, 7])
    pair_cols = (pair_heads[:, None] * HEAD_DIM + jnp.arange(HEAD_DIM)[None, :]).reshape(-1)
    wqkv = jnp.concatenate([w_in[:, :o1], w_in[:, o1:o2][:, pair_cols], w_in[:, o2:o3]], axis=1).astype(BF16)
    tabs = jnp.stack([
        _fold_gain(taba, p["q_norm_a"] * Q_SCALE, HEAD_DIM // 2),
        _fold_gain(taba, p["k_norm_a"], HEAD_DIM // 2),
        _fold_gain(tabb, p["q_norm_b"] * Q_SCALE, HEAD_DIM // 4),
        _fold_gain(tabb, p["k_norm_b"], HEAD_DIM // 4),
    ])
    seg = jnp.arange(2 * LANES) // HEAD_DIM
    bd = (seg[:, None] == seg[None, :]).astype(BF16)
    g1 = p["norm1_g"].reshape(1, d).astype(F32)

    a0, a1, a2, qb, kvb = _proj_call(x2d, g1, wqkv, tabs, bd, seq)
    ya = _attn_a_call(a0.reshape(b, seq, A_GROUP_W), a1.reshape(b, seq // 4, 4 * A_GROUP_W),
                      a2.reshape(b, seq // 16, 16 * A_GROUP_W))
    yb = _attn_b_call(qb.reshape(b, seq, B_Q_W), kvb.reshape(b, seq, B_KV_W))

    wg = w_in[:, o3:].astype(BF16)
    wbb = p["w_branch_b"][pair_cols, :].astype(BF16)
    wr = jnp.pad(p["w_router"], ((0, 0), (0, LANES - N_EXPERTS)))
    wrh = wr.astype(BF16)
    wrl = (wr - wrh.astype(F32)).astype(BF16)
    wsgu = jnp.concatenate([p["ws_gate"], p["ws_up"]], axis=1).astype(BF16)
    tm = TM_MERGE
    tri = (jnp.arange(tm)[:, None] < jnp.arange(tm)[None, :]).astype(BF16)
    wba = p["w_branch_a"].astype(BF16)
    wo = p["w_out"].astype(BF16)
    g2 = p["norm2_g"].reshape(1, d).astype(F32)
    bias = p["router_bias"].reshape(N_EXPERTS, 1).astype(F32)
    wsd = p["ws_down"].astype(BF16)
    weg = p["we_gate"].astype(BF16)
    weu = p["we_up"].astype(BF16)
    wd = p["we_down"].astype(BF16)
    ya2d = ya.reshape(n, A_OUT_W)
    yb2d = yb.reshape(n, B_Q_W)

    nc = n // MOE_CHUNKS
    bm = BM_EXPERT
    n_blocks = (nc * TOP_K + N_EXPERTS * (bm - 1)) // bm + 1
    n_blocks = -(-n_blocks // EXPERT_SUBBLOCKS) * EXPERT_SUBBLOCKS
    p_len = n_blocks * bm
    block_start = jnp.arange(n_blocks, dtype=I32) * bm
    out = None
    for c in range(MOE_CHUNKS):
        base, xp, eidx, rank, wgt, cnt = _merge_call(
            x2d, ya2d, yb2d, g1, wg, wba, wbb, wo, g2, wrh, wrl, bias, wsgu, wsd, tri, c * nc, nc)
        counts = cnt[:, 0].astype(I32)
        padded = (counts + bm - 1) // bm * bm
        pend = jnp.cumsum(padded)
        pstart = pend - padded
        onehot = eidx[:, :, None] == jnp.arange(N_EXPERTS)[None, None, :]
        dest = jnp.sum(jnp.where(onehot, pstart[None, None, :], 0), axis=-1) + rank
        block_e = jnp.minimum(jnp.sum(pend[None, :] <= block_start[:, None], axis=1),
                              N_EXPERTS - 1).astype(I32)
        n_used = (pend[-1] // bm).astype(I32).reshape(1)
        dest3 = dest.reshape(TOP_K, nc // SC_CHUNK, SC_CHUNK).transpose(1, 0, 2)
        dest3g = dest.reshape(TOP_K, nc // SC_GATHER_CHUNK, SC_GATHER_CHUNK).transpose(1, 0, 2)

        xs = _dispatch_rows(xp, dest3, p_len)
        ys = _expert_call(block_e, n_used, xs, weg, weu, wd)
        gathered = _gather_rows(ys, dest3g)
        out = _combine_call(base, gathered, wgt.T, out, c * nc, n)
    return out.reshape(b, seq, d)


def kernel(x, norm1_g, w_in, q_norm_a, k_norm_a, q_norm_b, k_norm_b, w_branch_a, w_branch_b, w_out,
           norm2_g, w_router, router_bias, we_gate, we_up, we_down, ws_gate, ws_up, ws_down):
    params = dict(norm1_g=norm1_g, w_in=w_in, q_norm_a=q_norm_a, k_norm_a=k_norm_a, q_norm_b=q_norm_b,
                  k_norm_b=k_norm_b, w_branch_a=w_branch_a, w_branch_b=w_branch_b, w_out=w_out,
                  norm2_g=norm2_g, w_router=w_router, router_bias=router_bias, we_gate=we_gate,
                  we_up=we_up, we_down=we_down, ws_gate=ws_gate, ws_up=ws_up, ws_down=ws_down)
    taba, tabb = _prep_tables(x.shape[1])
    h = x
    for l in range(norm1_g.shape[0]):
        h = _layer(h, {k: v[l] for k, v in params.items()}, taba, tabb)
    return h
```

```python
import functools

import jax
import jax.numpy as jnp
from jax import lax
from jax.experimental import pallas as pl
from jax.experimental.pallas import tpu as pltpu
from jax.experimental.pallas import tpu_sc as plsc

F32 = jnp.float32
BF16 = jnp.bfloat16
I32 = jnp.int32
U32 = jnp.uint32

D_MODEL = 1024
HEAD_DIM = 64
ROPE_THETA = 10000.0
EPS = 1e-6
GRID_W = 64
DILATIONS = (1, 4, 16)
RADIUS = 64
N_GROUPS_A = 3
A_GROUP_W = 768
A_OUT_W = 256
B_Q_W = 512
B_KV_W = 256
QKV_W = N_GROUPS_A * A_GROUP_W + B_Q_W + B_KV_W
N_EXPERTS = 64
N_EXPERT_GROUPS = 8
TOPK_GROUPS = 4
TOP_K = 8
EXPERT_FF = 256
ROUTED_SCALE = 2.5

LANES = 128
NEG_BIG = -1e30
Q_SCALE = HEAD_DIM ** -0.5 * 1.4426950408889634

TM_PROJ = 512
TQ_B = 256
TK_B = 512
TM_MERGE = 256
BM_EXPERT = 512
TM_COMBINE = 512
MOE_CHUNKS = 2
VMEM_LIMIT = 56 * 1024 * 1024


def _cparams(sem):
    return pltpu.CompilerParams(dimension_semantics=sem, vmem_limit_bytes=VMEM_LIMIT)


def _rope_tables(pos, dim):
    inv = ROPE_THETA ** (-jnp.arange(0, dim, 2, dtype=F32) / dim)
    ang = pos.astype(F32)[:, None] * inv[None, :]
    ang = jnp.concatenate([ang, ang], axis=-1)
    return jnp.cos(ang), jnp.sin(ang)


def _lane_tables(cos_h, sin_h, half):
    cos2 = jnp.concatenate([cos_h, cos_h], axis=-1)
    sin2 = jnp.concatenate([sin_h, sin_h], axis=-1)
    first = (jnp.arange(LANES) % (2 * half)) < half
    s_left = jnp.where(first[None, :], -sin2, 0.0)
    s_right = jnp.where(first[None, :], 0.0, sin2)
    return jnp.stack([cos2, s_left, s_right]).astype(F32)


TAB_AQ, TAB_AK, TAB_BQ, TAB_BK = range(4)


def _proj_kernel(x_ref, g1_ref, w_ref, tab_ref, bd_ref,
                 a0_ref, a1_ref, a2_ref, qb_ref, kvb_ref, pbuf_ref):
    x = x_ref[...]
    ms = jnp.mean(x * x, axis=-1, keepdims=True)
    xn = (x * lax.rsqrt(ms + EPS) * g1_ref[...]).astype(BF16)
    tm = x.shape[0]

    def norm_rope(y, t, shift):
        w = y.shape[1]
        ss = jnp.dot((y * y).astype(BF16), bd_ref[0:w, 0:w], preferred_element_type=F32)
        yn = y * lax.rsqrt(ss * (1.0 / HEAD_DIM) + EPS)
        out = []
        for i in range(w // LANES):
            z = yn[:, i * LANES:(i + 1) * LANES]
            out.append(z * tab_ref[t, 0] + pltpu.roll(z, LANES - shift, 1) * tab_ref[t, 1]
                       + pltpu.roll(z, shift, 1) * tab_ref[t, 2])
        return out

    for g, d in enumerate(DILATIONS):
        base = g * A_GROUP_W
        r = jnp.dot(xn, w_ref[:, base:base + A_GROUP_W], preferred_element_type=F32)
        chunks = (norm_rope(r[:, 0:2 * LANES], TAB_AQ, HEAD_DIM // 2)
                  + norm_rope(r[:, 2 * LANES:4 * LANES], TAB_AK, HEAD_DIM // 2)
                  + [r[:, 4 * LANES:5 * LANES], r[:, 5 * LANES:6 * LANES]])
        for c, y in enumerate(chunks):
            if d == 1:
                a0_ref[:, c * LANES:(c + 1) * LANES] = y.astype(BF16)
            else:
                pbuf_ref[c] = y
        if d > 1:
            out_ref = a1_ref if g == 1 else a2_ref
            rows = tm // d
            for res in range(d):
                for c in range(A_GROUP_W // LANES):
                    col = res * A_GROUP_W + c * LANES
                    out_ref[:, col:col + LANES] = pbuf_ref[c, pl.ds(res, rows, stride=d), :].astype(BF16)

    base = N_GROUPS_A * A_GROUP_W
    r = jnp.dot(xn, w_ref[:, base:base + B_Q_W + B_KV_W], preferred_element_type=F32)
    chunks = (norm_rope(r[:, 0:2 * LANES], TAB_BQ, HEAD_DIM // 4)
              + norm_rope(r[:, 2 * LANES:4 * LANES], TAB_BQ, HEAD_DIM // 4)
              + norm_rope(r[:, 4 * LANES:5 * LANES], TAB_BK, HEAD_DIM // 4)
              + [r[:, 5 * LANES:6 * LANES]])
    for c, y in enumerate(chunks):
        if c < 4:
            qb_ref[:, c * LANES:(c + 1) * LANES] = y.astype(BF16)
        else:
            kvb_ref[:, (c - 4) * LANES:(c - 3) * LANES] = y.astype(BF16)


def _proj_call(x2d, g1, wqkv, tabs, bd, seq):
    n = x2d.shape[0]
    tm = TM_PROJ
    tiles_per_seq = seq // tm
    full = lambda shape: pl.BlockSpec(shape, lambda i: (0,) * len(shape))
    return pl.pallas_call(
        _proj_kernel,
        grid=(n // tm,),
        in_specs=[
            pl.BlockSpec((tm, D_MODEL), lambda i: (i, 0)),
            full((1, D_MODEL)),
            full((D_MODEL, QKV_W)),
            pl.BlockSpec((4, 3, tm, LANES), lambda i: (0, 0, i % tiles_per_seq, 0)),
            full((2 * LANES, 2 * LANES)),
        ],
        out_specs=[
            pl.BlockSpec((tm, A_GROUP_W), lambda i: (i, 0)),
            pl.BlockSpec((tm // 4, 4 * A_GROUP_W), lambda i: (i, 0)),
            pl.BlockSpec((tm // 16, 16 * A_GROUP_W), lambda i: (i, 0)),
            pl.BlockSpec((tm, B_Q_W), lambda i: (i, 0)),
            pl.BlockSpec((tm, B_KV_W), lambda i: (i, 0)),
        ],
        out_shape=[
            jax.ShapeDtypeStruct((n, A_GROUP_W), BF16),
            jax.ShapeDtypeStruct((n // 4, 4 * A_GROUP_W), BF16),
            jax.ShapeDtypeStruct((n // 16, 16 * A_GROUP_W), BF16),
            jax.ShapeDtypeStruct((n, B_Q_W), BF16),
            jax.ShapeDtypeStruct((n, B_KV_W), BF16),
        ],
        scratch_shapes=[pltpu.VMEM((A_GROUP_W // LANES, tm, LANES), F32)],
        compiler_params=_cparams(("arbitrary",)),
        name="proj_qkv",
    )(x2d, g1, wqkv, tabs, bd)


def _attn_a_kernel(a0_ref, a1_ref, a2_ref, ya_ref, acc_ref, m_ref, l_ref):
    tq = 128
    lane = lax.broadcasted_iota(I32, (tq, LANES), 1)
    low = lane < HEAD_DIM

    def tile(g, ref, col0, length, i, rows):
        win = min(2 * tq, length)
        if isinstance(i, int):
            q0 = i * tq
            ws = max(0, min(q0 - RADIUS, length - win))
        else:
            q0 = pl.multiple_of(i * tq, tq)
            ws = pl.multiple_of(jnp.clip(q0 - RADIUS, 0, length - win), RADIUS)
        qrow = lax.broadcasted_iota(I32, (2 * tq, win), 0)
        qpos = q0 + jnp.where(qrow >= tq, qrow - tq, qrow)
        kpos = ws + lax.broadcasted_iota(I32, (2 * tq, win), 1)
        valid = jnp.abs(qpos - kpos) <= RADIUS
        for hp in range(2):
            q2 = ref[0, pl.ds(q0, tq), col0 + hp * LANES:col0 + (hp + 1) * LANES]
            k2 = ref[0, pl.ds(ws, win), col0 + 256 + hp * LANES:col0 + 256 + (hp + 1) * LANES]
            v2 = ref[0, pl.ds(ws, win), col0 + 512 + hp * LANES:col0 + 512 + (hp + 1) * LANES]
            zero = jnp.zeros_like(q2)
            qs = jnp.concatenate([jnp.where(low, q2, zero), jnp.where(low, zero, q2)], axis=0)
            s = lax.dot_general(qs, k2, (((1,), (1,)), ((), ())), preferred_element_type=F32)
            s = jnp.where(valid, s, NEG_BIG)
            m = jnp.max(s, axis=-1, keepdims=True)
            p = jnp.exp2(s - m)
            l = jnp.sum(p, axis=-1, keepdims=True)
            pv = jnp.dot(p.astype(BF16), v2, preferred_element_type=F32)
            slot = 2 * g + hp
            acc_ref[slot, rows, :] = jnp.where(low, pv[:tq], pv[tq:])
            m_ref[slot, rows, :] = jnp.where(low, m[:tq], m[tq:])
            l_ref[slot, rows, :] = jnp.where(low, l[:tq], l[tq:])

    seq = a0_ref.shape[1]

    def g0_body(i, carry):
        tile(0, a0_ref, 0, seq, i, pl.ds(pl.multiple_of(i * tq, tq), tq))
        return carry
    lax.fori_loop(0, seq // tq, g0_body, 0, unroll=4)

    len1 = seq // 4
    for res in range(4):
        def g1_body(i, carry, res=res):
            tile(1, a1_ref, res * A_GROUP_W, len1, i, pl.ds(res + 4 * i * tq, tq, stride=4))
            return carry
        lax.fori_loop(0, len1 // tq, g1_body, 0, unroll=4)

    len2 = seq // 16
    for res in range(16):
        tile(2, a2_ref, res * A_GROUP_W, len2, 0, pl.ds(res, tq, stride=16))

    rc = 256

    def merge_body(j, carry):
        rows = pl.ds(pl.multiple_of(j * rc, rc), rc)
        for hp in range(2):
            ms = [m_ref[2 * g + hp, rows, :] for g in range(N_GROUPS_A)]
            mx = jnp.maximum(jnp.maximum(ms[0], ms[1]), ms[2])
            num = jnp.zeros((rc, LANES), F32)
            den = jnp.zeros((rc, LANES), F32)
            for g in range(N_GROUPS_A):
                e = jnp.exp2(ms[g] - mx)
                num = num + e * acc_ref[2 * g + hp, rows, :]
                den = den + e * l_ref[2 * g + hp, rows, :]
            ya_ref[0, rows, hp * LANES:(hp + 1) * LANES] = (num / den).astype(BF16)
        return carry
    lax.fori_loop(0, seq // rc, merge_body, 0)


def _attn_a_call(a0, a1, a2):
    b, seq, _ = a0.shape
    return pl.pallas_call(
        _attn_a_kernel,
        grid=(b,),
        in_specs=[
            pl.BlockSpec((1, seq, A_GROUP_W), lambda i: (i, 0, 0)),
            pl.BlockSpec((1, seq // 4, 4 * A_GROUP_W), lambda i: (i, 0, 0)),
            pl.BlockSpec((1, seq // 16, 16 * A_GROUP_W), lambda i: (i, 0, 0)),
        ],
        out_specs=pl.BlockSpec((1, seq, A_OUT_W), lambda i: (i, 0, 0)),
        out_shape=jax.ShapeDtypeStruct((b, seq, A_OUT_W), BF16),
        scratch_shapes=[pltpu.VMEM((N_GROUPS_A * A_OUT_W // LANES, seq, LANES), F32)] * 3,
        compiler_params=_cparams(("arbitrary",)),
        name="attn_a",
    )(a0, a1, a2)


def _attn_b_kernel(q_ref, kv_ref, o_ref, va_ref, vb_ref):
    tq = q_ref.shape[1]
    seq = kv_ref.shape[1]
    low = lax.broadcasted_iota(I32, (tq, LANES), 1) < HEAD_DIM

    @pl.when(pl.program_id(1) == 0)
    def _():
        low_k = lax.broadcasted_iota(I32, (seq, LANES), 1) < HEAD_DIM
        v2 = kv_ref[0, :, LANES:2 * LANES]
        one = jnp.ones_like(v2)
        va_ref[...] = jnp.where(low_k, v2, one)
        vb_ref[...] = jnp.where(low_k, one, v2)

    for pr in range(B_Q_W // LANES):
        q2 = q_ref[0, :, pr * LANES:(pr + 1) * LANES]
        zero = jnp.zeros_like(q2)
        qs = jnp.concatenate([jnp.where(low, q2, zero), jnp.where(low, zero, q2)], axis=0)
        m = jnp.full((2 * tq, 1), NEG_BIG, F32)
        acc_a = jnp.zeros((tq, LANES), F32)
        acc_b = jnp.zeros((tq, LANES), F32)
        for c in range(seq // TK_B):
            k2 = kv_ref[0, c * TK_B:(c + 1) * TK_B, 0:LANES]
            va = va_ref[c * TK_B:(c + 1) * TK_B, :]
            vb = vb_ref[c * TK_B:(c + 1) * TK_B, :]
            s = lax.dot_general(qs, k2, (((1,), (1,)), ((), ())), preferred_element_type=F32)
            m_n = jnp.maximum(m, jnp.max(s, axis=-1, keepdims=True))
            p = jnp.exp2(s - m_n).astype(BF16)
            a = jnp.exp2(m - m_n)
            acc_a = a[:tq] * acc_a + jnp.dot(p[:tq], va, preferred_element_type=F32)
            acc_b = a[tq:] * acc_b + jnp.dot(p[tq:], vb, preferred_element_type=F32)
            m = m_n
        oa = acc_a / pltpu.roll(acc_a, HEAD_DIM, 1)
        ob = acc_b / pltpu.roll(acc_b, HEAD_DIM, 1)
        o_ref[0, :, pr * LANES:(pr + 1) * LANES] = jnp.where(low, oa, ob).astype(BF16)


def _attn_b_call(qb, kvb):
    b, seq, _ = qb.shape
    return pl.pallas_call(
        _attn_b_kernel,
        grid=(b, seq // TQ_B),
        in_specs=[
            pl.BlockSpec((1, TQ_B, B_Q_W), lambda i, j: (i, j, 0)),
            pl.BlockSpec((1, seq, B_KV_W), lambda i, j: (i, 0, 0)),
        ],
        out_specs=pl.BlockSpec((1, TQ_B, B_Q_W), lambda i, j: (i, j, 0)),
        out_shape=jax.ShapeDtypeStruct((b, seq, B_Q_W), BF16),
        scratch_shapes=[pltpu.VMEM((seq, LANES), BF16)] * 2,
        compiler_params=_cparams(("arbitrary",) * 2),
        name="attn_b",
    )(qb, kvb)


def _pack_bf16_pairs(v):
    half = v.shape[1] // 2
    lo = lax.bitcast_convert_type(v[:, :half].astype(BF16).astype(F32), U32)
    hi = lax.bitcast_convert_type(v[:, half:].astype(BF16).astype(F32), U32)
    return (lo >> 16) | (hi & jnp.uint32(0xFFFF0000))


def _unpack_bf16_pairs(p):
    lo = lax.bitcast_convert_type(p << 16, F32)
    hi = lax.bitcast_convert_type(p & jnp.uint32(0xFFFF0000), F32)
    return lo, hi


def _merge_kernel(x_ref, ya_ref, yb_ref, g1_ref, wg_ref, wba_ref, wbb_ref, wo_ref, g2_ref,
                  wrh_ref, wrl_ref, bias_ref, wsgu_ref, wsd_ref, tri_ref,
                  base_ref, xp_ref, eidx_ref, rank_ref, wgt_ref, cnt_ref, carry_ref):
    step = pl.program_id(0)

    @pl.when(step == 0)
    def _():
        carry_ref[...] = jnp.zeros_like(carry_ref)

    x = x_ref[...]
    tm = x.shape[0]
    ms = jnp.mean(x * x, axis=-1, keepdims=True)
    xn = (x * lax.rsqrt(ms + EPS) * g1_ref[...]).astype(BF16)
    gates = jax.nn.sigmoid(jnp.dot(xn, wg_ref[...], preferred_element_type=F32))
    pa = jnp.dot(ya_ref[...], wba_ref[...], preferred_element_type=F32)
    pb = jnp.dot(yb_ref[...], wbb_ref[...], preferred_element_type=F32)
    merged = gates[:, :D_MODEL] * pa + gates[:, D_MODEL:] * pb
    h = x + jnp.dot(merged.astype(BF16), wo_ref[...], preferred_element_type=F32)

    ms2 = jnp.mean(h * h, axis=-1, keepdims=True)
    xn2 = h * lax.rsqrt(ms2 + EPS) * g2_ref[...]
    xn2b = xn2.astype(BF16)
    xp_ref[...] = _pack_bf16_pairs(xn2)

    gu = jnp.dot(xn2b, wsgu_ref[...], preferred_element_type=F32)
    hs = (jax.nn.silu(gu[:, :EXPERT_FF]) * gu[:, EXPERT_FF:]).astype(BF16)
    base_ref[...] = h + jnp.dot(hs, wsd_ref[...], preferred_element_type=F32)

    xlo = (xn2 - xn2b.astype(F32)).astype(BF16)
    logits = (jnp.dot(xn2b, wrh_ref[...], preferred_element_type=F32)
              + jnp.dot(xlo, wrh_ref[...], preferred_element_type=F32)
              + jnp.dot(xn2b, wrl_ref[...], preferred_element_type=F32))
    scores = jax.nn.sigmoid(logits.T[:N_EXPERTS, :])
    sel = scores + bias_ref[...]

    gsz = N_EXPERTS // N_EXPERT_GROUPS
    iota8 = lax.broadcasted_iota(I32, (gsz, tm), 0).astype(F32)
    ninf = jnp.float32(-jnp.inf)
    blocks, gs = [], []
    for g in range(N_EXPERT_GROUPS):
        blk = sel[g * gsz:(g + 1) * gsz, :]
        m1 = jnp.max(blk, axis=0, keepdims=True)
        first = jnp.min(jnp.where(blk == m1, iota8, float(gsz)), axis=0, keepdims=True)
        m2 = jnp.max(jnp.where(iota8 == first, ninf, blk), axis=0, keepdims=True)
        blocks.append(blk)
        gs.append(m1 + m2)
    cur = jnp.concatenate(gs, axis=0)
    gmask = jnp.zeros((N_EXPERT_GROUPS, tm), F32)
    for _ in range(TOPK_GROUPS):
        mx = jnp.max(cur, axis=0, keepdims=True)
        fi = jnp.min(jnp.where(cur == mx, iota8, float(N_EXPERT_GROUPS)), axis=0, keepdims=True)
        pick = iota8 == fi
        gmask = jnp.where(pick, 1.0, gmask)
        cur = jnp.where(pick, ninf, cur)
    cur = jnp.concatenate(
        [jnp.where(gmask[g:g + 1, :] > 0.5, blocks[g], ninf) for g in range(N_EXPERT_GROUPS)], axis=0)

    iota64 = lax.broadcasted_iota(I32, (N_EXPERTS, tm), 0).astype(F32)
    idxs, wks = [], []
    onehot = jnp.zeros((N_EXPERTS, tm), F32)
    for _ in range(TOP_K):
        mx = jnp.max(cur, axis=0, keepdims=True)
        fi = jnp.min(jnp.where(cur == mx, iota64, float(N_EXPERTS)), axis=0, keepdims=True)
        pick = iota64 == fi
        wks.append(jnp.sum(jnp.where(pick, scores, 0.0), axis=0, keepdims=True))
        idxs.append(fi)
        onehot = jnp.where(pick, 1.0, onehot)
        cur = jnp.where(pick, ninf, cur)
    wk = jnp.concatenate(wks, axis=0)
    wgt_ref[...] = wk / jnp.sum(wk, axis=0, keepdims=True) * ROUTED_SCALE
    eidx_ref[...] = jnp.concatenate(idxs, axis=0).astype(I32)

    before = jnp.dot(onehot.astype(BF16), tri_ref[...], preferred_element_type=F32) + carry_ref[:, 0:1]
    rank_ref[...] = jnp.concatenate(
        [jnp.sum(jnp.where(iota64 == fi, before, 0.0), axis=0, keepdims=True) for fi in idxs],
        axis=0).astype(I32)
    carry_ref[...] = carry_ref[...] + jnp.sum(onehot, axis=1, keepdims=True)
    cnt_ref[...] = carry_ref[...]


def _merge_call(x2d, ya, yb, g1, wg, wba, wbb, wo, g2, wrh, wrl, bias, wsgu, wsd, tri, row0, n):
    tm = TM_MERGE
    off = row0 // tm
    full = lambda shape: pl.BlockSpec(shape, lambda i: (0,) * len(shape))
    row_in = lambda w: pl.BlockSpec((tm, w), lambda i: (i + off, 0))
    row = lambda w: pl.BlockSpec((tm, w), lambda i: (i, 0))
    col = lambda: pl.BlockSpec((TOP_K, tm), lambda i: (0, i))
    return pl.pallas_call(
        _merge_kernel,
        grid=(n // tm,),
        in_specs=[
            row_in(D_MODEL), row_in(A_OUT_W), row_in(B_Q_W),
            full((1, D_MODEL)), full((D_MODEL, 2 * D_MODEL)),
            full((A_OUT_W, D_MODEL)), full((B_Q_W, D_MODEL)), full((D_MODEL, D_MODEL)),
            full((1, D_MODEL)),
            full((D_MODEL, LANES)), full((D_MODEL, LANES)), full((N_EXPERTS, 1)),
            full((D_MODEL, 2 * EXPERT_FF)), full((EXPERT_FF, D_MODEL)),
            full((tm, tm)),
        ],
        out_specs=[row(D_MODEL), row(D_MODEL // 2), col(), col(), col(), full((N_EXPERTS, LANES))],
        out_shape=[
            jax.ShapeDtypeStruct((n, D_MODEL), F32),
            jax.ShapeDtypeStruct((n, D_MODEL // 2), U32),
            jax.ShapeDtypeStruct((TOP_K, n), I32),
            jax.ShapeDtypeStruct((TOP_K, n), I32),
            jax.ShapeDtypeStruct((TOP_K, n), F32),
            jax.ShapeDtypeStruct((N_EXPERTS, LANES), F32),
        ],
        scratch_shapes=[pltpu.VMEM((N_EXPERTS, LANES), F32)],
        compiler_params=_cparams(("arbitrary",)),
        name="merge_router",
    )(x2d, ya, yb, g1, wg, wba, wbb, wo, g2, wrh, wrl, bias, wsgu, wsd, tri)


EXPERT_SUBBLOCKS = 4


X_SLOTS = 3
Y_SLOTS = 2


def _expert_kernel(be_ref, nb_ref, xs_hbm, *refs):
    nw = 3 * EXPERT_SUBBLOCKS
    w_refs = refs[:nw]
    ys_hbm, xbuf, ybuf, xsem, ysem = refs[nw:]
    j = pl.program_id(0)
    nsteps = pl.num_programs(0)
    rows = EXPERT_SUBBLOCKS * BM_EXPERT

    def x_copy(step, slot):
        src = xs_hbm.at[pl.ds(pl.multiple_of(step * rows, rows), rows)]
        return pltpu.make_async_copy(src, xbuf.at[slot], xsem.at[slot])

    def y_copy(step, slot):
        dst = ys_hbm.at[pl.ds(pl.multiple_of(step * rows, rows), rows)]
        return pltpu.make_async_copy(ybuf.at[slot], dst, ysem.at[slot])

    @pl.when(j == 0)
    def _():
        x_copy(0, 0).start()
        x_copy(1, 1).start()

    @pl.when(j + 2 < nsteps)
    def _():
        x_copy(j + 2, lax.rem(j + 2, X_SLOTS)).start()

    xslot = lax.rem(j, X_SLOTS)
    yslot = lax.rem(j, Y_SLOTS)
    x_copy(j, xslot).wait()

    @pl.when(j >= Y_SLOTS)
    def _():
        y_copy(j - Y_SLOTS, yslot).wait()

    first = j * EXPERT_SUBBLOCKS

    @pl.when(first >= nb_ref[0])
    def _():
        ybuf[yslot] = jnp.zeros((rows, D_MODEL // 2), U32)

    @pl.when(first < nb_ref[0])
    def _():
        half = D_MODEL // 2
        for sub in range(EXPERT_SUBBLOCKS):
            wg_ref, wu_ref, wd_ref = w_refs[3 * sub:3 * sub + 3]
            blk = slice(sub * BM_EXPERT, (sub + 1) * BM_EXPERT)
            lo, hi = _unpack_bf16_pairs(xbuf[xslot, blk, :])
            lo = lo.astype(BF16)
            hi = hi.astype(BF16)
            gate = (jnp.dot(lo, wg_ref[0, :half, :], preferred_element_type=F32)
                    + jnp.dot(hi, wg_ref[0, half:, :], preferred_element_type=F32))
            up = (jnp.dot(lo, wu_ref[0, :half, :], preferred_element_type=F32)
                  + jnp.dot(hi, wu_ref[0, half:, :], preferred_element_type=F32))
            hb = (jax.nn.silu(gate) * up).astype(BF16)
            ybuf[yslot, blk, :] = _pack_bf16_pairs(jnp.dot(hb, wd_ref[0], preferred_element_type=F32))

    y_copy(j, yslot).start()

    @pl.when(j == nsteps - 1)
    def _():
        y_copy(j - 1, lax.rem(j - 1, Y_SLOTS)).wait()
        y_copy(j, yslot).wait()


def _expert_call(block_e, n_used, xs, wg, wu, wd):
    p_len = xs.shape[0]
    nsub = EXPERT_SUBBLOCKS
    rows = nsub * BM_EXPERT
    w_specs, w_args = [], []
    for sub in range(nsub):
        pick = lambda j, be, nb, sub=sub: (be[nsub * j + sub], 0, 0)
        w_specs += [
            pl.BlockSpec((1, D_MODEL, EXPERT_FF), pick),
            pl.BlockSpec((1, D_MODEL, EXPERT_FF), pick),
            pl.BlockSpec((1, EXPERT_FF, D_MODEL), pick),
        ]
        w_args += [wg, wu, wd]
    assert p_len // rows >= max(X_SLOTS, Y_SLOTS)
    return pl.pallas_call(
        _expert_kernel,
        grid_spec=pltpu.PrefetchScalarGridSpec(
            num_scalar_prefetch=2,
            grid=(p_len // rows,),
            in_specs=[pl.BlockSpec(memory_space=pl.ANY)] + w_specs,
            out_specs=pl.BlockSpec(memory_space=pl.ANY),
            scratch_shapes=[
                pltpu.VMEM((X_SLOTS, rows, D_MODEL // 2), U32),
                pltpu.VMEM((Y_SLOTS, rows, D_MODEL // 2), U32),
                pltpu.SemaphoreType.DMA((X_SLOTS,)),
                pltpu.SemaphoreType.DMA((Y_SLOTS,)),
            ],
        ),
        out_shape=jax.ShapeDtypeStruct((p_len, D_MODEL // 2), U32),
        compiler_params=_cparams(("arbitrary",)),
        name="routed_experts",
    )(block_e, n_used, xs, *w_args)


SC_CORES = 2
SC_SUBCORES = 16
SC_WORKERS = SC_CORES * SC_SUBCORES
SC_CHUNK = 128


def _sc_mesh():
    return plsc.VectorSubcoreMesh(core_axis_name="c", subcore_axis_name="s",
                                  num_cores=SC_CORES, num_subcores=SC_SUBCORES)


def _dispatch_rows(xp, dest3, p_len):
    n, width = xp.shape
    n_chunks = dest3.shape[0]
    per_worker = n_chunks // SC_WORKERS

    @functools.partial(
        pl.kernel, mesh=_sc_mesh(),
        out_type=jax.ShapeDtypeStruct((p_len, width), xp.dtype),
        scratch_types=[pltpu.VMEM((TOP_K, SC_CHUNK), I32), pltpu.VMEM((SC_CHUNK, width), xp.dtype),
                       pltpu.SemaphoreType.DMA],
        name="sc_dispatch")
    def body(xp_hbm, dest_hbm, xs_hbm, idx_v, rows_v, sem):
        wid = lax.axis_index("s") * SC_CORES + lax.axis_index("c")

        @pl.loop(0, per_worker)
        def _(j):
            chunk = wid * per_worker + j
            pltpu.sync_copy(dest_hbm.at[chunk], idx_v)
            pltpu.sync_copy(xp_hbm.at[pl.ds(chunk * SC_CHUNK, SC_CHUNK)], rows_v)
            copies = [pltpu.async_copy(rows_v, xs_hbm.at[idx_v.at[k]], sem) for k in range(TOP_K)]
            for cp in copies:
                cp.wait()

    return body(xp, dest3)


SC_GATHER_CHUNK = 64


def _gather_rows(ys, dest3):
    width = ys.shape[1]
    n_chunks, _, gc = dest3.shape
    n = n_chunks * gc
    per_worker = n_chunks // SC_WORKERS

    @functools.partial(
        pl.kernel, mesh=_sc_mesh(),
        out_type=jax.ShapeDtypeStruct((TOP_K, n, width), ys.dtype),
        scratch_types=[pltpu.VMEM((TOP_K, gc), I32),
                       pltpu.VMEM((gc, width), ys.dtype), pltpu.VMEM((gc, width), ys.dtype),
                       pltpu.SemaphoreType.DMA, pltpu.SemaphoreType.DMA,
                       pltpu.SemaphoreType.DMA, pltpu.SemaphoreType.DMA],
        name="sc_gather")
    def body(ys_hbm, dest_hbm, out_hbm, idx_v, buf0, buf1, gsem0, gsem1, wsem0, wsem1):
        wid = lax.axis_index("s") * SC_CORES + lax.axis_index("c")
        bufs, gsems, wsems = (buf0, buf1), (gsem0, gsem1), (wsem0, wsem1)

        @pl.loop(0, per_worker)
        def _(j):
            chunk = wid * per_worker + j
            pltpu.sync_copy(dest_hbm.at[chunk], idx_v)
            rows = pl.ds(chunk * gc, gc)
            gathers = [None] * TOP_K
            writes = [None] * TOP_K
            gathers[0] = pltpu.async_copy(ys_hbm.at[idx_v.at[0]], bufs[0], gsems[0])
            for k in range(TOP_K):
                gathers[k].wait()
                if k >= 1:
                    writes[k - 1].wait()
                if k + 1 < TOP_K:
                    nxt = (k + 1) % 2
                    gathers[k + 1] = pltpu.async_copy(ys_hbm.at[idx_v.at[k + 1]], bufs[nxt], gsems[nxt])
                writes[k] = pltpu.async_copy(bufs[k % 2], out_hbm.at[k, rows], wsems[k % 2])
            writes[TOP_K - 1].wait()

    return body(ys, dest3)


def _combine_kernel(base_ref, g_ref, w_ref, *refs):
    o_ref = refs[-1]
    half = D_MODEL // 2
    lo_acc = base_ref[:, :half]
    hi_acc = base_ref[:, half:]
    for k in range(TOP_K):
        lo, hi = _unpack_bf16_pairs(g_ref[k])
        wk = w_ref[:, k:k + 1]
        lo_acc = lo_acc + wk * lo
        hi_acc = hi_acc + wk * hi
    o_ref[:, :half] = lo_acc
    o_ref[:, half:] = hi_acc


def _combine_call(base, gathered, w_t, prev, row0, n_total):
    n = base.shape[0]
    tm = TM_COMBINE
    off = row0 // tm
    in_specs = [
        pl.BlockSpec((tm, D_MODEL), lambda i: (i, 0)),
        pl.BlockSpec((TOP_K, tm, D_MODEL // 2), lambda i: (0, i, 0)),
        pl.BlockSpec((tm, TOP_K), lambda i: (i, 0)),
    ]
    args = [base, gathered, w_t]
    aliases = {}
    if prev is not None:
        in_specs.append(pl.BlockSpec(memory_space=pl.ANY))
        args.append(prev)
        aliases = {3: 0}
    return pl.pallas_call(
        _combine_kernel,
        grid=(n // tm,),
        in_specs=in_specs,
        out_specs=pl.BlockSpec((tm, D_MODEL), lambda i: (i + off, 0)),
        out_shape=jax.ShapeDtypeStruct((n_total, D_MODEL), F32),
        input_output_aliases=aliases,
        compiler_params=_cparams(("arbitrary",)),
        name="moe_combine",
    )(*args)


def _fold_gain(tab, gain, half):
    g = jnp.tile(gain.astype(F32), LANES // HEAD_DIM)
    return jnp.stack([tab[0] * g, tab[1] * jnp.roll(g, LANES - half), tab[2] * jnp.roll(g, half)])


def _prep_tables(seq):
    pos = jnp.arange(seq)
    cos1, sin1 = _rope_tables(pos, HEAD_DIM)
    cos_r, sin_r = _rope_tables(pos // GRID_W, HEAD_DIM // 2)
    cos_c, sin_c = _rope_tables(pos % GRID_W, HEAD_DIM // 2)
    taba = _lane_tables(cos1, sin1, HEAD_DIM // 2)
    tabb = _lane_tables(jnp.concatenate([cos_r, cos_c], -1), jnp.concatenate([sin_r, sin_c], -1),
                        HEAD_DIM // 4)
    return taba, tabb


def _layer(h, p, taba, tabb):
    b, seq, d = h.shape
    n = b * seq
    x2d = h.reshape(n, d)
    w_in = p["w_in"]
    o1 = N_GROUPS_A * A_GROUP_W
    o2 = o1 + B_Q_W
    o3 = o2 + B_KV_W
    pair_heads = jnp.array([0, 4, 1, 5, 2, 6, 3, 7])
    pair_cols = (pair_heads[:, None] * HEAD_DIM + jnp.arange(HEAD_DIM)[None, :]).reshape(-1)
    wqkv = jnp.concatenate([w_in[:, :o1], w_in[:, o1:o2][:, pair_cols], w_in[:, o2:o3]], axis=1).astype(BF16)
    tabs = jnp.stack([
        _fold_gain(taba, p["q_norm_a"] * Q_SCALE, HEAD_DIM // 2),
        _fold_gain(taba, p["k_norm_a"], HEAD_DIM // 2),
        _fold_gain(tabb, p["q_norm_b"] * Q_SCALE, HEAD_DIM // 4),
        _fold_gain(tabb, p["k_norm_b"], HEAD_DIM // 4),
    ])
    seg = jnp.arange(2 * LANES) // HEAD_DIM
    bd = (seg[:, None] == seg[None, :]).astype(BF16)
    g1 = p["norm1_g"].reshape(1, d).astype(F32)

    a0, a1, a2, qb, kvb = _proj_call(x2d, g1, wqkv, tabs, bd, seq)
    ya = _attn_a_call(a0.reshape(b, seq, A_GROUP_W), a1.reshape(b, seq // 4, 4 * A_GROUP_W),
                      a2.reshape(b, seq // 16, 16 * A_GROUP_W))
    yb = _attn_b_call(qb.reshape(b, seq, B_Q_W), kvb.reshape(b, seq, B_KV_W))

    wg = w_in[:, o3:].astype(BF16)
    wbb = p["w_branch_b"][pair_cols, :].astype(BF16)
    wr = jnp.pad(p["w_router"], ((0, 0), (0, LANES - N_EXPERTS)))
    wrh = wr.astype(BF16)
    wrl = (wr - wrh.astype(F32)).astype(BF16)
    wsgu = jnp.concatenate([p["ws_gate"], p["ws_up"]], axis=1).astype(BF16)
    tm = TM_MERGE
    tri = (jnp.arange(tm)[:, None] < jnp.arange(tm)[None, :]).astype(BF16)
    wba = p["w_branch_a"].astype(BF16)
    wo = p["w_out"].astype(BF16)
    g2 = p["norm2_g"].reshape(1, d).astype(F32)
    bias = p["router_bias"].reshape(N_EXPERTS, 1).astype(F32)
    wsd = p["ws_down"].astype(BF16)
    weg = p["we_gate"].astype(BF16)
    weu = p["we_up"].astype(BF16)
    wd = p["we_down"].astype(BF16)
    ya2d = ya.reshape(n, A_OUT_W)
    yb2d = yb.reshape(n, B_Q_W)

    nc = n // MOE_CHUNKS
    bm = BM_EXPERT
    n_blocks = (nc * TOP_K + N_EXPERTS * (bm - 1)) // bm + 1
    n_blocks = -(-n_blocks // EXPERT_SUBBLOCKS) * EXPERT_SUBBLOCKS
    p_len = n_blocks * bm
    block_start = jnp.arange(n_blocks, dtype=I32) * bm
    out = None
    for c in range(MOE_CHUNKS):
        base, xp, eidx, rank, wgt, cnt = _merge_call(
            x2d, ya2d, yb2d, g1, wg, wba, wbb, wo, g2, wrh, wrl, bias, wsgu, wsd, tri, c * nc, nc)
        counts = cnt[:, 0].astype(I32)
        padded = (counts + bm - 1) // bm * bm
        pend = jnp.cumsum(padded)
        pstart = pend - padded
        onehot = eidx[:, :, None] == jnp.arange(N_EXPERTS)[None, None, :]
        dest = jnp.sum(jnp.where(onehot, pstart[None, None, :], 0), axis=-1) + rank
        block_e = jnp.minimum(jnp.sum(pend[None, :] <= block_start[:, None], axis=1),
                              N_EXPERTS - 1).astype(I32)
        n_used = (pend[-1] // bm).astype(I32).reshape(1)
        dest3 = dest.reshape(TOP_K, nc // SC_CHUNK, SC_CHUNK).transpose(1, 0, 2)
        dest3g = dest.reshape(TOP_K, nc // SC_GATHER_CHUNK, SC_GATHER_CHUNK).transpose(1, 0, 2)

        xs = _dispatch_rows(xp, dest3, p_len)
        ys = _expert_call(block_e, n_used, xs, weg, weu, wd)
        gathered = _gather_rows(ys, dest3g)
        out = _combine_call(base, gathered, wgt.T, out, c * nc, n)
    return out.reshape(b, seq, d)


def kernel(x, norm1_g, w_in, q_norm_a, k_norm_a, q_norm_b, k_norm_b, w_branch_a, w_branch_b, w_out,
           norm2_g, w_router, router_bias, we_gate, we_up, we_down, ws_gate, ws_up, ws_down):
    params = dict(norm1_g=norm1_g, w_in=w_in, q_norm_a=q_norm_a, k_norm_a=k_norm_a, q_norm_b=q_norm_b,
                  k_norm_b=k_norm_b, w_branch_a=w_branch_a, w_branch_b=w_branch_b, w_out=w_out,
                  norm2_g=norm2_g, w_router=w_router, router_bias=router_bias, we_gate=we_gate,
                  we_up=we_up, we_down=we_down, ws_gate=ws_gate, ws_up=ws_up, ws_down=ws_down)
    taba, tabb = _prep_tables(x.shape[1])
    h = x
    for l in range(norm1_g.shape[0]):
        h = _layer(h, {k: v[l] for k, v in params.items()}, taba, tabb)
    return h
```

```python
import functools

import jax
import jax.numpy as jnp
from jax import lax
from jax.experimental import pallas as pl
from jax.experimental.pallas import tpu as pltpu
from jax.experimental.pallas import tpu_sc as plsc

F32 = jnp.float32
BF16 = jnp.bfloat16
I32 = jnp.int32
U32 = jnp.uint32

D_MODEL = 1024
HEAD_DIM = 64
ROPE_THETA = 10000.0
EPS = 1e-6
GRID_W = 64
DILATIONS = (1, 4, 16)
RADIUS = 64
N_GROUPS_A = 3
A_GROUP_W = 768
A_OUT_W = 256
B_Q_W = 512
B_KV_W = 256
QKV_W = N_GROUPS_A * A_GROUP_W + B_Q_W + B_KV_W
N_EXPERTS = 64
N_EXPERT_GROUPS = 8
TOPK_GROUPS = 4
TOP_K = 8
EXPERT_FF = 256
ROUTED_SCALE = 2.5

LANES = 128
NEG_BIG = -1e30
Q_SCALE = HEAD_DIM ** -0.5 * 1.4426950408889634

TM_PROJ = 512
TQ_B = 256
TK_B = 512
TM_MERGE = 256
BM_EXPERT = 512
TM_COMBINE = 512
MOE_CHUNKS = 4
VMEM_LIMIT = 56 * 1024 * 1024


def _cparams(sem):
    return pltpu.CompilerParams(dimension_semantics=sem, vmem_limit_bytes=VMEM_LIMIT)


def _rope_tables(pos, dim):
    inv = ROPE_THETA ** (-jnp.arange(0, dim, 2, dtype=F32) / dim)
    ang = pos.astype(F32)[:, None] * inv[None, :]
    ang = jnp.concatenate([ang, ang], axis=-1)
    return jnp.cos(ang), jnp.sin(ang)


def _lane_tables(cos_h, sin_h, half):
    cos2 = jnp.concatenate([cos_h, cos_h], axis=-1)
    sin2 = jnp.concatenate([sin_h, sin_h], axis=-1)
    first = (jnp.arange(LANES) % (2 * half)) < half
    s_left = jnp.where(first[None, :], -sin2, 0.0)
    s_right = jnp.where(first[None, :], 0.0, sin2)
    return jnp.stack([cos2, s_left, s_right]).astype(F32)


TAB_AQ, TAB_AK, TAB_BQ, TAB_BK = range(4)


def _proj_kernel(x_ref, g1_ref, w_ref, tab_ref, bd_ref,
                 a0_ref, a1_ref, a2_ref, qb_ref, kvb_ref, pbuf_ref):
    x = x_ref[...]
    ms = jnp.mean(x * x, axis=-1, keepdims=True)
    xn = (x * lax.rsqrt(ms + EPS) * g1_ref[...]).astype(BF16)
    tm = x.shape[0]

    def norm_rope(y, t, shift):
        w = y.shape[1]
        ss = jnp.dot((y * y).astype(BF16), bd_ref[0:w, 0:w], preferred_element_type=F32)
        yn = y * lax.rsqrt(ss * (1.0 / HEAD_DIM) + EPS)
        out = []
        for i in range(w // LANES):
            z = yn[:, i * LANES:(i + 1) * LANES]
            out.append(z * tab_ref[t, 0] + pltpu.roll(z, LANES - shift, 1) * tab_ref[t, 1]
                       + pltpu.roll(z, shift, 1) * tab_ref[t, 2])
        return out

    for g, d in enumerate(DILATIONS):
        base = g * A_GROUP_W
        r = jnp.dot(xn, w_ref[:, base:base + A_GROUP_W], preferred_element_type=F32)
        chunks = (norm_rope(r[:, 0:2 * LANES], TAB_AQ, HEAD_DIM // 2)
                  + norm_rope(r[:, 2 * LANES:4 * LANES], TAB_AK, HEAD_DIM // 2)
                  + [r[:, 4 * LANES:5 * LANES], r[:, 5 * LANES:6 * LANES]])
        for c, y in enumerate(chunks):
            if d == 1:
                a0_ref[:, c * LANES:(c + 1) * LANES] = y.astype(BF16)
            else:
                pbuf_ref[c] = y
        if d > 1:
            out_ref = a1_ref if g == 1 else a2_ref
            rows = tm // d
            for res in range(d):
                for c in range(A_GROUP_W // LANES):
                    col = res * A_GROUP_W + c * LANES
                    out_ref[:, col:col + LANES] = pbuf_ref[c, pl.ds(res, rows, stride=d), :].astype(BF16)

    base = N_GROUPS_A * A_GROUP_W
    r = jnp.dot(xn, w_ref[:, base:base + B_Q_W + B_KV_W], preferred_element_type=F32)
    chunks = (norm_rope(r[:, 0:2 * LANES], TAB_BQ, HEAD_DIM // 4)
              + norm_rope(r[:, 2 * LANES:4 * LANES], TAB_BQ, HEAD_DIM // 4)
              + norm_rope(r[:, 4 * LANES:5 * LANES], TAB_BK, HEAD_DIM // 4)
              + [r[:, 5 * LANES:6 * LANES]])
    for c, y in enumerate(chunks):
        if c < 4:
            qb_ref[:, c * LANES:(c + 1) * LANES] = y.astype(BF16)
        else:
            kvb_ref[:, (c - 4) * LANES:(c - 3) * LANES] = y.astype(BF16)


def _proj_call(x2d, g1, wqkv, tabs, bd, seq):
    n = x2d.shape[0]
    tm = TM_PROJ
    tiles_per_seq = seq // tm
    full = lambda shape: pl.BlockSpec(shape, lambda i: (0,) * len(shape))
    return pl.pallas_call(
        _proj_kernel,
        grid=(n // tm,),
        in_specs=[
            pl.BlockSpec((tm, D_MODEL), lambda i: (i, 0)),
            full((1, D_MODEL)),
            full((D_MODEL, QKV_W)),
            pl.BlockSpec((4, 3, tm, LANES), lambda i: (0, 0, i % tiles_per_seq, 0)),
            full((2 * LANES, 2 * LANES)),
        ],
        out_specs=[
            pl.BlockSpec((tm, A_GROUP_W), lambda i: (i, 0)),
            pl.BlockSpec((tm // 4, 4 * A_GROUP_W), lambda i: (i, 0)),
            pl.BlockSpec((tm // 16, 16 * A_GROUP_W), lambda i: (i, 0)),
            pl.BlockSpec((tm, B_Q_W), lambda i: (i, 0)),
            pl.BlockSpec((tm, B_KV_W), lambda i: (i, 0)),
        ],
        out_shape=[
            jax.ShapeDtypeStruct((n, A_GROUP_W), BF16),
            jax.ShapeDtypeStruct((n // 4, 4 * A_GROUP_W), BF16),
            jax.ShapeDtypeStruct((n // 16, 16 * A_GROUP_W), BF16),
            jax.ShapeDtypeStruct((n, B_Q_W), BF16),
            jax.ShapeDtypeStruct((n, B_KV_W), BF16),
        ],
        scratch_shapes=[pltpu.VMEM((A_GROUP_W // LANES, tm, LANES), F32)],
        compiler_params=_cparams(("arbitrary",)),
        name="proj_qkv",
    )(x2d, g1, wqkv, tabs, bd)


def _attn_a_kernel(a0_ref, a1_ref, a2_ref, ya_ref, acc_ref, m_ref, l_ref):
    tq = 128
    lane = lax.broadcasted_iota(I32, (tq, LANES), 1)
    low = lane < HEAD_DIM

    def tile(g, ref, col0, length, i, rows):
        win = min(2 * tq, length)
        if isinstance(i, int):
            q0 = i * tq
            ws = max(0, min(q0 - RADIUS, length - win))
        else:
            q0 = pl.multiple_of(i * tq, tq)
            ws = pl.multiple_of(jnp.clip(q0 - RADIUS, 0, length - win), RADIUS)
        qrow = lax.broadcasted_iota(I32, (2 * tq, win), 0)
        qpos = q0 + jnp.where(qrow >= tq, qrow - tq, qrow)
        kpos = ws + lax.broadcasted_iota(I32, (2 * tq, win), 1)
        valid = jnp.abs(qpos - kpos) <= RADIUS
        for hp in range(2):
            q2 = ref[0, pl.ds(q0, tq), col0 + hp * LANES:col0 + (hp + 1) * LANES]
            k2 = ref[0, pl.ds(ws, win), col0 + 256 + hp * LANES:col0 + 256 + (hp + 1) * LANES]
            v2 = ref[0, pl.ds(ws, win), col0 + 512 + hp * LANES:col0 + 512 + (hp + 1) * LANES]
            zero = jnp.zeros_like(q2)
            qs = jnp.concatenate([jnp.where(low, q2, zero), jnp.where(low, zero, q2)], axis=0)
            s = lax.dot_general(qs, k2, (((1,), (1,)), ((), ())), preferred_element_type=F32)
            s = jnp.where(valid, s, NEG_BIG)
            m = jnp.max(s, axis=-1, keepdims=True)
            p = jnp.exp2(s - m)
            l = jnp.sum(p, axis=-1, keepdims=True)
            pv = jnp.dot(p.astype(BF16), v2, preferred_element_type=F32)
            slot = 2 * g + hp
            acc_ref[slot, rows, :] = jnp.where(low, pv[:tq], pv[tq:])
            m_ref[slot, rows, :] = jnp.where(low, m[:tq], m[tq:])
            l_ref[slot, rows, :] = jnp.where(low, l[:tq], l[tq:])

    seq = a0_ref.shape[1]

    def g0_body(i, carry):
        tile(0, a0_ref, 0, seq, i, pl.ds(pl.multiple_of(i * tq, tq), tq))
        return carry
    lax.fori_loop(0, seq // tq, g0_body, 0, unroll=4)

    len1 = seq // 4
    for res in range(4):
        def g1_body(i, carry, res=res):
            tile(1, a1_ref, res * A_GROUP_W, len1, i, pl.ds(res + 4 * i * tq, tq, stride=4))
            return carry
        lax.fori_loop(0, len1 // tq, g1_body, 0, unroll=4)

    len2 = seq // 16
    for res in range(16):
        tile(2, a2_ref, res * A_GROUP_W, len2, 0, pl.ds(res, tq, stride=16))

    rc = 256

    def merge_body(j, carry):
        rows = pl.ds(pl.multiple_of(j * rc, rc), rc)
        for hp in range(2):
            ms = [m_ref[2 * g + hp, rows, :] for g in range(N_GROUPS_A)]
            mx = jnp.maximum(jnp.maximum(ms[0], ms[1]), ms[2])
            num = jnp.zeros((rc, LANES), F32)
            den = jnp.zeros((rc, LANES), F32)
            for g in range(N_GROUPS_A):
                e = jnp.exp2(ms[g] - mx)
                num = num + e * acc_ref[2 * g + hp, rows, :]
                den = den + e * l_ref[2 * g + hp, rows, :]
            ya_ref[0, rows, hp * LANES:(hp + 1) * LANES] = (num / den).astype(BF16)
        return carry
    lax.fori_loop(0, seq // rc, merge_body, 0)


def _attn_a_call(a0, a1, a2):
    b, seq, _ = a0.shape
    return pl.pallas_call(
        _attn_a_kernel,
        grid=(b,),
        in_specs=[
            pl.BlockSpec((1, seq, A_GROUP_W), lambda i: (i, 0, 0)),
            pl.BlockSpec((1, seq // 4, 4 * A_GROUP_W), lambda i: (i, 0, 0)),
            pl.BlockSpec((1, seq // 16, 16 * A_GROUP_W), lambda i: (i, 0, 0)),
        ],
        out_specs=pl.BlockSpec((1, seq, A_OUT_W), lambda i: (i, 0, 0)),
        out_shape=jax.ShapeDtypeStruct((b, seq, A_OUT_W), BF16),
        scratch_shapes=[pltpu.VMEM((N_GROUPS_A * A_OUT_W // LANES, seq, LANES), F32)] * 3,
        compiler_params=_cparams(("arbitrary",)),
        name="attn_a",
    )(a0, a1, a2)


def _attn_b_kernel(q_ref, kv_ref, o_ref, va_ref, vb_ref):
    tq = q_ref.shape[1]
    seq = kv_ref.shape[1]
    low = lax.broadcasted_iota(I32, (tq, LANES), 1) < HEAD_DIM

    @pl.when(pl.program_id(1) == 0)
    def _():
        low_k = lax.broadcasted_iota(I32, (seq, LANES), 1) < HEAD_DIM
        v2 = kv_ref[0, :, LANES:2 * LANES]
        one = jnp.ones_like(v2)
        va_ref[...] = jnp.where(low_k, v2, one)
        vb_ref[...] = jnp.where(low_k, one, v2)

    for pr in range(B_Q_W // LANES):
        q2 = q_ref[0, :, pr * LANES:(pr + 1) * LANES]
        zero = jnp.zeros_like(q2)
        qs = jnp.concatenate([jnp.where(low, q2, zero), jnp.where(low, zero, q2)], axis=0)
        m = jnp.full((2 * tq, 1), NEG_BIG, F32)
        acc_a = jnp.zeros((tq, LANES), F32)
        acc_b = jnp.zeros((tq, LANES), F32)
        for c in range(seq // TK_B):
            k2 = kv_ref[0, c * TK_B:(c + 1) * TK_B, 0:LANES]
            va = va_ref[c * TK_B:(c + 1) * TK_B, :]
            vb = vb_ref[c * TK_B:(c + 1) * TK_B, :]
            s = lax.dot_general(qs, k2, (((1,), (1,)), ((), ())), preferred_element_type=F32)
            m_n = jnp.maximum(m, jnp.max(s, axis=-1, keepdims=True))
            p = jnp.exp2(s - m_n).astype(BF16)
            a = jnp.exp2(m - m_n)
            acc_a = a[:tq] * acc_a + jnp.dot(p[:tq], va, preferred_element_type=F32)
            acc_b = a[tq:] * acc_b + jnp.dot(p[tq:], vb, preferred_element_type=F32)
            m = m_n
        oa = acc_a / pltpu.roll(acc_a, HEAD_DIM, 1)
        ob = acc_b / pltpu.roll(acc_b, HEAD_DIM, 1)
        o_ref[0, :, pr * LANES:(pr + 1) * LANES] = jnp.where(low, oa, ob).astype(BF16)


def _attn_b_call(qb, kvb):
    b, seq, _ = qb.shape
    return pl.pallas_call(
        _attn_b_kernel,
        grid=(b, seq // TQ_B),
        in_specs=[
            pl.BlockSpec((1, TQ_B, B_Q_W), lambda i, j: (i, j, 0)),
            pl.BlockSpec((1, seq, B_KV_W), lambda i, j: (i, 0, 0)),
        ],
        out_specs=pl.BlockSpec((1, TQ_B, B_Q_W), lambda i, j: (i, j, 0)),
        out_shape=jax.ShapeDtypeStruct((b, seq, B_Q_W), BF16),
        scratch_shapes=[pltpu.VMEM((seq, LANES), BF16)] * 2,
        compiler_params=_cparams(("arbitrary",) * 2),
        name="attn_b",
    )(qb, kvb)


def _pack_bf16_pairs(v):
    half = v.shape[1] // 2
    lo = lax.bitcast_convert_type(v[:, :half].astype(BF16).astype(F32), U32)
    hi = lax.bitcast_convert_type(v[:, half:].astype(BF16).astype(F32), U32)
    return (lo >> 16) | (hi & jnp.uint32(0xFFFF0000))


def _unpack_bf16_pairs(p):
    lo = lax.bitcast_convert_type(p << 16, F32)
    hi = lax.bitcast_convert_type(p & jnp.uint32(0xFFFF0000), F32)
    return lo, hi


def _merge_kernel(x_ref, ya_ref, yb_ref, g1_ref, wg_ref, wba_ref, wbb_ref, wo_ref, g2_ref,
                  wrh_ref, wrl_ref, bias_ref, wsgu_ref, wsd_ref, tri_ref,
                  base_ref, xp_ref, eidx_ref, rank_ref, wgt_ref, cnt_ref, carry_ref):
    step = pl.program_id(0)

    @pl.when(step == 0)
    def _():
        carry_ref[...] = jnp.zeros_like(carry_ref)

    x = x_ref[...]
    tm = x.shape[0]
    ms = jnp.mean(x * x, axis=-1, keepdims=True)
    xn = (x * lax.rsqrt(ms + EPS) * g1_ref[...]).astype(BF16)
    gates = jax.nn.sigmoid(jnp.dot(xn, wg_ref[...], preferred_element_type=F32))
    pa = jnp.dot(ya_ref[...], wba_ref[...], preferred_element_type=F32)
    pb = jnp.dot(yb_ref[...], wbb_ref[...], preferred_element_type=F32)
    merged = gates[:, :D_MODEL] * pa + gates[:, D_MODEL:] * pb
    h = x + jnp.dot(merged.astype(BF16), wo_ref[...], preferred_element_type=F32)

    ms2 = jnp.mean(h * h, axis=-1, keepdims=True)
    xn2 = h * lax.rsqrt(ms2 + EPS) * g2_ref[...]
    xn2b = xn2.astype(BF16)
    xp_ref[...] = _pack_bf16_pairs(xn2)

    gu = jnp.dot(xn2b, wsgu_ref[...], preferred_element_type=F32)
    hs = (jax.nn.silu(gu[:, :EXPERT_FF]) * gu[:, EXPERT_FF:]).astype(BF16)
    base_ref[...] = h + jnp.dot(hs, wsd_ref[...], preferred_element_type=F32)

    xlo = (xn2 - xn2b.astype(F32)).astype(BF16)
    logits = (jnp.dot(xn2b, wrh_ref[...], preferred_element_type=F32)
              + jnp.dot(xlo, wrh_ref[...], preferred_element_type=F32)
              + jnp.dot(xn2b, wrl_ref[...], preferred_element_type=F32))
    scores = jax.nn.sigmoid(logits.T[:N_EXPERTS, :])
    sel = scores + bias_ref[...]

    gsz = N_EXPERTS // N_EXPERT_GROUPS
    iota8 = lax.broadcasted_iota(I32, (gsz, tm), 0).astype(F32)
    ninf = jnp.float32(-jnp.inf)
    blocks, gs = [], []
    for g in range(N_EXPERT_GROUPS):
        blk = sel[g * gsz:(g + 1) * gsz, :]
        m1 = jnp.max(blk, axis=0, keepdims=True)
        first = jnp.min(jnp.where(blk == m1, iota8, float(gsz)), axis=0, keepdims=True)
        m2 = jnp.max(jnp.where(iota8 == first, ninf, blk), axis=0, keepdims=True)
        blocks.append(blk)
        gs.append(m1 + m2)
    cur = jnp.concatenate(gs, axis=0)
    gmask = jnp.zeros((N_EXPERT_GROUPS, tm), F32)
    for _ in range(TOPK_GROUPS):
        mx = jnp.max(cur, axis=0, keepdims=True)
        fi = jnp.min(jnp.where(cur == mx, iota8, float(N_EXPERT_GROUPS)), axis=0, keepdims=True)
        pick = iota8 == fi
        gmask = jnp.where(pick, 1.0, gmask)
        cur = jnp.where(pick, ninf, cur)
    cur = jnp.concatenate(
        [jnp.where(gmask[g:g + 1, :] > 0.5, blocks[g], ninf) for g in range(N_EXPERT_GROUPS)], axis=0)

    iota64 = lax.broadcasted_iota(I32, (N_EXPERTS, tm), 0).astype(F32)
    idxs, wks = [], []
    onehot = jnp.zeros((N_EXPERTS, tm), F32)
    for _ in range(TOP_K):
        mx = jnp.max(cur, axis=0, keepdims=True)
        fi = jnp.min(jnp.where(cur == mx, iota64, float(N_EXPERTS)), axis=0, keepdims=True)
        pick = iota64 == fi
        wks.append(jnp.sum(jnp.where(pick, scores, 0.0), axis=0, keepdims=True))
        idxs.append(fi)
        onehot = jnp.where(pick, 1.0, onehot)
        cur = jnp.where(pick, ninf, cur)
    wk = jnp.concatenate(wks, axis=0)
    wgt_ref[...] = wk / jnp.sum(wk, axis=0, keepdims=True) * ROUTED_SCALE
    eidx_ref[...] = jnp.concatenate(idxs, axis=0).astype(I32)

    before = jnp.dot(onehot.astype(BF16), tri_ref[...], preferred_element_type=F32) + carry_ref[:, 0:1]
    rank_ref[...] = jnp.concatenate(
        [jnp.sum(jnp.where(iota64 == fi, before, 0.0), axis=0, keepdims=True) for fi in idxs],
        axis=0).astype(I32)
    carry_ref[...] = carry_ref[...] + jnp.sum(onehot, axis=1, keepdims=True)
    cnt_ref[...] = carry_ref[...]


def _merge_call(x2d, ya, yb, g1, wg, wba, wbb, wo, g2, wrh, wrl, bias, wsgu, wsd, tri, row0, n):
    tm = TM_MERGE
    off = row0 // tm
    full = lambda shape: pl.BlockSpec(shape, lambda i: (0,) * len(shape))
    row_in = lambda w: pl.BlockSpec((tm, w), lambda i: (i + off, 0))
    row = lambda w: pl.BlockSpec((tm, w), lambda i: (i, 0))
    col = lambda: pl.BlockSpec((TOP_K, tm), lambda i: (0, i))
    return pl.pallas_call(
        _merge_kernel,
        grid=(n // tm,),
        in_specs=[
            row_in(D_MODEL), row_in(A_OUT_W), row_in(B_Q_W),
            full((1, D_MODEL)), full((D_MODEL, 2 * D_MODEL)),
            full((A_OUT_W, D_MODEL)), full((B_Q_W, D_MODEL)), full((D_MODEL, D_MODEL)),
            full((1, D_MODEL)),
            full((D_MODEL, LANES)), full((D_MODEL, LANES)), full((N_EXPERTS, 1)),
            full((D_MODEL, 2 * EXPERT_FF)), full((EXPERT_FF, D_MODEL)),
            full((tm, tm)),
        ],
        out_specs=[row(D_MODEL), row(D_MODEL // 2), col(), col(), col(), full((N_EXPERTS, LANES))],
        out_shape=[
            jax.ShapeDtypeStruct((n, D_MODEL), F32),
            jax.ShapeDtypeStruct((n, D_MODEL // 2), U32),
            jax.ShapeDtypeStruct((TOP_K, n), I32),
            jax.ShapeDtypeStruct((TOP_K, n), I32),
            jax.ShapeDtypeStruct((TOP_K, n), F32),
            jax.ShapeDtypeStruct((N_EXPERTS, LANES), F32),
        ],
        scratch_shapes=[pltpu.VMEM((N_EXPERTS, LANES), F32)],
        compiler_params=_cparams(("arbitrary",)),
        name="merge_router",
    )(x2d, ya, yb, g1, wg, wba, wbb, wo, g2, wrh, wrl, bias, wsgu, wsd, tri)


EXPERT_SUBBLOCKS = 4


X_SLOTS = 3
Y_SLOTS = 2


def _expert_kernel(be_ref, nb_ref, xs_hbm, *refs):
    nw = 3 * EXPERT_SUBBLOCKS
    w_refs = refs[:nw]
    ys_hbm, xbuf, ybuf, xsem, ysem = refs[nw:]
    j = pl.program_id(0)
    nsteps = pl.num_programs(0)
    rows = EXPERT_SUBBLOCKS * BM_EXPERT

    def x_copy(step, slot):
        src = xs_hbm.at[pl.ds(pl.multiple_of(step * rows, rows), rows)]
        return pltpu.make_async_copy(src, xbuf.at[slot], xsem.at[slot])

    def y_copy(step, slot):
        dst = ys_hbm.at[pl.ds(pl.multiple_of(step * rows, rows), rows)]
        return pltpu.make_async_copy(ybuf.at[slot], dst, ysem.at[slot])

    @pl.when(j == 0)
    def _():
        x_copy(0, 0).start()
        x_copy(1, 1).start()

    @pl.when(j + 2 < nsteps)
    def _():
        x_copy(j + 2, lax.rem(j + 2, X_SLOTS)).start()

    xslot = lax.rem(j, X_SLOTS)
    yslot = lax.rem(j, Y_SLOTS)
    x_copy(j, xslot).wait()

    @pl.when(j >= Y_SLOTS)
    def _():
        y_copy(j - Y_SLOTS, yslot).wait()

    first = j * EXPERT_SUBBLOCKS

    @pl.when(first >= nb_ref[0])
    def _():
        ybuf[yslot] = jnp.zeros((rows, D_MODEL // 2), U32)

    @pl.when(first < nb_ref[0])
    def _():
        half = D_MODEL // 2
        for sub in range(EXPERT_SUBBLOCKS):
            wg_ref, wu_ref, wd_ref = w_refs[3 * sub:3 * sub + 3]
            blk = slice(sub * BM_EXPERT, (sub + 1) * BM_EXPERT)
            lo, hi = _unpack_bf16_pairs(xbuf[xslot, blk, :])
            lo = lo.astype(BF16)
            hi = hi.astype(BF16)
            gate = (jnp.dot(lo, wg_ref[0, :half, :], preferred_element_type=F32)
                    + jnp.dot(hi, wg_ref[0, half:, :], preferred_element_type=F32))
            up = (jnp.dot(lo, wu_ref[0, :half, :], preferred_element_type=F32)
                  + jnp.dot(hi, wu_ref[0, half:, :], preferred_element_type=F32))
            hb = (jax.nn.silu(gate) * up).astype(BF16)
            ybuf[yslot, blk, :] = _pack_bf16_pairs(jnp.dot(hb, wd_ref[0], preferred_element_type=F32))

    y_copy(j, yslot).start()

    @pl.when(j == nsteps - 1)
    def _():
        y_copy(j - 1, lax.rem(j - 1, Y_SLOTS)).wait()
        y_copy(j, yslot).wait()


def _expert_call(block_e, n_used, xs, wg, wu, wd):
    p_len = xs.shape[0]
    nsub = EXPERT_SUBBLOCKS
    rows = nsub * BM_EXPERT
    w_specs, w_args = [], []
    for sub in range(nsub):
        pick = lambda j, be, nb, sub=sub: (be[nsub * j + sub], 0, 0)
        w_specs += [
            pl.BlockSpec((1, D_MODEL, EXPERT_FF), pick),
            pl.BlockSpec((1, D_MODEL, EXPERT_FF), pick),
            pl.BlockSpec((1, EXPERT_FF, D_MODEL), pick),
        ]
        w_args += [wg, wu, wd]
    assert p_len // rows >= max(X_SLOTS, Y_SLOTS)
    return pl.pallas_call(
        _expert_kernel,
        grid_spec=pltpu.PrefetchScalarGridSpec(
            num_scalar_prefetch=2,
            grid=(p_len // rows,),
            in_specs=[pl.BlockSpec(memory_space=pl.ANY)] + w_specs,
            out_specs=pl.BlockSpec(memory_space=pl.ANY),
            scratch_shapes=[
                pltpu.VMEM((X_SLOTS, rows, D_MODEL // 2), U32),
                pltpu.VMEM((Y_SLOTS, rows, D_MODEL // 2), U32),
                pltpu.SemaphoreType.DMA((X_SLOTS,)),
                pltpu.SemaphoreType.DMA((Y_SLOTS,)),
            ],
        ),
        out_shape=jax.ShapeDtypeStruct((p_len, D_MODEL // 2), U32),
        compiler_params=_cparams(("arbitrary",)),
        name="routed_experts",
    )(block_e, n_used, xs, *w_args)


SC_CORES = 2
SC_SUBCORES = 16
SC_WORKERS = SC_CORES * SC_SUBCORES
SC_CHUNK = 128


def _sc_mesh():
    return plsc.VectorSubcoreMesh(core_axis_name="c", subcore_axis_name="s",
                                  num_cores=SC_CORES, num_subcores=SC_SUBCORES)


def _dispatch_rows(xp, dest3, p_len):
    n, width = xp.shape
    n_chunks = dest3.shape[0]
    per_worker = n_chunks // SC_WORKERS

    @functools.partial(
        pl.kernel, mesh=_sc_mesh(),
        out_type=jax.ShapeDtypeStruct((p_len, width), xp.dtype),
        scratch_types=[pltpu.VMEM((TOP_K, SC_CHUNK), I32), pltpu.VMEM((SC_CHUNK, width), xp.dtype),
                       pltpu.SemaphoreType.DMA],
        name="sc_dispatch")
    def body(xp_hbm, dest_hbm, xs_hbm, idx_v, rows_v, sem):
        wid = lax.axis_index("s") * SC_CORES + lax.axis_index("c")

        @pl.loop(0, per_worker)
        def _(j):
            chunk = wid * per_worker + j
            pltpu.sync_copy(dest_hbm.at[chunk], idx_v)
            pltpu.sync_copy(xp_hbm.at[pl.ds(chunk * SC_CHUNK, SC_CHUNK)], rows_v)
            copies = [pltpu.async_copy(rows_v, xs_hbm.at[idx_v.at[k]], sem) for k in range(TOP_K)]
            for cp in copies:
                cp.wait()

    return body(xp, dest3)


SC_GATHER_CHUNK = 64


def _gather_rows(ys, dest3):
    width = ys.shape[1]
    n_chunks, _, gc = dest3.shape
    n = n_chunks * gc
    per_worker = n_chunks // SC_WORKERS

    @functools.partial(
        pl.kernel, mesh=_sc_mesh(),
        out_type=jax.ShapeDtypeStruct((TOP_K, n, width), ys.dtype),
        scratch_types=[pltpu.VMEM((TOP_K, gc), I32),
                       pltpu.VMEM((gc, width), ys.dtype), pltpu.VMEM((gc, width), ys.dtype),
                       pltpu.SemaphoreType.DMA, pltpu.SemaphoreType.DMA,
                       pltpu.SemaphoreType.DMA, pltpu.SemaphoreType.DMA],
        name="sc_gather")
    def body(ys_hbm, dest_hbm, out_hbm, idx_v, buf0, buf1, gsem0, gsem1, wsem0, wsem1):
        wid = lax.axis_index("s") * SC_CORES + lax.axis_index("c")
        bufs, gsems, wsems = (buf0, buf1), (gsem0, gsem1), (wsem0, wsem1)

        @pl.loop(0, per_worker)
        def _(j):
            chunk = wid * per_worker + j
            pltpu.sync_copy(dest_hbm.at[chunk], idx_v)
            rows = pl.ds(chunk * gc, gc)
            gathers = [None] * TOP_K
            writes = [None] * TOP_K
            gathers[0] = pltpu.async_copy(ys_hbm.at[idx_v.at[0]], bufs[0], gsems[0])
            for k in range(TOP_K):
                gathers[k].wait()
                if k >= 1:
                    writes[k - 1].wait()
                if k + 1 < TOP_K:
                    nxt = (k + 1) % 2
                    gathers[k + 1] = pltpu.async_copy(ys_hbm.at[idx_v.at[k + 1]], bufs[nxt], gsems[nxt])
                writes[k] = pltpu.async_copy(bufs[k % 2], out_hbm.at[k, rows], wsems[k % 2])
            writes[TOP_K - 1].wait()

    return body(ys, dest3)


def _combine_kernel(base_ref, g_ref, w_ref, *refs):
    o_ref = refs[-1]
    half = D_MODEL // 2
    lo_acc = base_ref[:, :half]
    hi_acc = base_ref[:, half:]
    for k in range(TOP_K):
        lo, hi = _unpack_bf16_pairs(g_ref[k])
        wk = w_ref[:, k:k + 1]
        lo_acc = lo_acc + wk * lo
        hi_acc = hi_acc + wk * hi
    o_ref[:, :half] = lo_acc
    o_ref[:, half:] = hi_acc


def _combine_call(base, gathered, w_t, prev, row0, n_total):
    n = base.shape[0]
    tm = TM_COMBINE
    off = row0 // tm
    in_specs = [
        pl.BlockSpec((tm, D_MODEL), lambda i: (i, 0)),
        pl.BlockSpec((TOP_K, tm, D_MODEL // 2), lambda i: (0, i, 0)),
        pl.BlockSpec((tm, TOP_K), lambda i: (i, 0)),
    ]
    args = [base, gathered, w_t]
    aliases = {}
    if prev is not None:
        in_specs.append(pl.BlockSpec(memory_space=pl.ANY))
        args.append(prev)
        aliases = {3: 0}
    return pl.pallas_call(
        _combine_kernel,
        grid=(n // tm,),
        in_specs=in_specs,
        out_specs=pl.BlockSpec((tm, D_MODEL), lambda i: (i + off, 0)),
        out_shape=jax.ShapeDtypeStruct((n_total, D_MODEL), F32),
        input_output_aliases=aliases,
        compiler_params=_cparams(("arbitrary",)),
        name="moe_combine",
    )(*args)


SC_LANES = 16
SC_REDUCE_CHUNK = 16


def _reduce_rows(ys, dest3, wb):
    width = ys.shape[1]
    n_chunks, _, tc = dest3.shape
    n = n_chunks * tc
    per_worker = n_chunks // SC_WORKERS

    @functools.partial(
        pl.kernel, mesh=_sc_mesh(),
        out_type=jax.ShapeDtypeStruct((n, 2 * width), F32),
        scratch_types=[pltpu.VMEM((TOP_K, tc), I32),
                       pltpu.VMEM((TOP_K, tc, width), ys.dtype),
                       pltpu.VMEM((tc, TOP_K, SC_LANES), F32),
                       pltpu.VMEM((tc, 2 * width), F32),
                       pltpu.SemaphoreType.DMA],
        compiler_params=pltpu.CompilerParams(needs_layout_passes=False),
        name="sc_reduce")
    def body(ys_hbm, dest_hbm, wb_hbm, out_hbm, idx_v, rows_v, w_v, r_v, sem):
        wid = lax.axis_index("s") * SC_CORES + lax.axis_index("c")

        @pl.loop(0, per_worker)
        def _(j):
            chunk = wid * per_worker + j
            pltpu.sync_copy(dest_hbm.at[chunk], idx_v)
            pltpu.sync_copy(wb_hbm.at[pl.ds(chunk * tc, tc)], w_v)
            copies = [pltpu.async_copy(ys_hbm.at[idx_v.at[k]], rows_v.at[k], sem) for k in range(TOP_K)]
            for cp in copies:
                cp.wait()

            @pl.loop(0, tc)
            def _(t):
                ws = [w_v[t, k, :] for k in range(TOP_K)]

                @pl.loop(0, width // SC_LANES)
                def _(jv):
                    col = pl.multiple_of(jv * SC_LANES, SC_LANES)
                    lo_acc = jnp.zeros((SC_LANES,), F32)
                    hi_acc = jnp.zeros((SC_LANES,), F32)
                    for k in range(TOP_K):
                        word = rows_v[k, t, pl.ds(col, SC_LANES)]
                        lo = lax.bitcast_convert_type(word << 16, F32)
                        hi = lax.bitcast_convert_type(word & jnp.uint32(0xFFFF0000), F32)
                        lo_acc = lo_acc + ws[k] * lo
                        hi_acc = hi_acc + ws[k] * hi
                    r_v[t, pl.ds(col, SC_LANES)] = lo_acc
                    r_v[t, pl.ds(width + col, SC_LANES)] = hi_acc

            pltpu.sync_copy(r_v, out_hbm.at[pl.ds(chunk * tc, tc)])

    return body(ys, dest3, wb)


def _final_add_kernel(base_ref, r_ref, *refs):
    o_ref = refs[-1]
    o_ref[...] = base_ref[...] + r_ref[...]


def _final_add_call(base, routed, prev, row0, n_total):
    n = base.shape[0]
    tm = TM_COMBINE
    off = row0 // tm
    spec = pl.BlockSpec((tm, D_MODEL), lambda i: (i, 0))
    in_specs, args, aliases = [spec, spec], [base, routed], {}
    if prev is not None:
        in_specs.append(pl.BlockSpec(memory_space=pl.ANY))
        args.append(prev)
        aliases = {2: 0}
    return pl.pallas_call(
        _final_add_kernel,
        grid=(n // tm,),
        in_specs=in_specs,
        out_specs=pl.BlockSpec((tm, D_MODEL), lambda i: (i + off, 0)),
        out_shape=jax.ShapeDtypeStruct((n_total, D_MODEL), F32),
        input_output_aliases=aliases,
        compiler_params=_cparams(("arbitrary",)),
        name="moe_final_add",
    )(*args)


def _fold_gain(tab, gain, half):
    g = jnp.tile(gain.astype(F32), LANES // HEAD_DIM)
    return jnp.stack([tab[0] * g, tab[1] * jnp.roll(g, LANES - half), tab[2] * jnp.roll(g, half)])


def _prep_tables(seq):
    pos = jnp.arange(seq)
    cos1, sin1 = _rope_tables(pos, HEAD_DIM)
    cos_r, sin_r = _rope_tables(pos // GRID_W, HEAD_DIM // 2)
    cos_c, sin_c = _rope_tables(pos % GRID_W, HEAD_DIM // 2)
    taba = _lane_tables(cos1, sin1, HEAD_DIM // 2)
    tabb = _lane_tables(jnp.concatenate([cos_r, cos_c], -1), jnp.concatenate([sin_r, sin_c], -1),
                        HEAD_DIM // 4)
    return taba, tabb


def _layer(h, p, taba, tabb):
    b, seq, d = h.shape
    n = b * seq
    x2d = h.reshape(n, d)
    w_in = p["w_in"]
    o1 = N_GROUPS_A * A_GROUP_W
    o2 = o1 + B_Q_W
    o3 = o2 + B_KV_W
    pair_heads = jnp.array([0, 4, 1, 5, 2, 6, 3, 7])
    pair_cols = (pair_heads[:, None] * HEAD_DIM + jnp.arange(HEAD_DIM)[None, :]).reshape(-1)
    wqkv = jnp.concatenate([w_in[:, :o1], w_in[:, o1:o2][:, pair_cols], w_in[:, o2:o3]], axis=1).astype(BF16)
    tabs = jnp.stack([
        _fold_gain(taba, p["q_norm_a"] * Q_SCALE, HEAD_DIM // 2),
        _fold_gain(taba, p["k_norm_a"], HEAD_DIM // 2),
        _fold_gain(tabb, p["q_norm_b"] * Q_SCALE, HEAD_DIM // 4),
        _fold_gain(tabb, p["k_norm_b"], HEAD_DIM // 4),
    ])
    seg = jnp.arange(2 * LANES) // HEAD_DIM
    bd = (seg[:, None] == seg[None, :]).astype(BF16)
    g1 = p["norm1_g"].reshape(1, d).astype(F32)

    a0, a1, a2, qb, kvb = _proj_call(x2d, g1, wqkv, tabs, bd, seq)
    ya = _attn_a_call(a0.reshape(b, seq, A_GROUP_W), a1.reshape(b, seq // 4, 4 * A_GROUP_W),
                      a2.reshape(b, seq // 16, 16 * A_GROUP_W))
    yb = _attn_b_call(qb.reshape(b, seq, B_Q_W), kvb.reshape(b, seq, B_KV_W))

    wg = w_in[:, o3:].astype(BF16)
    wbb = p["w_branch_b"][pair_cols, :].astype(BF16)
    wr = jnp.pad(p["w_router"], ((0, 0), (0, LANES - N_EXPERTS)))
    wrh = wr.astype(BF16)
    wrl = (wr - wrh.astype(F32)).astype(BF16)
    wsgu = jnp.concatenate([p["ws_gate"], p["ws_up"]], axis=1).astype(BF16)
    tm = TM_MERGE
    tri = (jnp.arange(tm)[:, None] < jnp.arange(tm)[None, :]).astype(BF16)
    wba = p["w_branch_a"].astype(BF16)
    wo = p["w_out"].astype(BF16)
    g2 = p["norm2_g"].reshape(1, d).astype(F32)
    bias = p["router_bias"].reshape(N_EXPERTS, 1).astype(F32)
    wsd = p["ws_down"].astype(BF16)
    weg = p["we_gate"].astype(BF16)
    weu = p["we_up"].astype(BF16)
    wd = p["we_down"].astype(BF16)
    ya2d = ya.reshape(n, A_OUT_W)
    yb2d = yb.reshape(n, B_Q_W)

    nc = n // MOE_CHUNKS
    bm = BM_EXPERT
    n_blocks = (nc * TOP_K + N_EXPERTS * (bm - 1)) // bm + 1
    n_blocks = -(-n_blocks // EXPERT_SUBBLOCKS) * EXPERT_SUBBLOCKS
    p_len = n_blocks * bm
    block_start = jnp.arange(n_blocks, dtype=I32) * bm
    out = None
    for c in range(MOE_CHUNKS):
        base, xp, eidx, rank, wgt, cnt = _merge_call(
            x2d, ya2d, yb2d, g1, wg, wba, wbb, wo, g2, wrh, wrl, bias, wsgu, wsd, tri, c * nc, nc)
        counts = cnt[:, 0].astype(I32)
        padded = (counts + bm - 1) // bm * bm
        pend = jnp.cumsum(padded)
        pstart = pend - padded
        onehot = eidx[:, :, None] == jnp.arange(N_EXPERTS)[None, None, :]
        dest = jnp.sum(jnp.where(onehot, pstart[None, None, :], 0), axis=-1) + rank
        block_e = jnp.minimum(jnp.sum(pend[None, :] <= block_start[:, None], axis=1),
                              N_EXPERTS - 1).astype(I32)
        n_used = (pend[-1] // bm).astype(I32).reshape(1)
        dest3 = dest.reshape(TOP_K, nc // SC_CHUNK, SC_CHUNK).transpose(1, 0, 2)
        dest3r = dest.reshape(TOP_K, nc // SC_REDUCE_CHUNK, SC_REDUCE_CHUNK).transpose(1, 0, 2)
        wb = jnp.broadcast_to(wgt.T[:, :, None], (nc, TOP_K, SC_LANES))

        xs = _dispatch_rows(xp, dest3, p_len)
        ys = _expert_call(block_e, n_used, xs, weg, weu, wd)
        routed = _reduce_rows(ys, dest3r, wb)
        out = _final_add_call(base, routed, out, c * nc, n)
    return out.reshape(b, seq, d)


def kernel(x, norm1_g, w_in, q_norm_a, k_norm_a, q_norm_b, k_norm_b, w_branch_a, w_branch_b, w_out,
           norm2_g, w_router, router_bias, we_gate, we_up, we_down, ws_gate, ws_up, ws_down):
    params = dict(norm1_g=norm1_g, w_in=w_in, q_norm_a=q_norm_a, k_norm_a=k_norm_a, q_norm_b=q_norm_b,
                  k_norm_b=k_norm_b, w_branch_a=w_branch_a, w_branch_b=w_branch_b, w_out=w_out,
                  norm2_g=norm2_g, w_router=w_router, router_bias=router_bias, we_gate=we_gate,
                  we_up=we_up, we_down=we_down, ws_gate=ws_gate, ws_up=ws_up, ws_down=ws_down)
    taba, tabb = _prep_tables(x.shape[1])
    h = x
    for l in range(norm1_g.shape[0]):
        h = _layer(h, {k: v[l] for k, v in params.items()}, taba, tabb)
    return h
```

```python
import functools

import jax
import jax.numpy as jnp
from jax import lax
from jax.experimental import pallas as pl
from jax.experimental.pallas import tpu as pltpu
from jax.experimental.pallas import tpu_sc as plsc

F32 = jnp.float32
BF16 = jnp.bfloat16
I32 = jnp.int32
U32 = jnp.uint32

D_MODEL = 1024
HEAD_DIM = 64
ROPE_THETA = 10000.0
EPS = 1e-6
GRID_W = 64
DILATIONS = (1, 4, 16)
RADIUS = 64
N_GROUPS_A = 3
A_GROUP_W = 768
A_OUT_W = 256
B_Q_W = 512
B_KV_W = 256
QKV_W = N_GROUPS_A * A_GROUP_W + B_Q_W + B_KV_W
N_EXPERTS = 64
N_EXPERT_GROUPS = 8
TOPK_GROUPS = 4
TOP_K = 8
EXPERT_FF = 256
ROUTED_SCALE = 2.5

LANES = 128
NEG_BIG = -1e30
Q_SCALE = HEAD_DIM ** -0.5 * 1.4426950408889634

TM_PROJ = 1024
TQ_B = 512
TK_B = 512
TM_MERGE = 256
BM_EXPERT = 512
TM_COMBINE = 512
MOE_CHUNKS = 2
VMEM_LIMIT = 56 * 1024 * 1024


def _cparams(sem):
    return pltpu.CompilerParams(dimension_semantics=sem, vmem_limit_bytes=VMEM_LIMIT)


def _rope_tables(pos, dim):
    inv = ROPE_THETA ** (-jnp.arange(0, dim, 2, dtype=F32) / dim)
    ang = pos.astype(F32)[:, None] * inv[None, :]
    ang = jnp.concatenate([ang, ang], axis=-1)
    return jnp.cos(ang), jnp.sin(ang)


def _lane_tables(cos_h, sin_h, half):
    cos2 = jnp.concatenate([cos_h, cos_h], axis=-1)
    sin2 = jnp.concatenate([sin_h, sin_h], axis=-1)
    first = (jnp.arange(LANES) % (2 * half)) < half
    s_left = jnp.where(first[None, :], -sin2, 0.0)
    s_right = jnp.where(first[None, :], 0.0, sin2)
    return jnp.stack([cos2, s_left, s_right]).astype(F32)


TAB_AQ, TAB_AK, TAB_BQ, TAB_BK = range(4)


def _proj_kernel(x_ref, g1_ref, w_ref, tab_ref, bd_ref,
                 a0_ref, a1_ref, a2_ref, qb_ref, kvb_ref, pbuf_ref):
    x = x_ref[...]
    ms = jnp.mean(x * x, axis=-1, keepdims=True)
    xn = (x * lax.rsqrt(ms + EPS) * g1_ref[...]).astype(BF16)
    tm = x.shape[0]

    def norm_rope(y, t, shift):
        w = y.shape[1]
        ss = jnp.dot((y * y).astype(BF16), bd_ref[0:w, 0:w], preferred_element_type=F32)
        yn = y * lax.rsqrt(ss + EPS)
        out = []
        for i in range(w // LANES):
            z = yn[:, i * LANES:(i + 1) * LANES]
            out.append(z * tab_ref[t, 0] + pltpu.roll(z, LANES - shift, 1) * tab_ref[t, 1]
                       + pltpu.roll(z, shift, 1) * tab_ref[t, 2])
        return out

    for g, d in enumerate(DILATIONS):
        base = g * A_GROUP_W
        r = jnp.dot(xn, w_ref[:, base:base + A_GROUP_W], preferred_element_type=F32)
        chunks = (norm_rope(r[:, 0:2 * LANES], TAB_AQ, HEAD_DIM // 2)
                  + norm_rope(r[:, 2 * LANES:4 * LANES], TAB_AK, HEAD_DIM // 2)
                  + [r[:, 4 * LANES:5 * LANES], r[:, 5 * LANES:6 * LANES]])
        for c, y in enumerate(chunks):
            if d == 1:
                a0_ref[:, c * LANES:(c + 1) * LANES] = y.astype(BF16)
            else:
                pbuf_ref[c] = y
        if d > 1:
            out_ref = a1_ref if g == 1 else a2_ref
            rows = tm // d
            for res in range(d):
                for c in range(A_GROUP_W // LANES):
                    col = res * A_GROUP_W + c * LANES
                    out_ref[:, col:col + LANES] = pbuf_ref[c, pl.ds(res, rows, stride=d), :].astype(BF16)

    base = N_GROUPS_A * A_GROUP_W
    r = jnp.dot(xn, w_ref[:, base:base + B_Q_W + B_KV_W], preferred_element_type=F32)
    chunks = (norm_rope(r[:, 0:2 * LANES], TAB_BQ, HEAD_DIM // 4)
              + norm_rope(r[:, 2 * LANES:4 * LANES], TAB_BQ, HEAD_DIM // 4)
              + norm_rope(r[:, 4 * LANES:5 * LANES], TAB_BK, HEAD_DIM // 4)
              + [r[:, 5 * LANES:6 * LANES]])
    for c, y in enumerate(chunks):
        if c < 4:
            qb_ref[:, c * LANES:(c + 1) * LANES] = y.astype(BF16)
        else:
            kvb_ref[:, (c - 4) * LANES:(c - 3) * LANES] = y.astype(BF16)


def _proj_call(x2d, g1, wqkv, tabs, bd, seq):
    n = x2d.shape[0]
    tm = TM_PROJ
    tiles_per_seq = seq // tm
    full = lambda shape: pl.BlockSpec(shape, lambda i: (0,) * len(shape))
    return pl.pallas_call(
        _proj_kernel,
        grid=(n // tm,),
        in_specs=[
            pl.BlockSpec((tm, D_MODEL), lambda i: (i, 0)),
            full((1, D_MODEL)),
            full((D_MODEL, QKV_W)),
            pl.BlockSpec((4, 3, tm, LANES), lambda i: (0, 0, i % tiles_per_seq, 0)),
            full((2 * LANES, 2 * LANES)),
        ],
        out_specs=[
            pl.BlockSpec((tm, A_GROUP_W), lambda i: (i, 0)),
            pl.BlockSpec((tm // 4, 4 * A_GROUP_W), lambda i: (i, 0)),
            pl.BlockSpec((tm // 16, 16 * A_GROUP_W), lambda i: (i, 0)),
            pl.BlockSpec((tm, B_Q_W), lambda i: (i, 0)),
            pl.BlockSpec((tm, B_KV_W), lambda i: (i, 0)),
        ],
        out_shape=[
            jax.ShapeDtypeStruct((n, A_GROUP_W), BF16),
            jax.ShapeDtypeStruct((n // 4, 4 * A_GROUP_W), BF16),
            jax.ShapeDtypeStruct((n // 16, 16 * A_GROUP_W), BF16),
            jax.ShapeDtypeStruct((n, B_Q_W), BF16),
            jax.ShapeDtypeStruct((n, B_KV_W), BF16),
        ],
        scratch_shapes=[pltpu.VMEM((A_GROUP_W // LANES, tm, LANES), F32)],
        compiler_params=_cparams(("arbitrary",)),
        name="proj_qkv",
    )(x2d, g1, wqkv, tabs, bd)


def _attn_a_kernel(a0_ref, a1_ref, a2_ref, ya_ref, acc_ref, m_ref, l_ref):
    tq = 128
    lane = lax.broadcasted_iota(I32, (tq, LANES), 1)
    low = lane < HEAD_DIM

    def tile(g, ref, col0, length, i, rows):
        win = min(2 * tq, length)
        if isinstance(i, int):
            q0 = i * tq
            ws = max(0, min(q0 - RADIUS, length - win))
        else:
            q0 = pl.multiple_of(i * tq, tq)
            ws = pl.multiple_of(jnp.clip(q0 - RADIUS, 0, length - win), RADIUS)
        qrow = lax.broadcasted_iota(I32, (2 * tq, win), 0)
        qpos = q0 + jnp.where(qrow >= tq, qrow - tq, qrow)
        kpos = ws + lax.broadcasted_iota(I32, (2 * tq, win), 1)
        valid = jnp.abs(qpos - kpos) <= RADIUS
        for hp in range(2):
            q2 = ref[0, pl.ds(q0, tq), col0 + hp * LANES:col0 + (hp + 1) * LANES]
            k2 = ref[0, pl.ds(ws, win), col0 + 256 + hp * LANES:col0 + 256 + (hp + 1) * LANES]
            v2 = ref[0, pl.ds(ws, win), col0 + 512 + hp * LANES:col0 + 512 + (hp + 1) * LANES]
            zero = jnp.zeros_like(q2)
            qs = jnp.concatenate([jnp.where(low, q2, zero), jnp.where(low, zero, q2)], axis=0)
            s = lax.dot_general(qs, k2, (((1,), (1,)), ((), ())), preferred_element_type=F32)
            s = jnp.where(valid, s, NEG_BIG)
            m = jnp.max(s, axis=-1, keepdims=True)
            p = jnp.exp2(s - m)
            l = jnp.sum(p, axis=-1, keepdims=True)
            pv = jnp.dot(p.astype(BF16), v2, preferred_element_type=F32)
            slot = 2 * g + hp
            acc_ref[slot, rows, :] = jnp.where(low, pv[:tq], pv[tq:])
            m_ref[slot, rows, :] = jnp.where(low, m[:tq], m[tq:])
            l_ref[slot, rows, :] = jnp.where(low, l[:tq], l[tq:])

    seq = a0_ref.shape[1]

    def g0_body(i, carry):
        tile(0, a0_ref, 0, seq, i, pl.ds(pl.multiple_of(i * tq, tq), tq))
        return carry
    lax.fori_loop(0, seq // tq, g0_body, 0, unroll=4)

    len1 = seq // 4
    for res in range(4):
        def g1_body(i, carry, res=res):
            tile(1, a1_ref, res * A_GROUP_W, len1, i, pl.ds(res + 4 * i * tq, tq, stride=4))
            return carry
        lax.fori_loop(0, len1 // tq, g1_body, 0, unroll=4)

    len2 = seq // 16
    for res in range(16):
        tile(2, a2_ref, res * A_GROUP_W, len2, 0, pl.ds(res, tq, stride=16))

    rc = 256

    def merge_body(j, carry):
        rows = pl.ds(pl.multiple_of(j * rc, rc), rc)
        for hp in range(2):
            ms = [m_ref[2 * g + hp, rows, :] for g in range(N_GROUPS_A)]
            mx = jnp.maximum(jnp.maximum(ms[0], ms[1]), ms[2])
            num = jnp.zeros((rc, LANES), F32)
            den = jnp.zeros((rc, LANES), F32)
            for g in range(N_GROUPS_A):
                e = jnp.exp2(ms[g] - mx)
                num = num + e * acc_ref[2 * g + hp, rows, :]
                den = den + e * l_ref[2 * g + hp, rows, :]
            ya_ref[0, rows, hp * LANES:(hp + 1) * LANES] = (num / den).astype(BF16)
        return carry
    lax.fori_loop(0, seq // rc, merge_body, 0)


def _attn_a_call(a0, a1, a2):
    b, seq, _ = a0.shape
    return pl.pallas_call(
        _attn_a_kernel,
        grid=(b,),
        in_specs=[
            pl.BlockSpec((1, seq, A_GROUP_W), lambda i: (i, 0, 0)),
            pl.BlockSpec((1, seq // 4, 4 * A_GROUP_W), lambda i: (i, 0, 0)),
            pl.BlockSpec((1, seq // 16, 16 * A_GROUP_W), lambda i: (i, 0, 0)),
        ],
        out_specs=pl.BlockSpec((1, seq, A_OUT_W), lambda i: (i, 0, 0)),
        out_shape=jax.ShapeDtypeStruct((b, seq, A_OUT_W), BF16),
        scratch_shapes=[pltpu.VMEM((N_GROUPS_A * A_OUT_W // LANES, seq, LANES), F32)] * 3,
        compiler_params=_cparams(("arbitrary",)),
        name="attn_a",
    )(a0, a1, a2)


def _attn_b_kernel(q_ref, kv_ref, o_ref, va_ref, vb_ref):
    tq = q_ref.shape[1]
    seq = kv_ref.shape[1]
    low = lax.broadcasted_iota(I32, (tq, LANES), 1) < HEAD_DIM

    @pl.when(pl.program_id(1) == 0)
    def _():
        low_k = lax.broadcasted_iota(I32, (seq, LANES), 1) < HEAD_DIM
        v2 = kv_ref[0, :, LANES:2 * LANES]
        one = jnp.ones_like(v2)
        va_ref[...] = jnp.where(low_k, v2, one)
        vb_ref[...] = jnp.where(low_k, one, v2)

    for pr in range(B_Q_W // LANES):
        q2 = q_ref[0, :, pr * LANES:(pr + 1) * LANES]
        zero = jnp.zeros_like(q2)
        qs = jnp.concatenate([jnp.where(low, q2, zero), jnp.where(low, zero, q2)], axis=0)
        m = jnp.full((2 * tq, 1), NEG_BIG, F32)
        acc_a = jnp.zeros((tq, LANES), F32)
        acc_b = jnp.zeros((tq, LANES), F32)
        for c in range(seq // TK_B):
            k2 = kv_ref[0, c * TK_B:(c + 1) * TK_B, 0:LANES]
            va = va_ref[c * TK_B:(c + 1) * TK_B, :]
            vb = vb_ref[c * TK_B:(c + 1) * TK_B, :]
            s = lax.dot_general(qs, k2, (((1,), (1,)), ((), ())), preferred_element_type=F32)
            m_n = jnp.maximum(m, jnp.max(s, axis=-1, keepdims=True))
            p = jnp.exp2(s - m_n).astype(BF16)
            a = jnp.exp2(m - m_n)
            acc_a = a[:tq] * acc_a + jnp.dot(p[:tq], va, preferred_element_type=F32)
            acc_b = a[tq:] * acc_b + jnp.dot(p[tq:], vb, preferred_element_type=F32)
            m = m_n
        oa = acc_a / pltpu.roll(acc_a, HEAD_DIM, 1)
        ob = acc_b / pltpu.roll(acc_b, HEAD_DIM, 1)
        o_ref[0, :, pr * LANES:(pr + 1) * LANES] = jnp.where(low, oa, ob).astype(BF16)


def _attn_b_call(qb, kvb):
    b, seq, _ = qb.shape
    return pl.pallas_call(
        _attn_b_kernel,
        grid=(b, seq // TQ_B),
        in_specs=[
            pl.BlockSpec((1, TQ_B, B_Q_W), lambda i, j: (i, j, 0)),
            pl.BlockSpec((1, seq, B_KV_W), lambda i, j: (i, 0, 0)),
        ],
        out_specs=pl.BlockSpec((1, TQ_B, B_Q_W), lambda i, j: (i, j, 0)),
        out_shape=jax.ShapeDtypeStruct((b, seq, B_Q_W), BF16),
        scratch_shapes=[pltpu.VMEM((seq, LANES), BF16)] * 2,
        compiler_params=_cparams(("arbitrary",) * 2),
        name="attn_b",
    )(qb, kvb)


def _pack_bf16_pairs(v):
    half = v.shape[1] // 2
    lo = lax.bitcast_convert_type(v[:, :half].astype(BF16).astype(F32), U32)
    hi = lax.bitcast_convert_type(v[:, half:].astype(BF16).astype(F32), U32)
    return (lo >> 16) | (hi & jnp.uint32(0xFFFF0000))


def _unpack_bf16_pairs(p):
    lo = lax.bitcast_convert_type(p << 16, F32)
    hi = lax.bitcast_convert_type(p & jnp.uint32(0xFFFF0000), F32)
    return lo, hi


def _merge_kernel(x_ref, ya_ref, yb_ref, g1_ref, wg_ref, wba_ref, wbb_ref, wo_ref, g2_ref,
                  wrh_ref, wrl_ref, bias_ref, wsgu_ref, wsd_ref, tri_ref,
                  base_ref, xp_ref, eidx_ref, rank_ref, wgt_ref, cnt_ref, carry_ref):
    step = pl.program_id(0)

    @pl.when(step == 0)
    def _():
        carry_ref[...] = jnp.zeros_like(carry_ref)

    x = x_ref[...]
    tm = x.shape[0]
    ms = jnp.mean(x * x, axis=-1, keepdims=True)
    xn = (x * lax.rsqrt(ms + EPS) * g1_ref[...]).astype(BF16)
    gates = jax.nn.sigmoid(jnp.dot(xn, wg_ref[...], preferred_element_type=F32))
    pa = jnp.dot(ya_ref[...], wba_ref[...], preferred_element_type=F32)
    pb = jnp.dot(yb_ref[...], wbb_ref[...], preferred_element_type=F32)
    merged = gates[:, :D_MODEL] * pa + gates[:, D_MODEL:] * pb
    h = x + jnp.dot(merged.astype(BF16), wo_ref[...], preferred_element_type=F32)

    ms2 = jnp.mean(h * h, axis=-1, keepdims=True)
    xn2 = h * lax.rsqrt(ms2 + EPS) * g2_ref[...]
    xn2b = xn2.astype(BF16)
    xp_ref[...] = _pack_bf16_pairs(xn2)

    gu = jnp.dot(xn2b, wsgu_ref[...], preferred_element_type=F32)
    hs = (jax.nn.silu(gu[:, :EXPERT_FF]) * gu[:, EXPERT_FF:]).astype(BF16)
    base_ref[...] = h + jnp.dot(hs, wsd_ref[...], preferred_element_type=F32)

    xlo = (xn2 - xn2b.astype(F32)).astype(BF16)
    logits = (jnp.dot(xn2b, wrh_ref[...], preferred_element_type=F32)
              + jnp.dot(xlo, wrh_ref[...], preferred_element_type=F32)
              + jnp.dot(xn2b, wrl_ref[...], preferred_element_type=F32))
    scores = jax.nn.sigmoid(logits.T[:N_EXPERTS, :])
    sel = scores + bias_ref[...]

    gsz = N_EXPERTS // N_EXPERT_GROUPS
    iota8 = lax.broadcasted_iota(I32, (gsz, tm), 0).astype(F32)
    ninf = jnp.float32(-jnp.inf)
    blocks, gs = [], []
    for g in range(N_EXPERT_GROUPS):
        blk = sel[g * gsz:(g + 1) * gsz, :]
        m1 = jnp.max(blk, axis=0, keepdims=True)
        first = jnp.min(jnp.where(blk == m1, iota8, float(gsz)), axis=0, keepdims=True)
        m2 = jnp.max(jnp.where(iota8 == first, ninf, blk), axis=0, keepdims=True)
        blocks.append(blk)
        gs.append(m1 + m2)
    cur = jnp.concatenate(gs, axis=0)
    gmask = jnp.zeros((N_EXPERT_GROUPS, tm), F32)
    for _ in range(TOPK_GROUPS):
        mx = jnp.max(cur, axis=0, keepdims=True)
        fi = jnp.min(jnp.where(cur == mx, iota8, float(N_EXPERT_GROUPS)), axis=0, keepdims=True)
        pick = iota8 == fi
        gmask = jnp.where(pick, 1.0, gmask)
        cur = jnp.where(pick, ninf, cur)
    cur = jnp.concatenate(
        [jnp.where(gmask[g:g + 1, :] > 0.5, blocks[g], ninf) for g in range(N_EXPERT_GROUPS)], axis=0)

    iota64 = lax.broadcasted_iota(I32, (N_EXPERTS, tm), 0).astype(F32)
    idxs, wks = [], []
    onehot = jnp.zeros((N_EXPERTS, tm), F32)
    for _ in range(TOP_K):
        mx = jnp.max(cur, axis=0, keepdims=True)
        fi = jnp.min(jnp.where(cur == mx, iota64, float(N_EXPERTS)), axis=0, keepdims=True)
        pick = iota64 == fi
        wks.append(jnp.sum(jnp.where(pick, scores, 0.0), axis=0, keepdims=True))
        idxs.append(fi)
        onehot = jnp.where(pick, 1.0, onehot)
        cur = jnp.where(pick, ninf, cur)
    wk = jnp.concatenate(wks, axis=0)
    wgt_ref[...] = wk / jnp.sum(wk, axis=0, keepdims=True) * ROUTED_SCALE
    eidx_ref[...] = jnp.concatenate(idxs, axis=0).astype(I32)

    before = jnp.dot(onehot.astype(BF16), tri_ref[...], preferred_element_type=F32) + carry_ref[:, 0:1]
    rank_ref[...] = jnp.concatenate(
        [jnp.sum(jnp.where(iota64 == fi, before, 0.0), axis=0, keepdims=True) for fi in idxs],
        axis=0).astype(I32)
    carry_ref[...] = carry_ref[...] + jnp.sum(onehot, axis=1, keepdims=True)
    cnt_ref[...] = carry_ref[...]


def _merge_call(x2d, ya, yb, g1, wg, wba, wbb, wo, g2, wrh, wrl, bias, wsgu, wsd, tri, row0, n):
    tm = TM_MERGE
    off = row0 // tm
    full = lambda shape: pl.BlockSpec(shape, lambda i: (0,) * len(shape))
    row_in = lambda w: pl.BlockSpec((tm, w), lambda i: (i + off, 0))
    row = lambda w: pl.BlockSpec((tm, w), lambda i: (i, 0))
    col = lambda: pl.BlockSpec((TOP_K, tm), lambda i: (0, i))
    return pl.pallas_call(
        _merge_kernel,
        grid=(n // tm,),
        in_specs=[
            row_in(D_MODEL), row_in(A_OUT_W), row_in(B_Q_W),
            full((1, D_MODEL)), full((D_MODEL, 2 * D_MODEL)),
            full((A_OUT_W, D_MODEL)), full((B_Q_W, D_MODEL)), full((D_MODEL, D_MODEL)),
            full((1, D_MODEL)),
            full((D_MODEL, LANES)), full((D_MODEL, LANES)), full((N_EXPERTS, 1)),
            full((D_MODEL, 2 * EXPERT_FF)), full((EXPERT_FF, D_MODEL)),
            full((tm, tm)),
        ],
        out_specs=[row(D_MODEL), row(D_MODEL // 2), col(), col(), col(), full((N_EXPERTS, LANES))],
        out_shape=[
            jax.ShapeDtypeStruct((n, D_MODEL), F32),
            jax.ShapeDtypeStruct((n, D_MODEL // 2), U32),
            jax.ShapeDtypeStruct((TOP_K, n), I32),
            jax.ShapeDtypeStruct((TOP_K, n), I32),
            jax.ShapeDtypeStruct((TOP_K, n), F32),
            jax.ShapeDtypeStruct((N_EXPERTS, LANES), F32),
        ],
        scratch_shapes=[pltpu.VMEM((N_EXPERTS, LANES), F32)],
        compiler_params=_cparams(("arbitrary",)),
        name="merge_router",
    )(x2d, ya, yb, g1, wg, wba, wbb, wo, g2, wrh, wrl, bias, wsgu, wsd, tri)


EXPERT_SUBBLOCKS = 4


X_SLOTS = 3
Y_SLOTS = 2


def _expert_kernel(be_ref, nb_ref, xs_hbm, *refs):
    nw = 3 * EXPERT_SUBBLOCKS
    w_refs = refs[:nw]
    ys_hbm, xbuf, ybuf, xsem, ysem = refs[nw:]
    j = pl.program_id(0)
    nsteps = pl.num_programs(0)
    rows = EXPERT_SUBBLOCKS * BM_EXPERT

    def x_copy(step, slot):
        src = xs_hbm.at[pl.ds(pl.multiple_of(step * rows, rows), rows)]
        return pltpu.make_async_copy(src, xbuf.at[slot], xsem.at[slot])

    def y_copy(step, slot):
        dst = ys_hbm.at[pl.ds(pl.multiple_of(step * rows, rows), rows)]
        return pltpu.make_async_copy(ybuf.at[slot], dst, ysem.at[slot])

    @pl.when(j == 0)
    def _():
        x_copy(0, 0).start()
        x_copy(1, 1).start()

    @pl.when(j + 2 < nsteps)
    def _():
        x_copy(j + 2, lax.rem(j + 2, X_SLOTS)).start()

    xslot = lax.rem(j, X_SLOTS)
    yslot = lax.rem(j, Y_SLOTS)
    x_copy(j, xslot).wait()

    @pl.when(j >= Y_SLOTS)
    def _():
        y_copy(j - Y_SLOTS, yslot).wait()

    first = j * EXPERT_SUBBLOCKS

    @pl.when(first >= nb_ref[0])
    def _():
        ybuf[yslot] = jnp.zeros((rows, D_MODEL // 2), U32)

    @pl.when(first < nb_ref[0])
    def _():
        half = D_MODEL // 2
        for sub in range(EXPERT_SUBBLOCKS):
            wg_ref, wu_ref, wd_ref = w_refs[3 * sub:3 * sub + 3]
            blk = slice(sub * BM_EXPERT, (sub + 1) * BM_EXPERT)
            lo, hi = _unpack_bf16_pairs(xbuf[xslot, blk, :])
            lo = lo.astype(BF16)
            hi = hi.astype(BF16)
            gate = (jnp.dot(lo, wg_ref[0, :half, :], preferred_element_type=F32)
                    + jnp.dot(hi, wg_ref[0, half:, :], preferred_element_type=F32))
            up = (jnp.dot(lo, wu_ref[0, :half, :], preferred_element_type=F32)
                  + jnp.dot(hi, wu_ref[0, half:, :], preferred_element_type=F32))
            hb = (jax.nn.silu(gate) * up).astype(BF16)
            ybuf[yslot, blk, :] = _pack_bf16_pairs(jnp.dot(hb, wd_ref[0], preferred_element_type=F32))

    y_copy(j, yslot).start()

    @pl.when(j == nsteps - 1)
    def _():
        y_copy(j - 1, lax.rem(j - 1, Y_SLOTS)).wait()
        y_copy(j, yslot).wait()


def _expert_call(block_e, n_used, xs, wg, wu, wd):
    p_len = xs.shape[0]
    nsub = EXPERT_SUBBLOCKS
    rows = nsub * BM_EXPERT
    w_specs, w_args = [], []
    for sub in range(nsub):
        pick = lambda j, be, nb, sub=sub: (be[nsub * j + sub], 0, 0)
        w_specs += [
            pl.BlockSpec((1, D_MODEL, EXPERT_FF), pick),
            pl.BlockSpec((1, D_MODEL, EXPERT_FF), pick),
            pl.BlockSpec((1, EXPERT_FF, D_MODEL), pick),
        ]
        w_args += [wg, wu, wd]
    assert p_len // rows >= max(X_SLOTS, Y_SLOTS)
    return pl.pallas_call(
        _expert_kernel,
        grid_spec=pltpu.PrefetchScalarGridSpec(
            num_scalar_prefetch=2,
            grid=(p_len // rows,),
            in_specs=[pl.BlockSpec(memory_space=pl.ANY)] + w_specs,
            out_specs=pl.BlockSpec(memory_space=pl.ANY),
            scratch_shapes=[
                pltpu.VMEM((X_SLOTS, rows, D_MODEL // 2), U32),
                pltpu.VMEM((Y_SLOTS, rows, D_MODEL // 2), U32),
                pltpu.SemaphoreType.DMA((X_SLOTS,)),
                pltpu.SemaphoreType.DMA((Y_SLOTS,)),
            ],
        ),
        out_shape=jax.ShapeDtypeStruct((p_len, D_MODEL // 2), U32),
        compiler_params=_cparams(("arbitrary",)),
        name="routed_experts",
    )(block_e, n_used, xs, *w_args)


SC_CORES = 2
SC_SUBCORES = 16
SC_WORKERS = SC_CORES * SC_SUBCORES
SC_CHUNK = 128


def _sc_mesh():
    return plsc.VectorSubcoreMesh(core_axis_name="c", subcore_axis_name="s",
                                  num_cores=SC_CORES, num_subcores=SC_SUBCORES)


def _dispatch_rows(xp, dest3, p_len):
    n, width = xp.shape
    n_chunks = dest3.shape[0]
    per_worker = n_chunks // SC_WORKERS

    @functools.partial(
        pl.kernel, mesh=_sc_mesh(),
        out_type=jax.ShapeDtypeStruct((p_len, width), xp.dtype),
        scratch_types=[pltpu.VMEM((TOP_K, SC_CHUNK), I32), pltpu.VMEM((SC_CHUNK, width), xp.dtype),
                       pltpu.SemaphoreType.DMA],
        name="sc_dispatch")
    def body(xp_hbm, dest_hbm, xs_hbm, idx_v, rows_v, sem):
        wid = lax.axis_index("s") * SC_CORES + lax.axis_index("c")

        @pl.loop(0, per_worker)
        def _(j):
            chunk = wid * per_worker + j
            pltpu.sync_copy(dest_hbm.at[chunk], idx_v)
            pltpu.sync_copy(xp_hbm.at[pl.ds(chunk * SC_CHUNK, SC_CHUNK)], rows_v)
            copies = [pltpu.async_copy(rows_v, xs_hbm.at[idx_v.at[k]], sem) for k in range(TOP_K)]
            for cp in copies:
                cp.wait()

    return body(xp, dest3)


SC_GATHER_CHUNK = 64


def _gather_rows(ys, dest3):
    width = ys.shape[1]
    n_chunks, _, gc = dest3.shape
    n = n_chunks * gc
    per_worker = n_chunks // SC_WORKERS

    @functools.partial(
        pl.kernel, mesh=_sc_mesh(),
        out_type=jax.ShapeDtypeStruct((TOP_K, n, width), ys.dtype),
        scratch_types=[pltpu.VMEM((TOP_K, gc), I32),
                       pltpu.VMEM((gc, width), ys.dtype), pltpu.VMEM((gc, width), ys.dtype),
                       pltpu.SemaphoreType.DMA, pltpu.SemaphoreType.DMA,
                       pltpu.SemaphoreType.DMA, pltpu.SemaphoreType.DMA],
        name="sc_gather")
    def body(ys_hbm, dest_hbm, out_hbm, idx_v, buf0, buf1, gsem0, gsem1, wsem0, wsem1):
        wid = lax.axis_index("s") * SC_CORES + lax.axis_index("c")
        bufs, gsems, wsems = (buf0, buf1), (gsem0, gsem1), (wsem0, wsem1)

        @pl.loop(0, per_worker)
        def _(j):
            chunk = wid * per_worker + j
            pltpu.sync_copy(dest_hbm.at[chunk], idx_v)
            rows = pl.ds(chunk * gc, gc)
            gathers = [None] * TOP_K
            writes = [None] * TOP_K
            gathers[0] = pltpu.async_copy(ys_hbm.at[idx_v.at[0]], bufs[0], gsems[0])
            for k in range(TOP_K):
                gathers[k].wait()
                if k >= 1:
                    writes[k - 1].wait()
                if k + 1 < TOP_K:
                    nxt = (k + 1) % 2
                    gathers[k + 1] = pltpu.async_copy(ys_hbm.at[idx_v.at[k + 1]], bufs[nxt], gsems[nxt])
                writes[k] = pltpu.async_copy(bufs[k % 2], out_hbm.at[k, rows], wsems[k % 2])
            writes[TOP_K - 1].wait()

    return body(ys, dest3)


def _combine_kernel(base_ref, g_ref, w_ref, *refs):
    o_ref = refs[-1]
    half = D_MODEL // 2
    lo_acc = base_ref[:, :half]
    hi_acc = base_ref[:, half:]
    for k in range(TOP_K):
        lo, hi = _unpack_bf16_pairs(g_ref[k])
        wk = w_ref[:, k:k + 1]
        lo_acc = lo_acc + wk * lo
        hi_acc = hi_acc + wk * hi
    o_ref[:, :half] = lo_acc
    o_ref[:, half:] = hi_acc


def _combine_call(base, gathered, w_t, prev, row0, n_total):
    n = base.shape[0]
    tm = TM_COMBINE
    off = row0 // tm
    in_specs = [
        pl.BlockSpec((tm, D_MODEL), lambda i: (i, 0)),
        pl.BlockSpec((TOP_K, tm, D_MODEL // 2), lambda i: (0, i, 0)),
        pl.BlockSpec((tm, TOP_K), lambda i: (i, 0)),
    ]
    args = [base, gathered, w_t]
    aliases = {}
    if prev is not None:
        in_specs.append(pl.BlockSpec(memory_space=pl.ANY))
        args.append(prev)
        aliases = {3: 0}
    return pl.pallas_call(
        _combine_kernel,
        grid=(n // tm,),
        in_specs=in_specs,
        out_specs=pl.BlockSpec((tm, D_MODEL), lambda i: (i + off, 0)),
        out_shape=jax.ShapeDtypeStruct((n_total, D_MODEL), F32),
        input_output_aliases=aliases,
        compiler_params=_cparams(("arbitrary",)),
        name="moe_combine",
    )(*args)


def _fold_gain(tab, gain, half):
    g = jnp.tile(gain.astype(F32), LANES // HEAD_DIM)
    return jnp.stack([tab[0] * g, tab[1] * jnp.roll(g, LANES - half), tab[2] * jnp.roll(g, half)])


def _prep_tables(seq):
    pos = jnp.arange(seq)
    cos1, sin1 = _rope_tables(pos, HEAD_DIM)
    cos_r, sin_r = _rope_tables(pos // GRID_W, HEAD_DIM // 2)
    cos_c, sin_c = _rope_tables(pos % GRID_W, HEAD_DIM // 2)
    taba = _lane_tables(cos1, sin1, HEAD_DIM // 2)
    tabb = _lane_tables(jnp.concatenate([cos_r, cos_c], -1), jnp.concatenate([sin_r, sin_c], -1),
                        HEAD_DIM // 4)
    return taba, tabb


def _layer(h, p, taba, tabb):
    b, seq, d = h.shape
    n = b * seq
    x2d = h.reshape(n, d)
    w_in = p["w_in"]
    o1 = N_GROUPS_A * A_GROUP_W
    o2 = o1 + B_Q_W
    o3 = o2 + B_KV_W
    pair_heads = jnp.array([0, 4, 1, 5, 2, 6, 3, 7])
    pair_cols = (pair_heads[:, None] * HEAD_DIM + jnp.arange(HEAD_DIM)[None, :]).reshape(-1)
    wqkv = jnp.concatenate([w_in[:, :o1], w_in[:, o1:o2][:, pair_cols], w_in[:, o2:o3]], axis=1).astype(BF16)
    tabs = jnp.stack([
        _fold_gain(taba, p["q_norm_a"] * Q_SCALE, HEAD_DIM // 2),
        _fold_gain(taba, p["k_norm_a"], HEAD_DIM // 2),
        _fold_gain(tabb, p["q_norm_b"] * Q_SCALE, HEAD_DIM // 4),
        _fold_gain(tabb, p["k_norm_b"], HEAD_DIM // 4),
    ])
    seg = jnp.arange(2 * LANES) // HEAD_DIM
    bd = jnp.where(seg[:, None] == seg[None, :], 1.0 / HEAD_DIM, 0.0).astype(BF16)
    g1 = p["norm1_g"].reshape(1, d).astype(F32)

    a0, a1, a2, qb, kvb = _proj_call(x2d, g1, wqkv, tabs, bd, seq)
    ya = _attn_a_call(a0.reshape(b, seq, A_GROUP_W), a1.reshape(b, seq // 4, 4 * A_GROUP_W),
                      a2.reshape(b, seq // 16, 16 * A_GROUP_W))
    yb = _attn_b_call(qb.reshape(b, seq, B_Q_W), kvb.reshape(b, seq, B_KV_W))

    wg = w_in[:, o3:].astype(BF16)
    wbb = p["w_branch_b"][pair_cols, :].astype(BF16)
    wr = jnp.pad(p["w_router"], ((0, 0), (0, LANES - N_EXPERTS)))
    wrh = wr.astype(BF16)
    wrl = (wr - wrh.astype(F32)).astype(BF16)
    wsgu = jnp.concatenate([p["ws_gate"], p["ws_up"]], axis=1).astype(BF16)
    tm = TM_MERGE
    tri = (jnp.arange(tm)[:, None] < jnp.arange(tm)[None, :]).astype(BF16)
    wba = p["w_branch_a"].astype(BF16)
    wo = p["w_out"].astype(BF16)
    g2 = p["norm2_g"].reshape(1, d).astype(F32)
    bias = p["router_bias"].reshape(N_EXPERTS, 1).astype(F32)
    wsd = p["ws_down"].astype(BF16)
    weg = p["we_gate"].astype(BF16)
    weu = p["we_up"].astype(BF16)
    wd = p["we_down"].astype(BF16)
    ya2d = ya.reshape(n, A_OUT_W)
    yb2d = yb.reshape(n, B_Q_W)

    nc = n // MOE_CHUNKS
    bm = BM_EXPERT
    n_blocks = (nc * TOP_K + N_EXPERTS * (bm - 1)) // bm + 1
    n_blocks = -(-n_blocks // EXPERT_SUBBLOCKS) * EXPERT_SUBBLOCKS
    p_len = n_blocks * bm
    block_start = jnp.arange(n_blocks, dtype=I32) * bm
    out = None
    for c in range(MOE_CHUNKS):
        base, xp, eidx, rank, wgt, cnt = _merge_call(
            x2d, ya2d, yb2d, g1, wg, wba, wbb, wo, g2, wrh, wrl, bias, wsgu, wsd, tri, c * nc, nc)
        counts = cnt[:, 0].astype(I32)
        padded = (counts + bm - 1) // bm * bm
        pend = jnp.cumsum(padded)
        pstart = pend - padded
        onehot = eidx[:, :, None] == jnp.arange(N_EXPERTS)[None, None, :]
        dest = jnp.sum(jnp.where(onehot, pstart[None, None, :], 0), axis=-1) + rank
        block_e = jnp.minimum(jnp.sum(pend[None, :] <= block_start[:, None], axis=1),
                              N_EXPERTS - 1).astype(I32)
        n_used = (pend[-1] // bm).astype(I32).reshape(1)
        dest3 = dest.reshape(TOP_K, nc // SC_CHUNK, SC_CHUNK).transpose(1, 0, 2)
        dest3g = dest.reshape(TOP_K, nc // SC_GATHER_CHUNK, SC_GATHER_CHUNK).transpose(1, 0, 2)

        xs = _dispatch_rows(xp, dest3, p_len)
        ys = _expert_call(block_e, n_used, xs, weg, weu, wd)
        gathered = _gather_rows(ys, dest3g)
        out = _combine_call(base, gathered, wgt.T, out, c * nc, n)
    return out.reshape(b, seq, d)


def kernel(x, norm1_g, w_in, q_norm_a, k_norm_a, q_norm_b, k_norm_b, w_branch_a, w_branch_b, w_out,
           norm2_g, w_router, router_bias, we_gate, we_up, we_down, ws_gate, ws_up, ws_down):
    params = dict(norm1_g=norm1_g, w_in=w_in, q_norm_a=q_norm_a, k_norm_a=k_norm_a, q_norm_b=q_norm_b,
                  k_norm_b=k_norm_b, w_branch_a=w_branch_a, w_branch_b=w_branch_b, w_out=w_out,
                  norm2_g=norm2_g, w_router=w_router, router_bias=router_bias, we_gate=we_gate,
                  we_up=we_up, we_down=we_down, ws_gate=ws_gate, ws_up=ws_up, ws_down=ws_down)
    taba, tabb = _prep_tables(x.shape[1])
    h = x
    for l in range(norm1_g.shape[0]):
        h = _layer(h, {k: v[l] for k, v in params.items()}, taba, tabb)
    return h
```

```python
import functools

import jax
import jax.numpy as jnp
from jax import lax
from jax.experimental import pallas as pl
from jax.experimental.pallas import tpu as pltpu
from jax.experimental.pallas import tpu_sc as plsc

F32 = jnp.float32
BF16 = jnp.bfloat16
I32 = jnp.int32
U32 = jnp.uint32

D_MODEL = 1024
HEAD_DIM = 64
ROPE_THETA = 10000.0
EPS = 1e-6
GRID_W = 64
DILATIONS = (1, 4, 16)
RADIUS = 64
N_GROUPS_A = 3
A_GROUP_W = 768
A_OUT_W = 256
B_Q_W = 512
B_KV_W = 256
QKV_W = N_GROUPS_A * A_GROUP_W + B_Q_W + B_KV_W
N_EXPERTS = 64
N_EXPERT_GROUPS = 8
TOPK_GROUPS = 4
TOP_K = 8
EXPERT_FF = 256
ROUTED_SCALE = 2.5

LANES = 128
NEG_BIG = -1e30
Q_SCALE = HEAD_DIM ** -0.5 * 1.4426950408889634

TM_PROJ = 1024
TQ_B = 512
TK_B = 512
TM_MERGE = 256
ROUTE_TILE = 256
BM_EXPERT = 512
TM_COMBINE = 512
MOE_CHUNKS = 2
VMEM_LIMIT = 56 * 1024 * 1024


def _cparams(sem):
    return pltpu.CompilerParams(dimension_semantics=sem, vmem_limit_bytes=VMEM_LIMIT)


def _rope_tables(pos, dim):
    inv = ROPE_THETA ** (-jnp.arange(0, dim, 2, dtype=F32) / dim)
    ang = pos.astype(F32)[:, None] * inv[None, :]
    ang = jnp.concatenate([ang, ang], axis=-1)
    return jnp.cos(ang), jnp.sin(ang)


def _lane_tables(cos_h, sin_h, half):
    cos2 = jnp.concatenate([cos_h, cos_h], axis=-1)
    sin2 = jnp.concatenate([sin_h, sin_h], axis=-1)
    first = (jnp.arange(LANES) % (2 * half)) < half
    s_left = jnp.where(first[None, :], -sin2, 0.0)
    s_right = jnp.where(first[None, :], 0.0, sin2)
    return jnp.stack([cos2, s_left, s_right]).astype(F32)


TAB_AQ, TAB_AK, TAB_BQ, TAB_BK = range(4)


def _proj_kernel(x_ref, g1_ref, w_ref, tab_ref, bd_ref,
                 a0_ref, a1_ref, a2_ref, qb_ref, kvb_ref, pbuf_ref):
    x = x_ref[...]
    ms = jnp.mean(x * x, axis=-1, keepdims=True)
    xn = (x * lax.rsqrt(ms + EPS) * g1_ref[...]).astype(BF16)
    tm = x.shape[0]

    def norm_rope(y, t, shift):
        w = y.shape[1]
        ss = jnp.dot((y * y).astype(BF16), bd_ref[0:w, 0:w], preferred_element_type=F32)
        yn = y * lax.rsqrt(ss + EPS)
        out = []
        for i in range(w // LANES):
            z = yn[:, i * LANES:(i + 1) * LANES]
            out.append(z * tab_ref[t, 0] + pltpu.roll(z, LANES - shift, 1) * tab_ref[t, 1]
                       + pltpu.roll(z, shift, 1) * tab_ref[t, 2])
        return out

    for g, d in enumerate(DILATIONS):
        base = g * A_GROUP_W
        r = jnp.dot(xn, w_ref[:, base:base + A_GROUP_W], preferred_element_type=F32)
        chunks = (norm_rope(r[:, 0:2 * LANES], TAB_AQ, HEAD_DIM // 2)
                  + norm_rope(r[:, 2 * LANES:4 * LANES], TAB_AK, HEAD_DIM // 2)
                  + [r[:, 4 * LANES:5 * LANES], r[:, 5 * LANES:6 * LANES]])
        for c, y in enumerate(chunks):
            if d == 1:
                a0_ref[:, c * LANES:(c + 1) * LANES] = y.astype(BF16)
            else:
                pbuf_ref[c] = y
        if d > 1:
            out_ref = a1_ref if g == 1 else a2_ref
            rows = tm // d
            for res in range(d):
                for c in range(A_GROUP_W // LANES):
                    col = res * A_GROUP_W + c * LANES
                    out_ref[:, col:col + LANES] = pbuf_ref[c, pl.ds(res, rows, stride=d), :].astype(BF16)

    base = N_GROUPS_A * A_GROUP_W
    r = jnp.dot(xn, w_ref[:, base:base + B_Q_W + B_KV_W], preferred_element_type=F32)
    chunks = (norm_rope(r[:, 0:2 * LANES], TAB_BQ, HEAD_DIM // 4)
              + norm_rope(r[:, 2 * LANES:4 * LANES], TAB_BQ, HEAD_DIM // 4)
              + norm_rope(r[:, 4 * LANES:5 * LANES], TAB_BK, HEAD_DIM // 4)
              + [r[:, 5 * LANES:6 * LANES]])
    for c, y in enumerate(chunks):
        if c < 4:
            qb_ref[:, c * LANES:(c + 1) * LANES] = y.astype(BF16)
        else:
            kvb_ref[:, (c - 4) * LANES:(c - 3) * LANES] = y.astype(BF16)


def _proj_call(x2d, g1, wqkv, tabs, bd, seq):
    n = x2d.shape[0]
    tm = TM_PROJ
    tiles_per_seq = seq // tm
    full = lambda shape: pl.BlockSpec(shape, lambda i: (0,) * len(shape))
    return pl.pallas_call(
        _proj_kernel,
        grid=(n // tm,),
        in_specs=[
            pl.BlockSpec((tm, D_MODEL), lambda i: (i, 0)),
            full((1, D_MODEL)),
            full((D_MODEL, QKV_W)),
            pl.BlockSpec((4, 3, tm, LANES), lambda i: (0, 0, i % tiles_per_seq, 0)),
            full((2 * LANES, 2 * LANES)),
        ],
        out_specs=[
            pl.BlockSpec((tm, A_GROUP_W), lambda i: (i, 0)),
            pl.BlockSpec((tm // 4, 4 * A_GROUP_W), lambda i: (i, 0)),
            pl.BlockSpec((tm // 16, 16 * A_GROUP_W), lambda i: (i, 0)),
            pl.BlockSpec((tm, B_Q_W), lambda i: (i, 0)),
            pl.BlockSpec((tm, B_KV_W), lambda i: (i, 0)),
        ],
        out_shape=[
            jax.ShapeDtypeStruct((n, A_GROUP_W), BF16),
            jax.ShapeDtypeStruct((n // 4, 4 * A_GROUP_W), BF16),
            jax.ShapeDtypeStruct((n // 16, 16 * A_GROUP_W), BF16),
            jax.ShapeDtypeStruct((n, B_Q_W), BF16),
            jax.ShapeDtypeStruct((n, B_KV_W), BF16),
        ],
        scratch_shapes=[pltpu.VMEM((A_GROUP_W // LANES, tm, LANES), F32)],
        compiler_params=_cparams(("arbitrary",)),
        name="proj_qkv",
    )(x2d, g1, wqkv, tabs, bd)


def _attn_a_kernel(a0_ref, a1_ref, a2_ref, bw_ref, bn_ref, ya_ref, acc_ref, m_ref, l_ref):
    tq = 128
    lane = lax.broadcasted_iota(I32, (tq, LANES), 1)
    low = lane < HEAD_DIM

    def tile(g, ref, col0, length, i, rows):
        win = min(2 * tq, length)
        if isinstance(i, int):
            q0 = i * tq
            ws = max(0, min(q0 - RADIUS, length - win))
        else:
            q0 = pl.multiple_of(i * tq, tq)
            ws = pl.multiple_of(jnp.clip(q0 - RADIUS, 0, length - win), RADIUS)
        if win == length:
            bias = bn_ref[...]
        elif isinstance(i, int):
            bias = bw_ref[(q0 - ws) // RADIUS]
        else:
            bias = bw_ref[lax.shift_right_logical(q0 - ws, 6)]
        for hp in range(2):
            q2 = ref[0, pl.ds(q0, tq), col0 + hp * LANES:col0 + (hp + 1) * LANES]
            k2 = ref[0, pl.ds(ws, win), col0 + 256 + hp * LANES:col0 + 256 + (hp + 1) * LANES]
            v2 = ref[0, pl.ds(ws, win), col0 + 512 + hp * LANES:col0 + 512 + (hp + 1) * LANES]
            zero = jnp.zeros_like(q2)
            qs = jnp.concatenate([jnp.where(low, q2, zero), jnp.where(low, zero, q2)], axis=0)
            s = lax.dot_general(qs, k2, (((1,), (1,)), ((), ())), preferred_element_type=F32)
            s = s + bias
            m = jnp.max(s, axis=-1, keepdims=True)
            p = jnp.exp2(s - m)
            l = jnp.sum(p, axis=-1, keepdims=True)
            pv = jnp.dot(p.astype(BF16), v2, preferred_element_type=F32)
            slot = 2 * g + hp
            acc_ref[slot, rows, :] = jnp.where(low, pv[:tq], pv[tq:])
            m_ref[slot, rows, :] = jnp.where(low, m[:tq], m[tq:])
            l_ref[slot, rows, :] = jnp.where(low, l[:tq], l[tq:])

    seq = a0_ref.shape[1]

    def g0_body(i, carry):
        tile(0, a0_ref, 0, seq, i, pl.ds(pl.multiple_of(i * tq, tq), tq))
        return carry
    lax.fori_loop(0, seq // tq, g0_body, 0, unroll=4)

    len1 = seq // 4
    for res in range(4):
        def g1_body(i, carry, res=res):
            tile(1, a1_ref, res * A_GROUP_W, len1, i, pl.ds(res + 4 * i * tq, tq, stride=4))
            return carry
        lax.fori_loop(0, len1 // tq, g1_body, 0, unroll=4)

    len2 = seq // 16
    for res in range(16):
        tile(2, a2_ref, res * A_GROUP_W, len2, 0, pl.ds(res, tq, stride=16))

    rc = 256

    def merge_body(j, carry):
        rows = pl.ds(pl.multiple_of(j * rc, rc), rc)
        for hp in range(2):
            ms = [m_ref[2 * g + hp, rows, :] for g in range(N_GROUPS_A)]
            mx = jnp.maximum(jnp.maximum(ms[0], ms[1]), ms[2])
            num = jnp.zeros((rc, LANES), F32)
            den = jnp.zeros((rc, LANES), F32)
            for g in range(N_GROUPS_A):
                e = jnp.exp2(ms[g] - mx)
                num = num + e * acc_ref[2 * g + hp, rows, :]
                den = den + e * l_ref[2 * g + hp, rows, :]
            ya_ref[0, rows, hp * LANES:(hp + 1) * LANES] = (num / den).astype(BF16)
        return carry
    lax.fori_loop(0, seq // rc, merge_body, 0)


def _band_bias(offset, win):
    q = jnp.arange(2 * 128) % 128 + offset
    k = jnp.arange(win)
    return jnp.where(jnp.abs(q[:, None] - k[None, :]) <= RADIUS, 0.0, NEG_BIG).astype(F32)


def _attn_a_call(a0, a1, a2):
    b, seq, _ = a0.shape
    bias_wide = jnp.stack([_band_bias(off, 256) for off in (0, RADIUS, 2 * RADIUS)])
    bias_narrow = _band_bias(0, 128)
    return pl.pallas_call(
        _attn_a_kernel,
        grid=(b,),
        in_specs=[
            pl.BlockSpec((1, seq, A_GROUP_W), lambda i: (i, 0, 0)),
            pl.BlockSpec((1, seq // 4, 4 * A_GROUP_W), lambda i: (i, 0, 0)),
            pl.BlockSpec((1, seq // 16, 16 * A_GROUP_W), lambda i: (i, 0, 0)),
            pl.BlockSpec((3, 256, 256), lambda i: (0, 0, 0)),
            pl.BlockSpec((256, 128), lambda i: (0, 0)),
        ],
        out_specs=pl.BlockSpec((1, seq, A_OUT_W), lambda i: (i, 0, 0)),
        out_shape=jax.ShapeDtypeStruct((b, seq, A_OUT_W), BF16),
        scratch_shapes=[pltpu.VMEM((N_GROUPS_A * A_OUT_W // LANES, seq, LANES), F32)] * 3,
        compiler_params=_cparams(("arbitrary",)),
        name="attn_a",
    )(a0, a1, a2, bias_wide, bias_narrow)


def _attn_b_kernel(q_ref, kv_ref, o_ref, va_ref, vb_ref):
    tq = q_ref.shape[1]
    seq = kv_ref.shape[1]
    low = lax.broadcasted_iota(I32, (tq, LANES), 1) < HEAD_DIM

    @pl.when(pl.program_id(1) == 0)
    def _():
        low_k = lax.broadcasted_iota(I32, (seq, LANES), 1) < HEAD_DIM
        v2 = kv_ref[0, :, LANES:2 * LANES]
        one = jnp.ones_like(v2)
        va_ref[...] = jnp.where(low_k, v2, one)
        vb_ref[...] = jnp.where(low_k, one, v2)

    for pr in range(B_Q_W // LANES):
        q2 = q_ref[0, :, pr * LANES:(pr + 1) * LANES]
        zero = jnp.zeros_like(q2)
        qs = jnp.concatenate([jnp.where(low, q2, zero), jnp.where(low, zero, q2)], axis=0)
        m = jnp.full((2 * tq, 1), NEG_BIG, F32)
        acc_a = jnp.zeros((tq, LANES), F32)
        acc_b = jnp.zeros((tq, LANES), F32)
        for c in range(seq // TK_B):
            k2 = kv_ref[0, c * TK_B:(c + 1) * TK_B, 0:LANES]
            va = va_ref[c * TK_B:(c + 1) * TK_B, :]
            vb = vb_ref[c * TK_B:(c + 1) * TK_B, :]
            s = lax.dot_general(qs, k2, (((1,), (1,)), ((), ())), preferred_element_type=F32)
            m_n = jnp.maximum(m, jnp.max(s, axis=-1, keepdims=True))
            p = jnp.exp2(s - m_n).astype(BF16)
            a = jnp.exp2(m - m_n)
            acc_a = a[:tq] * acc_a + jnp.dot(p[:tq], va, preferred_element_type=F32)
            acc_b = a[tq:] * acc_b + jnp.dot(p[tq:], vb, preferred_element_type=F32)
            m = m_n
        oa = acc_a / pltpu.roll(acc_a, HEAD_DIM, 1)
        ob = acc_b / pltpu.roll(acc_b, HEAD_DIM, 1)
        o_ref[0, :, pr * LANES:(pr + 1) * LANES] = jnp.where(low, oa, ob).astype(BF16)


def _attn_b_call(qb, kvb):
    b, seq, _ = qb.shape
    return pl.pallas_call(
        _attn_b_kernel,
        grid=(b, seq // TQ_B),
        in_specs=[
            pl.BlockSpec((1, TQ_B, B_Q_W), lambda i, j: (i, j, 0)),
            pl.BlockSpec((1, seq, B_KV_W), lambda i, j: (i, 0, 0)),
        ],
        out_specs=pl.BlockSpec((1, TQ_B, B_Q_W), lambda i, j: (i, j, 0)),
        out_shape=jax.ShapeDtypeStruct((b, seq, B_Q_W), BF16),
        scratch_shapes=[pltpu.VMEM((seq, LANES), BF16)] * 2,
        compiler_params=_cparams(("arbitrary",) * 2),
        name="attn_b",
    )(qb, kvb)


def _pack_bf16_pairs(v):
    half = v.shape[1] // 2
    lo = lax.bitcast_convert_type(v[:, :half].astype(BF16).astype(F32), U32)
    hi = lax.bitcast_convert_type(v[:, half:].astype(BF16).astype(F32), U32)
    return (lo >> 16) | (hi & jnp.uint32(0xFFFF0000))


def _unpack_bf16_pairs(p):
    lo = lax.bitcast_convert_type(p << 16, F32)
    hi = lax.bitcast_convert_type(p & jnp.uint32(0xFFFF0000), F32)
    return lo, hi


def _merge_kernel(x_ref, ya_ref, yb_ref, g1_ref, wg_ref, wba_ref, wbb_ref, wo_ref, g2_ref,
                  wrh_ref, wrl_ref, bias_ref, wsgu_ref, wsd_ref, tri_ref,
                  base_ref, xp_ref, eidx_ref, rank_ref, wgt_ref, cnt_ref, carry_ref):
    step = pl.program_id(0)

    @pl.when(step == 0)
    def _():
        carry_ref[...] = jnp.zeros_like(carry_ref)

    x = x_ref[...]
    tm = x.shape[0]
    ms = jnp.mean(x * x, axis=-1, keepdims=True)
    xn = (x * lax.rsqrt(ms + EPS) * g1_ref[...]).astype(BF16)
    gates = jax.nn.sigmoid(jnp.dot(xn, wg_ref[...], preferred_element_type=F32))
    pa = jnp.dot(ya_ref[...], wba_ref[...], preferred_element_type=F32)
    pb = jnp.dot(yb_ref[...], wbb_ref[...], preferred_element_type=F32)
    merged = gates[:, :D_MODEL] * pa + gates[:, D_MODEL:] * pb
    h = x + jnp.dot(merged.astype(BF16), wo_ref[...], preferred_element_type=F32)

    ms2 = jnp.mean(h * h, axis=-1, keepdims=True)
    xn2 = h * lax.rsqrt(ms2 + EPS) * g2_ref[...]
    xn2b = xn2.astype(BF16)
    xp_ref[...] = _pack_bf16_pairs(xn2)

    gu = jnp.dot(xn2b, wsgu_ref[...], preferred_element_type=F32)
    hs = (jax.nn.silu(gu[:, :EXPERT_FF]) * gu[:, EXPERT_FF:]).astype(BF16)
    base_ref[...] = h + jnp.dot(hs, wsd_ref[...], preferred_element_type=F32)

    xlo = (xn2 - xn2b.astype(F32)).astype(BF16)
    logits = (jnp.dot(xn2b, wrh_ref[...], preferred_element_type=F32)
              + jnp.dot(xlo, wrh_ref[...], preferred_element_type=F32)
              + jnp.dot(xn2b, wrl_ref[...], preferred_element_type=F32))
    logits_t = logits.T[:N_EXPERTS, :]
    for h in range(tm // ROUTE_TILE):
        cols = slice(h * ROUTE_TILE, (h + 1) * ROUTE_TILE)
        _route_tile(logits_t[:, cols], bias_ref, tri_ref, carry_ref,
                    eidx_ref, rank_ref, wgt_ref, cols)
    cnt_ref[...] = carry_ref[...]


def _route_tile(logits_t, bias_ref, tri_ref, carry_ref, eidx_ref, rank_ref, wgt_ref, cols):
    tm = logits_t.shape[1]
    scores = jax.nn.sigmoid(logits_t)
    sel = scores + bias_ref[...]

    gsz = N_EXPERTS // N_EXPERT_GROUPS
    iota8 = lax.broadcasted_iota(I32, (gsz, tm), 0).astype(F32)
    ninf = jnp.float32(-jnp.inf)
    blocks, gs = [], []
    for g in range(N_EXPERT_GROUPS):
        blk = sel[g * gsz:(g + 1) * gsz, :]
        m1 = jnp.max(blk, axis=0, keepdims=True)
        first = jnp.min(jnp.where(blk == m1, iota8, float(gsz)), axis=0, keepdims=True)
        m2 = jnp.max(jnp.where(iota8 == first, ninf, blk), axis=0, keepdims=True)
        blocks.append(blk)
        gs.append(m1 + m2)
    cur = jnp.concatenate(gs, axis=0)
    gmask = jnp.zeros((N_EXPERT_GROUPS, tm), F32)
    for _ in range(TOPK_GROUPS):
        mx = jnp.max(cur, axis=0, keepdims=True)
        fi = jnp.min(jnp.where(cur == mx, iota8, float(N_EXPERT_GROUPS)), axis=0, keepdims=True)
        pick = iota8 == fi
        gmask = jnp.where(pick, 1.0, gmask)
        cur = jnp.where(pick, ninf, cur)
    cur = jnp.concatenate(
        [jnp.where(gmask[g:g + 1, :] > 0.5, blocks[g], ninf) for g in range(N_EXPERT_GROUPS)], axis=0)

    iota64 = lax.broadcasted_iota(I32, (N_EXPERTS, tm), 0).astype(F32)
    idxs, wks = [], []
    onehot = jnp.zeros((N_EXPERTS, tm), F32)
    for _ in range(TOP_K):
        mx = jnp.max(cur, axis=0, keepdims=True)
        fi = jnp.min(jnp.where(cur == mx, iota64, float(N_EXPERTS)), axis=0, keepdims=True)
        pick = iota64 == fi
        wks.append(jnp.sum(jnp.where(pick, scores, 0.0), axis=0, keepdims=True))
        idxs.append(fi)
        onehot = jnp.where(pick, 1.0, onehot)
        cur = jnp.where(pick, ninf, cur)
    wk = jnp.concatenate(wks, axis=0)
    wgt_ref[:, cols] = wk / jnp.sum(wk, axis=0, keepdims=True) * ROUTED_SCALE
    eidx_ref[:, cols] = jnp.concatenate(idxs, axis=0).astype(I32)

    before = jnp.dot(onehot.astype(BF16), tri_ref[...], preferred_element_type=F32) + carry_ref[:, 0:1]
    rank_ref[:, cols] = jnp.concatenate(
        [jnp.sum(jnp.where(iota64 == fi, before, 0.0), axis=0, keepdims=True) for fi in idxs],
        axis=0).astype(I32)
    carry_ref[...] = carry_ref[...] + jnp.sum(onehot, axis=1, keepdims=True)


def _merge_call(x2d, ya, yb, g1, wg, wba, wbb, wo, g2, wrh, wrl, bias, wsgu, wsd, tri, row0, n):
    tm = TM_MERGE
    off = row0 // tm
    full = lambda shape: pl.BlockSpec(shape, lambda i: (0,) * len(shape))
    row_in = lambda w: pl.BlockSpec((tm, w), lambda i: (i + off, 0))
    row = lambda w: pl.BlockSpec((tm, w), lambda i: (i, 0))
    col = lambda: pl.BlockSpec((TOP_K, tm), lambda i: (0, i))
    return pl.pallas_call(
        _merge_kernel,
        grid=(n // tm,),
        in_specs=[
            row_in(D_MODEL), row_in(A_OUT_W), row_in(B_Q_W),
            full((1, D_MODEL)), full((D_MODEL, 2 * D_MODEL)),
            full((A_OUT_W, D_MODEL)), full((B_Q_W, D_MODEL)), full((D_MODEL, D_MODEL)),
            full((1, D_MODEL)),
            full((D_MODEL, LANES)), full((D_MODEL, LANES)), full((N_EXPERTS, 1)),
            full((D_MODEL, 2 * EXPERT_FF)), full((EXPERT_FF, D_MODEL)),
            full((ROUTE_TILE, ROUTE_TILE)),
        ],
        out_specs=[row(D_MODEL), row(D_MODEL // 2), col(), col(), col(), full((N_EXPERTS, LANES))],
        out_shape=[
            jax.ShapeDtypeStruct((n, D_MODEL), F32),
            jax.ShapeDtypeStruct((n, D_MODEL // 2), U32),
            jax.ShapeDtypeStruct((TOP_K, n), I32),
            jax.ShapeDtypeStruct((TOP_K, n), I32),
            jax.ShapeDtypeStruct((TOP_K, n), F32),
            jax.ShapeDtypeStruct((N_EXPERTS, LANES), F32),
        ],
        scratch_shapes=[pltpu.VMEM((N_EXPERTS, LANES), F32)],
        compiler_params=_cparams(("arbitrary",)),
        name="merge_router",
    )(x2d, ya, yb, g1, wg, wba, wbb, wo, g2, wrh, wrl, bias, wsgu, wsd, tri)


EXPERT_SUBBLOCKS = 4


X_SLOTS = 3
Y_SLOTS = 2


def _expert_kernel(be_ref, nb_ref, xs_hbm, *refs):
    nw = 3 * EXPERT_SUBBLOCKS
    w_refs = refs[:nw]
    ys_hbm, xbuf, ybuf, xsem, ysem = refs[nw:]
    j = pl.program_id(0)
    nsteps = pl.num_programs(0)
    rows = EXPERT_SUBBLOCKS * BM_EXPERT

    def x_copy(step, slot):
        src = xs_hbm.at[pl.ds(pl.multiple_of(step * rows, rows), rows)]
        return pltpu.make_async_copy(src, xbuf.at[slot], xsem.at[slot])

    def y_copy(step, slot):
        dst = ys_hbm.at[pl.ds(pl.multiple_of(step * rows, rows), rows)]
        return pltpu.make_async_copy(ybuf.at[slot], dst, ysem.at[slot])

    @pl.when(j == 0)
    def _():
        x_copy(0, 0).start()
        x_copy(1, 1).start()

    @pl.when(j + 2 < nsteps)
    def _():
        x_copy(j + 2, lax.rem(j + 2, X_SLOTS)).start()

    xslot = lax.rem(j, X_SLOTS)
    yslot = lax.rem(j, Y_SLOTS)
    x_copy(j, xslot).wait()

    @pl.when(j >= Y_SLOTS)
    def _():
        y_copy(j - Y_SLOTS, yslot).wait()

    first = j * EXPERT_SUBBLOCKS

    @pl.when(first >= nb_ref[0])
    def _():
        ybuf[yslot] = jnp.zeros((rows, D_MODEL // 2), U32)

    @pl.when(first < nb_ref[0])
    def _():
        half = D_MODEL // 2
        for sub in range(EXPERT_SUBBLOCKS):
            wg_ref, wu_ref, wd_ref = w_refs[3 * sub:3 * sub + 3]
            blk = slice(sub * BM_EXPERT, (sub + 1) * BM_EXPERT)
            lo, hi = _unpack_bf16_pairs(xbuf[xslot, blk, :])
            lo = lo.astype(BF16)
            hi = hi.astype(BF16)
            gate = (jnp.dot(lo, wg_ref[0, :half, :], preferred_element_type=F32)
                    + jnp.dot(hi, wg_ref[0, half:, :], preferred_element_type=F32))
            up = (jnp.dot(lo, wu_ref[0, :half, :], preferred_element_type=F32)
                  + jnp.dot(hi, wu_ref[0, half:, :], preferred_element_type=F32))
            hb = (jax.nn.silu(gate) * up).astype(BF16)
            ybuf[yslot, blk, :] = _pack_bf16_pairs(jnp.dot(hb, wd_ref[0], preferred_element_type=F32))

    y_copy(j, yslot).start()

    @pl.when(j == nsteps - 1)
    def _():
        y_copy(j - 1, lax.rem(j - 1, Y_SLOTS)).wait()
        y_copy(j, yslot).wait()


def _expert_call(block_e, n_used, xs, wg, wu, wd):
    p_len = xs.shape[0]
    nsub = EXPERT_SUBBLOCKS
    rows = nsub * BM_EXPERT
    w_specs, w_args = [], []
    for sub in range(nsub):
        pick = lambda j, be, nb, sub=sub: (be[nsub * j + sub], 0, 0)
        w_specs += [
            pl.BlockSpec((1, D_MODEL, EXPERT_FF), pick),
            pl.BlockSpec((1, D_MODEL, EXPERT_FF), pick),
            pl.BlockSpec((1, EXPERT_FF, D_MODEL), pick),
        ]
        w_args += [wg, wu, wd]
    assert p_len // rows >= max(X_SLOTS, Y_SLOTS)
    return pl.pallas_call(
        _expert_kernel,
        grid_spec=pltpu.PrefetchScalarGridSpec(
            num_scalar_prefetch=2,
            grid=(p_len // rows,),
            in_specs=[pl.BlockSpec(memory_space=pl.ANY)] + w_specs,
            out_specs=pl.BlockSpec(memory_space=pl.ANY),
            scratch_shapes=[
                pltpu.VMEM((X_SLOTS, rows, D_MODEL // 2), U32),
                pltpu.VMEM((Y_SLOTS, rows, D_MODEL // 2), U32),
                pltpu.SemaphoreType.DMA((X_SLOTS,)),
                pltpu.SemaphoreType.DMA((Y_SLOTS,)),
            ],
        ),
        out_shape=jax.ShapeDtypeStruct((p_len, D_MODEL // 2), U32),
        compiler_params=_cparams(("arbitrary",)),
        name="routed_experts",
    )(block_e, n_used, xs, *w_args)


SC_CORES = 2
SC_SUBCORES = 16
SC_WORKERS = SC_CORES * SC_SUBCORES
SC_CHUNK = 128


def _sc_mesh():
    return plsc.VectorSubcoreMesh(core_axis_name="c", subcore_axis_name="s",
                                  num_cores=SC_CORES, num_subcores=SC_SUBCORES)


def _dispatch_rows(xp, dest3, p_len):
    n, width = xp.shape
    n_chunks = dest3.shape[0]
    per_worker = n_chunks // SC_WORKERS

    @functools.partial(
        pl.kernel, mesh=_sc_mesh(),
        out_type=jax.ShapeDtypeStruct((p_len, width), xp.dtype),
        scratch_types=[pltpu.VMEM((TOP_K, SC_CHUNK), I32), pltpu.VMEM((SC_CHUNK, width), xp.dtype),
                       pltpu.SemaphoreType.DMA],
        name="sc_dispatch")
    def body(xp_hbm, dest_hbm, xs_hbm, idx_v, rows_v, sem):
        wid = lax.axis_index("s") * SC_CORES + lax.axis_index("c")

        @pl.loop(0, per_worker)
        def _(j):
            chunk = wid * per_worker + j
            pltpu.sync_copy(dest_hbm.at[chunk], idx_v)
            pltpu.sync_copy(xp_hbm.at[pl.ds(chunk * SC_CHUNK, SC_CHUNK)], rows_v)
            copies = [pltpu.async_copy(rows_v, xs_hbm.at[idx_v.at[k]], sem) for k in range(TOP_K)]
            for cp in copies:
                cp.wait()

    return body(xp, dest3)


SC_GATHER_CHUNK = 64


def _gather_rows(ys, dest3):
    width = ys.shape[1]
    n_chunks, _, gc = dest3.shape
    n = n_chunks * gc
    per_worker = n_chunks // SC_WORKERS

    @functools.partial(
        pl.kernel, mesh=_sc_mesh(),
        out_type=jax.ShapeDtypeStruct((TOP_K, n, width), ys.dtype),
        scratch_types=[pltpu.VMEM((TOP_K, gc), I32),
                       pltpu.VMEM((gc, width), ys.dtype), pltpu.VMEM((gc, width), ys.dtype),
                       pltpu.SemaphoreType.DMA, pltpu.SemaphoreType.DMA,
                       pltpu.SemaphoreType.DMA, pltpu.SemaphoreType.DMA],
        name="sc_gather")
    def body(ys_hbm, dest_hbm, out_hbm, idx_v, buf0, buf1, gsem0, gsem1, wsem0, wsem1):
        wid = lax.axis_index("s") * SC_CORES + lax.axis_index("c")
        bufs, gsems, wsems = (buf0, buf1), (gsem0, gsem1), (wsem0, wsem1)

        @pl.loop(0, per_worker)
        def _(j):
            chunk = wid * per_worker + j
            pltpu.sync_copy(dest_hbm.at[chunk], idx_v)
            rows = pl.ds(chunk * gc, gc)
            gathers = [None] * TOP_K
            writes = [None] * TOP_K
            gathers[0] = pltpu.async_copy(ys_hbm.at[idx_v.at[0]], bufs[0], gsems[0])
            for k in range(TOP_K):
                gathers[k].wait()
                if k >= 1:
                    writes[k - 1].wait()
                if k + 1 < TOP_K:
                    nxt = (k + 1) % 2
                    gathers[k + 1] = pltpu.async_copy(ys_hbm.at[idx_v.at[k + 1]], bufs[nxt], gsems[nxt])
                writes[k] = pltpu.async_copy(bufs[k % 2], out_hbm.at[k, rows], wsems[k % 2])
            writes[TOP_K - 1].wait()

    return body(ys, dest3)


def _combine_kernel(base_ref, g_ref, w_ref, *refs):
    o_ref = refs[-1]
    half = D_MODEL // 2
    lo_acc = base_ref[:, :half]
    hi_acc = base_ref[:, half:]
    for k in range(TOP_K):
        lo, hi = _unpack_bf16_pairs(g_ref[k])
        wk = w_ref[:, k:k + 1]
        lo_acc = lo_acc + wk * lo
        hi_acc = hi_acc + wk * hi
    o_ref[:, :half] = lo_acc
    o_ref[:, half:] = hi_acc


def _combine_call(base, gathered, w_t, prev, row0, n_total):
    n = base.shape[0]
    tm = TM_COMBINE
    off = row0 // tm
    in_specs = [
        pl.BlockSpec((tm, D_MODEL), lambda i: (i, 0)),
        pl.BlockSpec((TOP_K, tm, D_MODEL // 2), lambda i: (0, i, 0)),
        pl.BlockSpec((tm, TOP_K), lambda i: (i, 0)),
    ]
    args = [base, gathered, w_t]
    aliases = {}
    if prev is not None:
        in_specs.append(pl.BlockSpec(memory_space=pl.ANY))
        args.append(prev)
        aliases = {3: 0}
    return pl.pallas_call(
        _combine_kernel,
        grid=(n // tm,),
        in_specs=in_specs,
        out_specs=pl.BlockSpec((tm, D_MODEL), lambda i: (i + off, 0)),
        out_shape=jax.ShapeDtypeStruct((n_total, D_MODEL), F32),
        input_output_aliases=aliases,
        compiler_params=_cparams(("arbitrary",)),
        name="moe_combine",
    )(*args)


SC_LANES = 16
SC_REDUCE_CHUNK = 8


def _reduce_rows(ys, dest3, wb):
    width = ys.shape[1]
    tc = SC_REDUCE_CHUNK
    n_chunks = dest3.shape[0]
    n = n_chunks * tc
    per_worker = n_chunks // SC_WORKERS
    assert per_worker % 2 == 0 and dest3.shape[1] == TOP_K * tc

    @functools.partial(
        pl.kernel, mesh=_sc_mesh(),
        out_type=jax.ShapeDtypeStruct((n, width), ys.dtype),
        scratch_types=[pltpu.VMEM((per_worker, TOP_K * tc), I32),
                       pltpu.VMEM((2, TOP_K, tc, width), ys.dtype),
                       pltpu.VMEM((2, tc * TOP_K // SC_LANES, SC_LANES), ys.dtype),
                       pltpu.VMEM((2, tc, width), ys.dtype),
                       pltpu.SemaphoreType.DMA, pltpu.SemaphoreType.DMA,
                       pltpu.SemaphoreType.DMA, pltpu.SemaphoreType.DMA],
        compiler_params=pltpu.CompilerParams(needs_layout_passes=False),
        name="sc_reduce")
    def body(ys_hbm, dest_hbm, wb_hbm, out_hbm, idx_v, rows_v, w_v, r_v, gsem0, gsem1, osem0, osem1):
        wid = lax.axis_index("s") * SC_CORES + lax.axis_index("c")
        first = wid * per_worker
        gsems, osems = (gsem0, gsem1), (osem0, osem1)
        pltpu.sync_copy(dest_hbm.at[pl.ds(first, per_worker)], idx_v)

        def fetches(j, slot):
            cps = [pltpu.make_async_copy(ys_hbm.at[idx_v.at[j, pl.ds(k * tc, tc)]], rows_v.at[slot, k],
                                         gsems[slot])
                   for k in range(TOP_K)]
            wrows = tc * TOP_K // SC_LANES
            cps.append(pltpu.make_async_copy(wb_hbm.at[pl.ds((first + j) * wrows, wrows)], w_v.at[slot],
                                             gsems[slot]))
            return cps

        def write_back(j, slot):
            return pltpu.make_async_copy(r_v.at[slot], out_hbm.at[pl.ds((first + j) * tc, tc)],
                                         osems[slot])

        def reduce_chunk(slot):
            @pl.loop(0, tc)
            def _(t):
                wrow = w_v[slot, lax.shift_right_logical(t, 1), :]
                lane0 = (t & 1) * TOP_K
                ws = [plsc.bitcast(
                    wrow.at[jnp.full((SC_LANES,), lane0 + k, I32)].get(mode="promise_in_bounds"), BF16)
                    for k in range(TOP_K)]

                @pl.loop(0, width // SC_LANES)
                def _(jv):
                    col = pl.multiple_of(jv * SC_LANES, SC_LANES)
                    terms = [plsc.bitcast(rows_v[slot, k, t, pl.ds(col, SC_LANES)], BF16) * ws[k]
                             for k in range(TOP_K)]
                    while len(terms) > 1:
                        terms = [terms[i] + terms[i + 1] for i in range(0, len(terms), 2)]
                    r_v[slot, t, pl.ds(col, SC_LANES)] = plsc.bitcast(terms[0], ys.dtype)

        for cp in fetches(0, 0):
            cp.start()

        @pl.loop(0, per_worker // 2)
        def _(p):
            for slot in range(2):
                j = 2 * p + slot
                @pl.when(j + 1 < per_worker)
                def _():
                    for cp in fetches(j + 1, 1 - slot):
                        cp.start()
                for cp in fetches(j, slot):
                    cp.wait()

                @pl.when(p > 0)
                def _():
                    write_back(j - 2, slot).wait()
                reduce_chunk(slot)
                write_back(j, slot).start()

        write_back(per_worker - 2, 0).wait()
        write_back(per_worker - 1, 1).wait()

    return body(ys, dest3, wb)


def _final_add_kernel(base_ref, r_ref, *refs):
    o_ref = refs[-1]
    half = D_MODEL // 2
    lo, hi = _unpack_bf16_pairs(r_ref[...])
    o_ref[:, :half] = base_ref[:, :half] + lo
    o_ref[:, half:] = base_ref[:, half:] + hi


def _final_add_call(base, routed, prev, row0, n_total):
    n = base.shape[0]
    tm = TM_COMBINE
    off = row0 // tm
    in_specs = [pl.BlockSpec((tm, D_MODEL), lambda i: (i, 0)),
                pl.BlockSpec((tm, D_MODEL // 2), lambda i: (i, 0))]
    args, aliases = [base, routed], {}
    if prev is not None:
        in_specs.append(pl.BlockSpec(memory_space=pl.ANY))
        args.append(prev)
        aliases = {2: 0}
    return pl.pallas_call(
        _final_add_kernel,
        grid=(n // tm,),
        in_specs=in_specs,
        out_specs=pl.BlockSpec((tm, D_MODEL), lambda i: (i + off, 0)),
        out_shape=jax.ShapeDtypeStruct((n_total, D_MODEL), F32),
        input_output_aliases=aliases,
        compiler_params=_cparams(("arbitrary",)),
        name="moe_final_add",
    )(*args)


def _fold_gain(tab, gain, half):
    g = jnp.tile(gain.astype(F32), LANES // HEAD_DIM)
    return jnp.stack([tab[0] * g, tab[1] * jnp.roll(g, LANES - half), tab[2] * jnp.roll(g, half)])


def _prep_tables(seq):
    pos = jnp.arange(seq)
    cos1, sin1 = _rope_tables(pos, HEAD_DIM)
    cos_r, sin_r = _rope_tables(pos // GRID_W, HEAD_DIM // 2)
    cos_c, sin_c = _rope_tables(pos % GRID_W, HEAD_DIM // 2)
    taba = _lane_tables(cos1, sin1, HEAD_DIM // 2)
    tabb = _lane_tables(jnp.concatenate([cos_r, cos_c], -1), jnp.concatenate([sin_r, sin_c], -1),
                        HEAD_DIM // 4)
    return taba, tabb


def _layer(h, p, taba, tabb):
    b, seq, d = h.shape
    n = b * seq
    x2d = h.reshape(n, d)
    w_in = p["w_in"]
    o1 = N_GROUPS_A * A_GROUP_W
    o2 = o1 + B_Q_W
    o3 = o2 + B_KV_W
    pair_heads = jnp.array([0, 4, 1, 5, 2, 6, 3, 7])
    pair_cols = (pair_heads[:, None] * HEAD_DIM + jnp.arange(HEAD_DIM)[None, :]).reshape(-1)
    wqkv = jnp.concatenate([w_in[:, :o1], w_in[:, o1:o2][:, pair_cols], w_in[:, o2:o3]], axis=1).astype(BF16)
    tabs = jnp.stack([
        _fold_gain(taba, p["q_norm_a"] * Q_SCALE, HEAD_DIM // 2),
        _fold_gain(taba, p["k_norm_a"], HEAD_DIM // 2),
        _fold_gain(tabb, p["q_norm_b"] * Q_SCALE, HEAD_DIM // 4),
        _fold_gain(tabb, p["k_norm_b"], HEAD_DIM // 4),
    ])
    seg = jnp.arange(2 * LANES) // HEAD_DIM
    bd = jnp.where(seg[:, None] == seg[None, :], 1.0 / HEAD_DIM, 0.0).astype(BF16)
    g1 = p["norm1_g"].reshape(1, d).astype(F32)

    a0, a1, a2, qb, kvb = _proj_call(x2d, g1, wqkv, tabs, bd, seq)
    ya = _attn_a_call(a0.reshape(b, seq, A_GROUP_W), a1.reshape(b, seq // 4, 4 * A_GROUP_W),
                      a2.reshape(b, seq // 16, 16 * A_GROUP_W))
    yb = _attn_b_call(qb.reshape(b, seq, B_Q_W), kvb.reshape(b, seq, B_KV_W))

    wg = w_in[:, o3:].astype(BF16)
    wbb = p["w_branch_b"][pair_cols, :].astype(BF16)
    wr = jnp.pad(p["w_router"], ((0, 0), (0, LANES - N_EXPERTS)))
    wrh = wr.astype(BF16)
    wrl = (wr - wrh.astype(F32)).astype(BF16)
    wsgu = jnp.concatenate([p["ws_gate"], p["ws_up"]], axis=1).astype(BF16)
    tri = (jnp.arange(ROUTE_TILE)[:, None] < jnp.arange(ROUTE_TILE)[None, :]).astype(BF16)
    wba = p["w_branch_a"].astype(BF16)
    wo = p["w_out"].astype(BF16)
    g2 = p["norm2_g"].reshape(1, d).astype(F32)
    bias = p["router_bias"].reshape(N_EXPERTS, 1).astype(F32)
    wsd = p["ws_down"].astype(BF16)
    weg = p["we_gate"].astype(BF16)
    weu = p["we_up"].astype(BF16)
    wd = p["we_down"].astype(BF16)
    ya2d = ya.reshape(n, A_OUT_W)
    yb2d = yb.reshape(n, B_Q_W)

    nc = n // MOE_CHUNKS
    bm = BM_EXPERT
    n_blocks = (nc * TOP_K + N_EXPERTS * (bm - 1)) // bm + 1
    n_blocks = -(-n_blocks // EXPERT_SUBBLOCKS) * EXPERT_SUBBLOCKS
    p_len = n_blocks * bm
    block_start = jnp.arange(n_blocks, dtype=I32) * bm
    out = None
    for c in range(MOE_CHUNKS):
        base, xp, eidx, rank, wgt, cnt = _merge_call(
            x2d, ya2d, yb2d, g1, wg, wba, wbb, wo, g2, wrh, wrl, bias, wsgu, wsd, tri, c * nc, nc)
        counts = cnt[:, 0].astype(I32)
        padded = (counts + bm - 1) // bm * bm
        pend = jnp.cumsum(padded)
        pstart = pend - padded
        onehot = eidx[:, :, None] == jnp.arange(N_EXPERTS)[None, None, :]
        dest = jnp.sum(jnp.where(onehot, pstart[None, None, :], 0), axis=-1) + rank
        block_e = jnp.minimum(jnp.sum(pend[None, :] <= block_start[:, None], axis=1),
                              N_EXPERTS - 1).astype(I32)
        n_used = (pend[-1] // bm).astype(I32).reshape(1)
        dest3 = dest.reshape(TOP_K, nc // SC_CHUNK, SC_CHUNK).transpose(1, 0, 2)
        dest3r = dest.reshape(TOP_K, nc // SC_REDUCE_CHUNK, SC_REDUCE_CHUNK).transpose(1, 0, 2).reshape(
            nc // SC_REDUCE_CHUNK, TOP_K * SC_REDUCE_CHUNK)
        wbits = lax.bitcast_convert_type(wgt.T.astype(BF16), jnp.uint16).astype(U32)
        wb = (wbits | (wbits << 16)).reshape(nc * TOP_K // SC_LANES, SC_LANES)

        xs = _dispatch_rows(xp, dest3, p_len)
        ys = _expert_call(block_e, n_used, xs, weg, weu, wd)
        routed = _reduce_rows(ys, dest3r, wb)
        out = _final_add_call(base, routed, out, c * nc, n)
    return out.reshape(b, seq, d)


def kernel(x, norm1_g, w_in, q_norm_a, k_norm_a, q_norm_b, k_norm_b, w_branch_a, w_branch_b, w_out,
           norm2_g, w_router, router_bias, we_gate, we_up, we_down, ws_gate, ws_up, ws_down):
    params = dict(norm1_g=norm1_g, w_in=w_in, q_norm_a=q_norm_a, k_norm_a=k_norm_a, q_norm_b=q_norm_b,
                  k_norm_b=k_norm_b, w_branch_a=w_branch_a, w_branch_b=w_branch_b, w_out=w_out,
                  norm2_g=norm2_g, w_router=w_router, router_bias=router_bias, we_gate=we_gate,
                  we_up=we_up, we_down=we_down, ws_gate=ws_gate, ws_up=ws_up, ws_down=ws_down)
    taba, tabb = _prep_tables(x.shape[1])
    h = x
    for l in range(norm1_g.shape[0]):
        h = _layer(h, {k: v[l] for k, v in params.items()}, taba, tabb)
    return h
```

```python
import functools

import jax
import jax.numpy as jnp
from jax import lax
from jax.experimental import pallas as pl
from jax.experimental.pallas import tpu as pltpu
from jax.experimental.pallas import tpu_sc as plsc

F32 = jnp.float32
BF16 = jnp.bfloat16
I32 = jnp.int32
U32 = jnp.uint32

D_MODEL = 1024
HEAD_DIM = 64
ROPE_THETA = 10000.0
EPS = 1e-6
GRID_W = 64
DILATIONS = (1, 4, 16)
RADIUS = 64
N_GROUPS_A = 3
A_GROUP_W = 768
A_OUT_W = 256
B_Q_W = 512
B_KV_W = 256
QKV_W = N_GROUPS_A * A_GROUP_W + B_Q_W + B_KV_W
N_EXPERTS = 64
N_EXPERT_GROUPS = 8
TOPK_GROUPS = 4
TOP_K = 8
EXPERT_FF = 256
ROUTED_SCALE = 2.5

LANES = 128
NEG_BIG = -1e30
Q_SCALE = HEAD_DIM ** -0.5 * 1.4426950408889634

TM_PROJ = 1024
TQ_B = 512
TK_B = 512
TM_MERGE = 256
ROUTE_TILE = 256
BM_EXPERT = 512
TM_COMBINE = 512
MOE_CHUNKS = 4
VMEM_LIMIT = 56 * 1024 * 1024


def _cparams(sem):
    return pltpu.CompilerParams(dimension_semantics=sem, vmem_limit_bytes=VMEM_LIMIT)


def _rope_tables(pos, dim):
    inv = ROPE_THETA ** (-jnp.arange(0, dim, 2, dtype=F32) / dim)
    ang = pos.astype(F32)[:, None] * inv[None, :]
    ang = jnp.concatenate([ang, ang], axis=-1)
    return jnp.cos(ang), jnp.sin(ang)


def _lane_tables(cos_h, sin_h, half):
    cos2 = jnp.concatenate([cos_h, cos_h], axis=-1)
    sin2 = jnp.concatenate([sin_h, sin_h], axis=-1)
    first = (jnp.arange(LANES) % (2 * half)) < half
    s_left = jnp.where(first[None, :], -sin2, 0.0)
    s_right = jnp.where(first[None, :], 0.0, sin2)
    return jnp.stack([cos2, s_left, s_right]).astype(F32)


TAB_AQ, TAB_AK, TAB_BQ, TAB_BK = range(4)


def _proj_kernel(x_ref, g1_ref, w_ref, tab_ref, bd_ref,
                 a0_ref, a1_ref, a2_ref, qb_ref, kvb_ref, pbuf_ref):
    x = x_ref[...]
    ms = jnp.mean(x * x, axis=-1, keepdims=True)
    xn = (x * lax.rsqrt(ms + EPS) * g1_ref[...]).astype(BF16)
    tm = x.shape[0]

    def norm_rope(y, t, shift):
        w = y.shape[1]
        ss = jnp.dot((y * y).astype(BF16), bd_ref[0:w, 0:w], preferred_element_type=F32)
        yn = y * lax.rsqrt(ss + EPS)
        out = []
        for i in range(w // LANES):
            z = yn[:, i * LANES:(i + 1) * LANES]
            out.append(z * tab_ref[t, 0] + pltpu.roll(z, LANES - shift, 1) * tab_ref[t, 1]
                       + pltpu.roll(z, shift, 1) * tab_ref[t, 2])
        return out

    for g, d in enumerate(DILATIONS):
        base = g * A_GROUP_W
        r = jnp.dot(xn, w_ref[:, base:base + A_GROUP_W], preferred_element_type=F32)
        chunks = (norm_rope(r[:, 0:2 * LANES], TAB_AQ, HEAD_DIM // 2)
                  + norm_rope(r[:, 2 * LANES:4 * LANES], TAB_AK, HEAD_DIM // 2)
                  + [r[:, 4 * LANES:5 * LANES], r[:, 5 * LANES:6 * LANES]])
        for c, y in enumerate(chunks):
            if d == 1:
                a0_ref[:, c * LANES:(c + 1) * LANES] = y.astype(BF16)
            else:
                pbuf_ref[c] = y
        if d > 1:
            out_ref = a1_ref if g == 1 else a2_ref
            rows = tm // d
            for res in range(d):
                for c in range(A_GROUP_W // LANES):
                    col = res * A_GROUP_W + c * LANES
                    out_ref[:, col:col + LANES] = pbuf_ref[c, pl.ds(res, rows, stride=d), :].astype(BF16)

    base = N_GROUPS_A * A_GROUP_W
    r = jnp.dot(xn, w_ref[:, base:base + B_Q_W + B_KV_W], preferred_element_type=F32)
    chunks = (norm_rope(r[:, 0:2 * LANES], TAB_BQ, HEAD_DIM // 4)
              + norm_rope(r[:, 2 * LANES:4 * LANES], TAB_BQ, HEAD_DIM // 4)
              + norm_rope(r[:, 4 * LANES:5 * LANES], TAB_BK, HEAD_DIM // 4)
              + [r[:, 5 * LANES:6 * LANES]])
    for c, y in enumerate(chunks):
        if c < 4:
            qb_ref[:, c * LANES:(c + 1) * LANES] = y.astype(BF16)
        else:
            kvb_ref[:, (c - 4) * LANES:(c - 3) * LANES] = y.astype(BF16)


def _proj_call(x2d, g1, wqkv, tabs, bd, seq):
    n = x2d.shape[0]
    tm = TM_PROJ
    tiles_per_seq = seq // tm
    full = lambda shape: pl.BlockSpec(shape, lambda i: (0,) * len(shape))
    return pl.pallas_call(
        _proj_kernel,
        grid=(n // tm,),
        in_specs=[
            pl.BlockSpec((tm, D_MODEL), lambda i: (i, 0)),
            full((1, D_MODEL)),
            full((D_MODEL, QKV_W)),
            pl.BlockSpec((4, 3, tm, LANES), lambda i: (0, 0, i % tiles_per_seq, 0)),
            full((2 * LANES, 2 * LANES)),
        ],
        out_specs=[
            pl.BlockSpec((tm, A_GROUP_W), lambda i: (i, 0)),
            pl.BlockSpec((tm // 4, 4 * A_GROUP_W), lambda i: (i, 0)),
            pl.BlockSpec((tm // 16, 16 * A_GROUP_W), lambda i: (i, 0)),
            pl.BlockSpec((tm, B_Q_W), lambda i: (i, 0)),
            pl.BlockSpec((tm, B_KV_W), lambda i: (i, 0)),
        ],
        out_shape=[
            jax.ShapeDtypeStruct((n, A_GROUP_W), BF16),
            jax.ShapeDtypeStruct((n // 4, 4 * A_GROUP_W), BF16),
            jax.ShapeDtypeStruct((n // 16, 16 * A_GROUP_W), BF16),
            jax.ShapeDtypeStruct((n, B_Q_W), BF16),
            jax.ShapeDtypeStruct((n, B_KV_W), BF16),
        ],
        scratch_shapes=[pltpu.VMEM((A_GROUP_W // LANES, tm, LANES), F32)],
        compiler_params=_cparams(("arbitrary",)),
        name="proj_qkv",
    )(x2d, g1, wqkv, tabs, bd)


def _attn_a_kernel(a0_ref, a1_ref, a2_ref, bw_ref, bn_ref, ya_ref, acc_ref, m_ref, l_ref):
    tq = 128
    lane = lax.broadcasted_iota(I32, (tq, LANES), 1)
    low = lane < HEAD_DIM

    def tile(g, ref, col0, length, i, rows):
        win = min(2 * tq, length)
        if isinstance(i, int):
            q0 = i * tq
            ws = max(0, min(q0 - RADIUS, length - win))
        else:
            q0 = pl.multiple_of(i * tq, tq)
            ws = pl.multiple_of(jnp.clip(q0 - RADIUS, 0, length - win), RADIUS)
        if win == length:
            bias = bn_ref[...]
        elif isinstance(i, int):
            bias = bw_ref[(q0 - ws) // RADIUS]
        else:
            bias = bw_ref[lax.shift_right_logical(q0 - ws, 6)]
        for hp in range(2):
            q2 = ref[0, pl.ds(q0, tq), col0 + hp * LANES:col0 + (hp + 1) * LANES]
            k2 = ref[0, pl.ds(ws, win), col0 + 256 + hp * LANES:col0 + 256 + (hp + 1) * LANES]
            v2 = ref[0, pl.ds(ws, win), col0 + 512 + hp * LANES:col0 + 512 + (hp + 1) * LANES]
            zero = jnp.zeros_like(q2)
            qs = jnp.concatenate([jnp.where(low, q2, zero), jnp.where(low, zero, q2)], axis=0)
            s = lax.dot_general(qs, k2, (((1,), (1,)), ((), ())), preferred_element_type=F32)
            s = s + bias
            m = jnp.max(s, axis=-1, keepdims=True)
            p = jnp.exp2(s - m)
            l = jnp.sum(p, axis=-1, keepdims=True)
            pv = jnp.dot(p.astype(BF16), v2, preferred_element_type=F32)
            slot = 2 * g + hp
            acc_ref[slot, rows, :] = jnp.where(low, pv[:tq], pv[tq:])
            m_ref[slot, rows, :] = jnp.where(low, m[:tq], m[tq:])
            l_ref[slot, rows, :] = jnp.where(low, l[:tq], l[tq:])

    seq = a0_ref.shape[1]

    def g0_body(i, carry):
        tile(0, a0_ref, 0, seq, i, pl.ds(pl.multiple_of(i * tq, tq), tq))
        return carry
    lax.fori_loop(0, seq // tq, g0_body, 0, unroll=4)

    len1 = seq // 4
    for res in range(4):
        def g1_body(i, carry, res=res):
            tile(1, a1_ref, res * A_GROUP_W, len1, i, pl.ds(res + 4 * i * tq, tq, stride=4))
            return carry
        lax.fori_loop(0, len1 // tq, g1_body, 0, unroll=4)

    len2 = seq // 16
    for res in range(16):
        tile(2, a2_ref, res * A_GROUP_W, len2, 0, pl.ds(res, tq, stride=16))

    rc = 256

    def merge_body(j, carry):
        rows = pl.ds(pl.multiple_of(j * rc, rc), rc)
        for hp in range(2):
            ms = [m_ref[2 * g + hp, rows, :] for g in range(N_GROUPS_A)]
            mx = jnp.maximum(jnp.maximum(ms[0], ms[1]), ms[2])
            num = jnp.zeros((rc, LANES), F32)
            den = jnp.zeros((rc, LANES), F32)
            for g in range(N_GROUPS_A):
                e = jnp.exp2(ms[g] - mx)
                num = num + e * acc_ref[2 * g + hp, rows, :]
                den = den + e * l_ref[2 * g + hp, rows, :]
            ya_ref[0, rows, hp * LANES:(hp + 1) * LANES] = (num / den).astype(BF16)
        return carry
    lax.fori_loop(0, seq // rc, merge_body, 0)


def _band_bias(offset, win):
    q = jnp.arange(2 * 128) % 128 + offset
    k = jnp.arange(win)
    return jnp.where(jnp.abs(q[:, None] - k[None, :]) <= RADIUS, 0.0, NEG_BIG).astype(F32)


def _attn_a_call(a0, a1, a2):
    b, seq, _ = a0.shape
    bias_wide = jnp.stack([_band_bias(off, 256) for off in (0, RADIUS, 2 * RADIUS)])
    bias_narrow = _band_bias(0, 128)
    return pl.pallas_call(
        _attn_a_kernel,
        grid=(b,),
        in_specs=[
            pl.BlockSpec((1, seq, A_GROUP_W), lambda i: (i, 0, 0)),
            pl.BlockSpec((1, seq // 4, 4 * A_GROUP_W), lambda i: (i, 0, 0)),
            pl.BlockSpec((1, seq // 16, 16 * A_GROUP_W), lambda i: (i, 0, 0)),
            pl.BlockSpec((3, 256, 256), lambda i: (0, 0, 0)),
            pl.BlockSpec((256, 128), lambda i: (0, 0)),
        ],
        out_specs=pl.BlockSpec((1, seq, A_OUT_W), lambda i: (i, 0, 0)),
        out_shape=jax.ShapeDtypeStruct((b, seq, A_OUT_W), BF16),
        scratch_shapes=[pltpu.VMEM((N_GROUPS_A * A_OUT_W // LANES, seq, LANES), F32)] * 3,
        compiler_params=_cparams(("arbitrary",)),
        name="attn_a",
    )(a0, a1, a2, bias_wide, bias_narrow)


def _attn_b_kernel(q_ref, kv_ref, o_ref, va_ref, vb_ref):
    tq = q_ref.shape[1]
    seq = kv_ref.shape[1]
    low = lax.broadcasted_iota(I32, (tq, LANES), 1) < HEAD_DIM

    @pl.when(pl.program_id(1) == 0)
    def _():
        low_k = lax.broadcasted_iota(I32, (seq, LANES), 1) < HEAD_DIM
        v2 = kv_ref[0, :, LANES:2 * LANES]
        one = jnp.ones_like(v2)
        va_ref[...] = jnp.where(low_k, v2, one)
        vb_ref[...] = jnp.where(low_k, one, v2)

    for pr in range(B_Q_W // LANES):
        q2 = q_ref[0, :, pr * LANES:(pr + 1) * LANES]
        zero = jnp.zeros_like(q2)
        qs = jnp.concatenate([jnp.where(low, q2, zero), jnp.where(low, zero, q2)], axis=0)
        m = jnp.full((2 * tq, 1), NEG_BIG, F32)
        acc_a = jnp.zeros((tq, LANES), F32)
        acc_b = jnp.zeros((tq, LANES), F32)
        for c in range(seq // TK_B):
            k2 = kv_ref[0, c * TK_B:(c + 1) * TK_B, 0:LANES]
            va = va_ref[c * TK_B:(c + 1) * TK_B, :]
            vb = vb_ref[c * TK_B:(c + 1) * TK_B, :]
            s = lax.dot_general(qs, k2, (((1,), (1,)), ((), ())), preferred_element_type=F32)
            m_n = jnp.maximum(m, jnp.max(s, axis=-1, keepdims=True))
            p = jnp.exp2(s - m_n).astype(BF16)
            a = jnp.exp2(m - m_n)
            acc_a = a[:tq] * acc_a + jnp.dot(p[:tq], va, preferred_element_type=F32)
            acc_b = a[tq:] * acc_b + jnp.dot(p[tq:], vb, preferred_element_type=F32)
            m = m_n
        oa = acc_a / pltpu.roll(acc_a, HEAD_DIM, 1)
        ob = acc_b / pltpu.roll(acc_b, HEAD_DIM, 1)
        o_ref[0, :, pr * LANES:(pr + 1) * LANES] = jnp.where(low, oa, ob).astype(BF16)


def _attn_b_call(qb, kvb):
    b, seq, _ = qb.shape
    return pl.pallas_call(
        _attn_b_kernel,
        grid=(b, seq // TQ_B),
        in_specs=[
            pl.BlockSpec((1, TQ_B, B_Q_W), lambda i, j: (i, j, 0)),
            pl.BlockSpec((1, seq, B_KV_W), lambda i, j: (i, 0, 0)),
        ],
        out_specs=pl.BlockSpec((1, TQ_B, B_Q_W), lambda i, j: (i, j, 0)),
        out_shape=jax.ShapeDtypeStruct((b, seq, B_Q_W), BF16),
        scratch_shapes=[pltpu.VMEM((seq, LANES), BF16)] * 2,
        compiler_params=_cparams(("arbitrary",) * 2),
        name="attn_b",
    )(qb, kvb)


def _pack_bf16_pairs(v):
    half = v.shape[1] // 2
    lo = lax.bitcast_convert_type(v[:, :half].astype(BF16).astype(F32), U32)
    hi = lax.bitcast_convert_type(v[:, half:].astype(BF16).astype(F32), U32)
    return (lo >> 16) | (hi & jnp.uint32(0xFFFF0000))


def _unpack_bf16_pairs(p):
    lo = lax.bitcast_convert_type(p << 16, F32)
    hi = lax.bitcast_convert_type(p & jnp.uint32(0xFFFF0000), F32)
    return lo, hi


def _merge_kernel(x_ref, ya_ref, yb_ref, g1_ref, wg_ref, wba_ref, wbb_ref, wo_ref, g2_ref,
                  wrh_ref, wrl_ref, bias_ref, wsgu_ref, wsd_ref, tri_ref,
                  base_ref, xp_ref, eidx_ref, rank_ref, wgt_ref, cnt_ref, carry_ref):
    step = pl.program_id(0)

    @pl.when(step == 0)
    def _():
        carry_ref[...] = jnp.zeros_like(carry_ref)

    x = x_ref[...]
    tm = x.shape[0]
    ms = jnp.mean(x * x, axis=-1, keepdims=True)
    xn = (x * lax.rsqrt(ms + EPS) * g1_ref[...]).astype(BF16)
    gates = jax.nn.sigmoid(jnp.dot(xn, wg_ref[...], preferred_element_type=F32))
    pa = jnp.dot(ya_ref[...], wba_ref[...], preferred_element_type=F32)
    pb = jnp.dot(yb_ref[...], wbb_ref[...], preferred_element_type=F32)
    merged = gates[:, :D_MODEL] * pa + gates[:, D_MODEL:] * pb
    h = x + jnp.dot(merged.astype(BF16), wo_ref[...], preferred_element_type=F32)

    ms2 = jnp.mean(h * h, axis=-1, keepdims=True)
    xn2 = h * lax.rsqrt(ms2 + EPS) * g2_ref[...]
    xn2b = xn2.astype(BF16)
    xp_ref[...] = _pack_bf16_pairs(xn2)

    gu = jnp.dot(xn2b, wsgu_ref[...], preferred_element_type=F32)
    hs = (jax.nn.silu(gu[:, :EXPERT_FF]) * gu[:, EXPERT_FF:]).astype(BF16)
    base_ref[...] = h + jnp.dot(hs, wsd_ref[...], preferred_element_type=F32)

    xlo = (xn2 - xn2b.astype(F32)).astype(BF16)
    logits = (jnp.dot(xn2b, wrh_ref[...], preferred_element_type=F32)
              + jnp.dot(xlo, wrh_ref[...], preferred_element_type=F32)
              + jnp.dot(xn2b, wrl_ref[...], preferred_element_type=F32))
    logits_t = logits.T[:N_EXPERTS, :]
    for h in range(tm // ROUTE_TILE):
        cols = slice(h * ROUTE_TILE, (h + 1) * ROUTE_TILE)
        _route_tile(logits_t[:, cols], bias_ref, tri_ref, carry_ref,
                    eidx_ref, rank_ref, wgt_ref, cols)
    cnt_ref[...] = carry_ref[...]


def _route_tile(logits_t, bias_ref, tri_ref, carry_ref, eidx_ref, rank_ref, wgt_ref, cols):
    tm = logits_t.shape[1]
    scores = jax.nn.sigmoid(logits_t)
    sel = scores + bias_ref[...]

    gsz = N_EXPERTS // N_EXPERT_GROUPS
    iota8 = lax.broadcasted_iota(I32, (gsz, tm), 0).astype(F32)
    ninf = jnp.float32(-jnp.inf)
    blocks, gs = [], []
    for g in range(N_EXPERT_GROUPS):
        blk = sel[g * gsz:(g + 1) * gsz, :]
        m1 = jnp.max(blk, axis=0, keepdims=True)
        first = jnp.min(jnp.where(blk == m1, iota8, float(gsz)), axis=0, keepdims=True)
        m2 = jnp.max(jnp.where(iota8 == first, ninf, blk), axis=0, keepdims=True)
        blocks.append(blk)
        gs.append(m1 + m2)
    cur = jnp.concatenate(gs, axis=0)
    gmask = jnp.zeros((N_EXPERT_GROUPS, tm), F32)
    for _ in range(TOPK_GROUPS):
        mx = jnp.max(cur, axis=0, keepdims=True)
        fi = jnp.min(jnp.where(cur == mx, iota8, float(N_EXPERT_GROUPS)), axis=0, keepdims=True)
        pick = iota8 == fi
        gmask = jnp.where(pick, 1.0, gmask)
        cur = jnp.where(pick, ninf, cur)
    cur = jnp.concatenate(
        [jnp.where(gmask[g:g + 1, :] > 0.5, blocks[g], ninf) for g in range(N_EXPERT_GROUPS)], axis=0)

    iota64 = lax.broadcasted_iota(I32, (N_EXPERTS, tm), 0).astype(F32)
    idxs, wks = [], []
    onehot = jnp.zeros((N_EXPERTS, tm), F32)
    for _ in range(TOP_K):
        mx = jnp.max(cur, axis=0, keepdims=True)
        fi = jnp.min(jnp.where(cur == mx, iota64, float(N_EXPERTS)), axis=0, keepdims=True)
        pick = iota64 == fi
        wks.append(jnp.sum(jnp.where(pick, scores, 0.0), axis=0, keepdims=True))
        idxs.append(fi)
        onehot = jnp.where(pick, 1.0, onehot)
        cur = jnp.where(pick, ninf, cur)
    wk = jnp.concatenate(wks, axis=0)
    wgt_ref[:, cols] = wk / jnp.sum(wk, axis=0, keepdims=True) * ROUTED_SCALE
    eidx_ref[:, cols] = jnp.concatenate(idxs, axis=0).astype(I32)

    before = jnp.dot(onehot.astype(BF16), tri_ref[...], preferred_element_type=F32) + carry_ref[:, 0:1]
    rank_ref[:, cols] = jnp.concatenate(
        [jnp.sum(jnp.where(iota64 == fi, before, 0.0), axis=0, keepdims=True) for fi in idxs],
        axis=0).astype(I32)
    carry_ref[...] = carry_ref[...] + jnp.sum(onehot, axis=1, keepdims=True)


def _merge_call(x2d, ya, yb, g1, wg, wba, wbb, wo, g2, wrh, wrl, bias, wsgu, wsd, tri, row0, n):
    tm = TM_MERGE
    off = row0 // tm
    full = lambda shape: pl.BlockSpec(shape, lambda i: (0,) * len(shape))
    row_in = lambda w: pl.BlockSpec((tm, w), lambda i: (i + off, 0))
    row = lambda w: pl.BlockSpec((tm, w), lambda i: (i, 0))
    col = lambda: pl.BlockSpec((TOP_K, tm), lambda i: (0, i))
    return pl.pallas_call(
        _merge_kernel,
        grid=(n // tm,),
        in_specs=[
            row_in(D_MODEL), row_in(A_OUT_W), row_in(B_Q_W),
            full((1, D_MODEL)), full((D_MODEL, 2 * D_MODEL)),
            full((A_OUT_W, D_MODEL)), full((B_Q_W, D_MODEL)), full((D_MODEL, D_MODEL)),
            full((1, D_MODEL)),
            full((D_MODEL, LANES)), full((D_MODEL, LANES)), full((N_EXPERTS, 1)),
            full((D_MODEL, 2 * EXPERT_FF)), full((EXPERT_FF, D_MODEL)),
            full((ROUTE_TILE, ROUTE_TILE)),
        ],
        out_specs=[row(D_MODEL), row(D_MODEL // 2), col(), col(), col(), full((N_EXPERTS, LANES))],
        out_shape=[
            jax.ShapeDtypeStruct((n, D_MODEL), F32),
            jax.ShapeDtypeStruct((n, D_MODEL // 2), U32),
            jax.ShapeDtypeStruct((TOP_K, n), I32),
            jax.ShapeDtypeStruct((TOP_K, n), I32),
            jax.ShapeDtypeStruct((TOP_K, n), F32),
            jax.ShapeDtypeStruct((N_EXPERTS, LANES), F32),
        ],
        scratch_shapes=[pltpu.VMEM((N_EXPERTS, LANES), F32)],
        compiler_params=_cparams(("arbitrary",)),
        name="merge_router",
    )(x2d, ya, yb, g1, wg, wba, wbb, wo, g2, wrh, wrl, bias, wsgu, wsd, tri)


EXPERT_SUBBLOCKS = 4


X_SLOTS = 3
Y_SLOTS = 2


def _expert_kernel(be_ref, nb_ref, xs_hbm, *refs):
    nw = 3 * EXPERT_SUBBLOCKS
    w_refs = refs[:nw]
    ys_hbm, xbuf, ybuf, xsem, ysem = refs[nw:]
    j = pl.program_id(0)
    nsteps = pl.num_programs(0)
    rows = EXPERT_SUBBLOCKS * BM_EXPERT

    def x_copy(step, slot):
        src = xs_hbm.at[pl.ds(pl.multiple_of(step * rows, rows), rows)]
        return pltpu.make_async_copy(src, xbuf.at[slot], xsem.at[slot])

    def y_copy(step, slot):
        dst = ys_hbm.at[pl.ds(pl.multiple_of(step * rows, rows), rows)]
        return pltpu.make_async_copy(ybuf.at[slot], dst, ysem.at[slot])

    @pl.when(j == 0)
    def _():
        x_copy(0, 0).start()
        x_copy(1, 1).start()

    @pl.when(j + 2 < nsteps)
    def _():
        x_copy(j + 2, lax.rem(j + 2, X_SLOTS)).start()

    xslot = lax.rem(j, X_SLOTS)
    yslot = lax.rem(j, Y_SLOTS)
    x_copy(j, xslot).wait()

    @pl.when(j >= Y_SLOTS)
    def _():
        y_copy(j - Y_SLOTS, yslot).wait()

    first = j * EXPERT_SUBBLOCKS

    @pl.when(first >= nb_ref[0])
    def _():
        ybuf[yslot] = jnp.zeros((rows, D_MODEL // 2), U32)

    @pl.when(first < nb_ref[0])
    def _():
        half = D_MODEL // 2
        for sub in range(EXPERT_SUBBLOCKS):
            wg_ref, wu_ref, wd_ref = w_refs[3 * sub:3 * sub + 3]
            blk = slice(sub * BM_EXPERT, (sub + 1) * BM_EXPERT)
            lo, hi = _unpack_bf16_pairs(xbuf[xslot, blk, :])
            lo = lo.astype(BF16)
            hi = hi.astype(BF16)
            gate = (jnp.dot(lo, wg_ref[0, :half, :], preferred_element_type=F32)
                    + jnp.dot(hi, wg_ref[0, half:, :], preferred_element_type=F32))
            up = (jnp.dot(lo, wu_ref[0, :half, :], preferred_element_type=F32)
                  + jnp.dot(hi, wu_ref[0, half:, :], preferred_element_type=F32))
            hb = (jax.nn.silu(gate) * up).astype(BF16)
            ybuf[yslot, blk, :] = _pack_bf16_pairs(jnp.dot(hb, wd_ref[0], preferred_element_type=F32))

    y_copy(j, yslot).start()

    @pl.when(j == nsteps - 1)
    def _():
        y_copy(j - 1, lax.rem(j - 1, Y_SLOTS)).wait()
        y_copy(j, yslot).wait()


def _expert_call(block_e, n_used, xs, wg, wu, wd):
    p_len = xs.shape[0]
    nsub = EXPERT_SUBBLOCKS
    rows = nsub * BM_EXPERT
    w_specs, w_args = [], []
    for sub in range(nsub):
        pick = lambda j, be, nb, sub=sub: (be[nsub * j + sub], 0, 0)
        w_specs += [
            pl.BlockSpec((1, D_MODEL, EXPERT_FF), pick),
            pl.BlockSpec((1, D_MODEL, EXPERT_FF), pick),
            pl.BlockSpec((1, EXPERT_FF, D_MODEL), pick),
        ]
        w_args += [wg, wu, wd]
    assert p_len // rows >= max(X_SLOTS, Y_SLOTS)
    return pl.pallas_call(
        _expert_kernel,
        grid_spec=pltpu.PrefetchScalarGridSpec(
            num_scalar_prefetch=2,
            grid=(p_len // rows,),
            in_specs=[pl.BlockSpec(memory_space=pl.ANY)] + w_specs,
            out_specs=pl.BlockSpec(memory_space=pl.ANY),
            scratch_shapes=[
                pltpu.VMEM((X_SLOTS, rows, D_MODEL // 2), U32),
                pltpu.VMEM((Y_SLOTS, rows, D_MODEL // 2), U32),
                pltpu.SemaphoreType.DMA((X_SLOTS,)),
                pltpu.SemaphoreType.DMA((Y_SLOTS,)),
            ],
        ),
        out_shape=jax.ShapeDtypeStruct((p_len, D_MODEL // 2), U32),
        compiler_params=_cparams(("arbitrary",)),
        name="routed_experts",
    )(block_e, n_used, xs, *w_args)


SC_CORES = 2
SC_SUBCORES = 16
SC_WORKERS = SC_CORES * SC_SUBCORES
SC_CHUNK = 128


def _sc_mesh():
    return plsc.VectorSubcoreMesh(core_axis_name="c", subcore_axis_name="s",
                                  num_cores=SC_CORES, num_subcores=SC_SUBCORES)


def _dispatch_rows(xp, dest3, p_len):
    n, width = xp.shape
    n_chunks = dest3.shape[0]
    per_worker = n_chunks // SC_WORKERS

    @functools.partial(
        pl.kernel, mesh=_sc_mesh(),
        out_type=jax.ShapeDtypeStruct((p_len, width), xp.dtype),
        scratch_types=[pltpu.VMEM((TOP_K, SC_CHUNK), I32), pltpu.VMEM((SC_CHUNK, width), xp.dtype),
                       pltpu.SemaphoreType.DMA],
        name="sc_dispatch")
    def body(xp_hbm, dest_hbm, xs_hbm, idx_v, rows_v, sem):
        wid = lax.axis_index("s") * SC_CORES + lax.axis_index("c")

        @pl.loop(0, per_worker)
        def _(j):
            chunk = wid * per_worker + j
            pltpu.sync_copy(dest_hbm.at[chunk], idx_v)
            pltpu.sync_copy(xp_hbm.at[pl.ds(chunk * SC_CHUNK, SC_CHUNK)], rows_v)
            copies = [pltpu.async_copy(rows_v, xs_hbm.at[idx_v.at[k]], sem) for k in range(TOP_K)]
            for cp in copies:
                cp.wait()

    return body(xp, dest3)


SC_LANES = 16
SC_REDUCE_CHUNK = 8


def _reduce_rows(ys, dest3, wb):
    width = ys.shape[1]
    tc = SC_REDUCE_CHUNK
    n_chunks = dest3.shape[0]
    n = n_chunks * tc
    per_worker = n_chunks // SC_WORKERS
    assert per_worker % 2 == 0 and dest3.shape[1] == TOP_K * tc

    @functools.partial(
        pl.kernel, mesh=_sc_mesh(),
        out_type=jax.ShapeDtypeStruct((n, width), ys.dtype),
        scratch_types=[pltpu.VMEM((per_worker, TOP_K * tc), I32),
                       pltpu.VMEM((2, TOP_K, tc, width), ys.dtype),
                       pltpu.VMEM((2, tc * TOP_K // SC_LANES, SC_LANES), ys.dtype),
                       pltpu.VMEM((2, tc, width), ys.dtype),
                       pltpu.SemaphoreType.DMA, pltpu.SemaphoreType.DMA,
                       pltpu.SemaphoreType.DMA, pltpu.SemaphoreType.DMA],
        compiler_params=pltpu.CompilerParams(needs_layout_passes=False),
        name="sc_reduce")
    def body(ys_hbm, dest_hbm, wb_hbm, out_hbm, idx_v, rows_v, w_v, r_v, gsem0, gsem1, osem0, osem1):
        wid = lax.axis_index("s") * SC_CORES + lax.axis_index("c")
        first = wid * per_worker
        gsems, osems = (gsem0, gsem1), (osem0, osem1)
        pltpu.sync_copy(dest_hbm.at[pl.ds(first, per_worker)], idx_v)

        def fetches(j, slot):
            cps = [pltpu.make_async_copy(ys_hbm.at[idx_v.at[j, pl.ds(k * tc, tc)]], rows_v.at[slot, k],
                                         gsems[slot])
                   for k in range(TOP_K)]
            wrows = tc * TOP_K // SC_LANES
            cps.append(pltpu.make_async_copy(wb_hbm.at[pl.ds((first + j) * wrows, wrows)], w_v.at[slot],
                                             gsems[slot]))
            return cps

        def write_back(j, slot):
            return pltpu.make_async_copy(r_v.at[slot], out_hbm.at[pl.ds((first + j) * tc, tc)],
                                         osems[slot])

        def reduce_chunk(slot):
            @pl.loop(0, tc)
            def _(t):
                wrow = w_v[slot, lax.shift_right_logical(t, 1), :]
                lane0 = (t & 1) * TOP_K
                ws = [plsc.bitcast(
                    wrow.at[jnp.full((SC_LANES,), lane0 + k, I32)].get(mode="promise_in_bounds"), BF16)
                    for k in range(TOP_K)]

                @pl.loop(0, width // SC_LANES)
                def _(jv):
                    col = pl.multiple_of(jv * SC_LANES, SC_LANES)
                    terms = [plsc.bitcast(rows_v[slot, k, t, pl.ds(col, SC_LANES)], BF16) * ws[k]
                             for k in range(TOP_K)]
                    while len(terms) > 1:
                        terms = [terms[i] + terms[i + 1] for i in range(0, len(terms), 2)]
                    r_v[slot, t, pl.ds(col, SC_LANES)] = plsc.bitcast(terms[0], ys.dtype)

        for cp in fetches(0, 0):
            cp.start()

        @pl.loop(0, per_worker // 2)
        def _(p):
            for slot in range(2):
                j = 2 * p + slot
                @pl.when(j + 1 < per_worker)
                def _():
                    for cp in fetches(j + 1, 1 - slot):
                        cp.start()
                for cp in fetches(j, slot):
                    cp.wait()

                @pl.when(p > 0)
                def _():
                    write_back(j - 2, slot).wait()
                reduce_chunk(slot)
                write_back(j, slot).start()

        write_back(per_worker - 2, 0).wait()
        write_back(per_worker - 1, 1).wait()

    return body(ys, dest3, wb)


def _final_add_kernel(base_ref, r_ref, *refs):
    o_ref = refs[-1]
    half = D_MODEL // 2
    lo, hi = _unpack_bf16_pairs(r_ref[...])
    o_ref[:, :half] = base_ref[:, :half] + lo
    o_ref[:, half:] = base_ref[:, half:] + hi


def _final_add_call(base, routed, prev, row0, n_total):
    n = base.shape[0]
    tm = TM_COMBINE
    off = row0 // tm
    in_specs = [pl.BlockSpec((tm, D_MODEL), lambda i: (i, 0)),
                pl.BlockSpec((tm, D_MODEL // 2), lambda i: (i, 0))]
    args, aliases = [base, routed], {}
    if prev is not None:
        in_specs.append(pl.BlockSpec(memory_space=pl.ANY))
        args.append(prev)
        aliases = {2: 0}
    return pl.pallas_call(
        _final_add_kernel,
        grid=(n // tm,),
        in_specs=in_specs,
        out_specs=pl.BlockSpec((tm, D_MODEL), lambda i: (i + off, 0)),
        out_shape=jax.ShapeDtypeStruct((n_total, D_MODEL), F32),
        input_output_aliases=aliases,
        compiler_params=_cparams(("arbitrary",)),
        name="moe_final_add",
    )(*args)


def _fold_gain(tab, gain, half):
    g = jnp.tile(gain.astype(F32), LANES // HEAD_DIM)
    return jnp.stack([tab[0] * g, tab[1] * jnp.roll(g, LANES - half), tab[2] * jnp.roll(g, half)])


def _prep_tables(seq):
    pos = jnp.arange(seq)
    cos1, sin1 = _rope_tables(pos, HEAD_DIM)
    cos_r, sin_r = _rope_tables(pos // GRID_W, HEAD_DIM // 2)
    cos_c, sin_c = _rope_tables(pos % GRID_W, HEAD_DIM // 2)
    taba = _lane_tables(cos1, sin1, HEAD_DIM // 2)
    tabb = _lane_tables(jnp.concatenate([cos_r, cos_c], -1), jnp.concatenate([sin_r, sin_c], -1),
                        HEAD_DIM // 4)
    return taba, tabb


def _layer(h, p, taba, tabb):
    b, seq, d = h.shape
    n = b * seq
    x2d = h.reshape(n, d)
    w_in = p["w_in"]
    o1 = N_GROUPS_A * A_GROUP_W
    o2 = o1 + B_Q_W
    o3 = o2 + B_KV_W
    pair_heads = jnp.array([0, 4, 1, 5, 2, 6, 3, 7])
    pair_cols = (pair_heads[:, None] * HEAD_DIM + jnp.arange(HEAD_DIM)[None, :]).reshape(-1)
    wqkv = jnp.concatenate([w_in[:, :o1], w_in[:, o1:o2][:, pair_cols], w_in[:, o2:o3]], axis=1).astype(BF16)
    tabs = jnp.stack([
        _fold_gain(taba, p["q_norm_a"] * Q_SCALE, HEAD_DIM // 2),
        _fold_gain(taba, p["k_norm_a"], HEAD_DIM // 2),
        _fold_gain(tabb, p["q_norm_b"] * Q_SCALE, HEAD_DIM // 4),
        _fold_gain(tabb, p["k_norm_b"], HEAD_DIM // 4),
    ])
    seg = jnp.arange(2 * LANES) // HEAD_DIM
    bd = jnp.where(seg[:, None] == seg[None, :], 1.0 / HEAD_DIM, 0.0).astype(BF16)
    g1 = p["norm1_g"].reshape(1, d).astype(F32)

    a0, a1, a2, qb, kvb = _proj_call(x2d, g1, wqkv, tabs, bd, seq)
    ya = _attn_a_call(a0.reshape(b, seq, A_GROUP_W), a1.reshape(b, seq // 4, 4 * A_GROUP_W),
                      a2.reshape(b, seq // 16, 16 * A_GROUP_W))
    yb = _attn_b_call(qb.reshape(b, seq, B_Q_W), kvb.reshape(b, seq, B_KV_W))

    wg = w_in[:, o3:].astype(BF16)
    wbb = p["w_branch_b"][pair_cols, :].astype(BF16)
    wr = jnp.pad(p["w_router"], ((0, 0), (0, LANES - N_EXPERTS)))
    wrh = wr.astype(BF16)
    wrl = (wr - wrh.astype(F32)).astype(BF16)
    wsgu = jnp.concatenate([p["ws_gate"], p["ws_up"]], axis=1).astype(BF16)
    tri = (jnp.arange(ROUTE_TILE)[:, None] < jnp.arange(ROUTE_TILE)[None, :]).astype(BF16)
    wba = p["w_branch_a"].astype(BF16)
    wo = p["w_out"].astype(BF16)
    g2 = p["norm2_g"].reshape(1, d).astype(F32)
    bias = p["router_bias"].reshape(N_EXPERTS, 1).astype(F32)
    wsd = p["ws_down"].astype(BF16)
    weg = p["we_gate"].astype(BF16)
    weu = p["we_up"].astype(BF16)
    wd = p["we_down"].astype(BF16)
    ya2d = ya.reshape(n, A_OUT_W)
    yb2d = yb.reshape(n, B_Q_W)

    nc = n // MOE_CHUNKS
    bm = BM_EXPERT
    n_blocks = (nc * TOP_K + N_EXPERTS * (bm - 1)) // bm + 1
    n_blocks = -(-n_blocks // EXPERT_SUBBLOCKS) * EXPERT_SUBBLOCKS
    p_len = n_blocks * bm
    block_start = jnp.arange(n_blocks, dtype=I32) * bm
    out = None
    for c in range(MOE_CHUNKS):
        base, xp, eidx, rank, wgt, cnt = _merge_call(
            x2d, ya2d, yb2d, g1, wg, wba, wbb, wo, g2, wrh, wrl, bias, wsgu, wsd, tri, c * nc, nc)
        counts = cnt[:, 0].astype(I32)
        padded = (counts + bm - 1) // bm * bm
        pend = jnp.cumsum(padded)
        pstart = pend - padded
        onehot = eidx[:, :, None] == jnp.arange(N_EXPERTS)[None, None, :]
        dest = jnp.sum(jnp.where(onehot, pstart[None, None, :], 0), axis=-1) + rank
        block_e = jnp.minimum(jnp.sum(pend[None, :] <= block_start[:, None], axis=1),
                              N_EXPERTS - 1).astype(I32)
        n_used = (pend[-1] // bm).astype(I32).reshape(1)
        dest3 = dest.reshape(TOP_K, nc // SC_CHUNK, SC_CHUNK).transpose(1, 0, 2)
        dest3r = dest.reshape(TOP_K, nc // SC_REDUCE_CHUNK, SC_REDUCE_CHUNK).transpose(1, 0, 2).reshape(
            nc // SC_REDUCE_CHUNK, TOP_K * SC_REDUCE_CHUNK)
        wbits = lax.bitcast_convert_type(wgt.T.astype(BF16), jnp.uint16).astype(U32)
        wb = (wbits | (wbits << 16)).reshape(nc * TOP_K // SC_LANES, SC_LANES)

        xs = _dispatch_rows(xp, dest3, p_len)
        ys = _expert_call(block_e, n_used, xs, weg, weu, wd)
        routed = _reduce_rows(ys, dest3r, wb)
        out = _final_add_call(base, routed, out, c * nc, n)
    return out.reshape(b, seq, d)


def kernel(x, norm1_g, w_in, q_norm_a, k_norm_a, q_norm_b, k_norm_b, w_branch_a, w_branch_b, w_out,
           norm2_g, w_router, router_bias, we_gate, we_up, we_down, ws_gate, ws_up, ws_down):
    params = dict(norm1_g=norm1_g, w_in=w_in, q_norm_a=q_norm_a, k_norm_a=k_norm_a, q_norm_b=q_norm_b,
                  k_norm_b=k_norm_b, w_branch_a=w_branch_a, w_branch_b=w_branch_b, w_out=w_out,
                  norm2_g=norm2_g, w_router=w_router, router_bias=router_bias, we_gate=we_gate,
                  we_up=we_up, we_down=we_down, ws_gate=ws_gate, ws_up=ws_up, ws_down=ws_down)
    taba, tabb = _prep_tables(x.shape[1])
    h = x
    for l in range(norm1_g.shape[0]):
        h = _layer(h, {k: v[l] for k, v in params.items()}, taba, tabb)
    return h
```

```python
import functools

import jax
import jax.numpy as jnp
from jax import lax
from jax.experimental import pallas as pl
from jax.experimental.pallas import tpu as pltpu
from jax.experimental.pallas import tpu_sc as plsc

F32 = jnp.float32
BF16 = jnp.bfloat16
I32 = jnp.int32
U32 = jnp.uint32

D_MODEL = 1024
HEAD_DIM = 64
ROPE_THETA = 10000.0
EPS = 1e-6
GRID_W = 64
DILATIONS = (1, 4, 16)
RADIUS = 64
N_GROUPS_A = 3
A_GROUP_W = 768
A_OUT_W = 256
B_Q_W = 512
B_KV_W = 256
QKV_W = N_GROUPS_A * A_GROUP_W + B_Q_W + B_KV_W
N_EXPERTS = 64
N_EXPERT_GROUPS = 8
TOPK_GROUPS = 4
TOP_K = 8
EXPERT_FF = 256
ROUTED_SCALE = 2.5

LANES = 128
NEG_BIG = -1e30
Q_SCALE = HEAD_DIM ** -0.5 * 1.4426950408889634

TM_PROJ = 1024
TQ_B = 512
TK_B = 512
TM_MERGE = 256
ROUTE_TILE = 256
BM_EXPERT = 512
TM_COMBINE = 512
MOE_CHUNKS = 2
VMEM_LIMIT = 56 * 1024 * 1024


def _cparams(sem):
    return pltpu.CompilerParams(dimension_semantics=sem, vmem_limit_bytes=VMEM_LIMIT)


def _rope_tables(pos, dim):
    inv = ROPE_THETA ** (-jnp.arange(0, dim, 2, dtype=F32) / dim)
    ang = pos.astype(F32)[:, None] * inv[None, :]
    ang = jnp.concatenate([ang, ang], axis=-1)
    return jnp.cos(ang), jnp.sin(ang)


def _lane_tables(cos_h, sin_h, half):
    cos2 = jnp.concatenate([cos_h, cos_h], axis=-1)
    sin2 = jnp.concatenate([sin_h, sin_h], axis=-1)
    first = (jnp.arange(LANES) % (2 * half)) < half
    s_left = jnp.where(first[None, :], -sin2, 0.0)
    s_right = jnp.where(first[None, :], 0.0, sin2)
    return jnp.stack([cos2, s_left, s_right]).astype(F32)


TAB_AQ, TAB_AK, TAB_BQ, TAB_BK = range(4)


def _proj_kernel(x_ref, g1_ref, w_ref, tab_ref, bd_ref,
                 a0_ref, a1_ref, a2_ref, qb_ref, kvb_ref, pbuf_ref):
    x = x_ref[...]
    ms = jnp.mean(x * x, axis=-1, keepdims=True)
    xn = (x * lax.rsqrt(ms + EPS) * g1_ref[...]).astype(BF16)
    tm = x.shape[0]

    def norm_rope(y, t, shift):
        w = y.shape[1]
        ss = jnp.dot((y * y).astype(BF16), bd_ref[0:w, 0:w], preferred_element_type=F32)
        yn = y * lax.rsqrt(ss + EPS)
        out = []
        for i in range(w // LANES):
            z = yn[:, i * LANES:(i + 1) * LANES]
            out.append(z * tab_ref[t, 0] + pltpu.roll(z, LANES - shift, 1) * tab_ref[t, 1]
                       + pltpu.roll(z, shift, 1) * tab_ref[t, 2])
        return out

    for g, d in enumerate(DILATIONS):
        base = g * A_GROUP_W
        r = jnp.dot(xn, w_ref[:, base:base + A_GROUP_W], preferred_element_type=F32)
        chunks = (norm_rope(r[:, 0:2 * LANES], TAB_AQ, HEAD_DIM // 2)
                  + norm_rope(r[:, 2 * LANES:4 * LANES], TAB_AK, HEAD_DIM // 2)
                  + [r[:, 4 * LANES:5 * LANES], r[:, 5 * LANES:6 * LANES]])
        for c, y in enumerate(chunks):
            if d == 1:
                a0_ref[:, c * LANES:(c + 1) * LANES] = y.astype(BF16)
            else:
                pbuf_ref[c] = y
        if d > 1:
            out_ref = a1_ref if g == 1 else a2_ref
            rows = tm // d
            for res in range(d):
                for c in range(A_GROUP_W // LANES):
                    col = res * A_GROUP_W + c * LANES
                    out_ref[:, col:col + LANES] = pbuf_ref[c, pl.ds(res, rows, stride=d), :].astype(BF16)

    base = N_GROUPS_A * A_GROUP_W
    r = jnp.dot(xn, w_ref[:, base:base + B_Q_W + B_KV_W], preferred_element_type=F32)
    chunks = (norm_rope(r[:, 0:2 * LANES], TAB_BQ, HEAD_DIM // 4)
              + norm_rope(r[:, 2 * LANES:4 * LANES], TAB_BQ, HEAD_DIM // 4)
              + norm_rope(r[:, 4 * LANES:5 * LANES], TAB_BK, HEAD_DIM // 4)
              + [r[:, 5 * LANES:6 * LANES]])
    for c, y in enumerate(chunks):
        if c < 4:
            qb_ref[:, c * LANES:(c + 1) * LANES] = y.astype(BF16)
        else:
            kvb_ref[:, (c - 4) * LANES:(c - 3) * LANES] = y.astype(BF16)


def _proj_call(x2d, g1, wqkv, tabs, bd, seq):
    n = x2d.shape[0]
    tm = TM_PROJ
    tiles_per_seq = seq // tm
    full = lambda shape: pl.BlockSpec(shape, lambda i: (0,) * len(shape))
    return pl.pallas_call(
        _proj_kernel,
        grid=(n // tm,),
        in_specs=[
            pl.BlockSpec((tm, D_MODEL), lambda i: (i, 0)),
            full((1, D_MODEL)),
            full((D_MODEL, QKV_W)),
            pl.BlockSpec((4, 3, tm, LANES), lambda i: (0, 0, i % tiles_per_seq, 0)),
            full((2 * LANES, 2 * LANES)),
        ],
        out_specs=[
            pl.BlockSpec((tm, A_GROUP_W), lambda i: (i, 0)),
            pl.BlockSpec((tm // 4, 4 * A_GROUP_W), lambda i: (i, 0)),
            pl.BlockSpec((tm // 16, 16 * A_GROUP_W), lambda i: (i, 0)),
            pl.BlockSpec((tm, B_Q_W), lambda i: (i, 0)),
            pl.BlockSpec((tm, B_KV_W), lambda i: (i, 0)),
        ],
        out_shape=[
            jax.ShapeDtypeStruct((n, A_GROUP_W), BF16),
            jax.ShapeDtypeStruct((n // 4, 4 * A_GROUP_W), BF16),
            jax.ShapeDtypeStruct((n // 16, 16 * A_GROUP_W), BF16),
            jax.ShapeDtypeStruct((n, B_Q_W), BF16),
            jax.ShapeDtypeStruct((n, B_KV_W), BF16),
        ],
        scratch_shapes=[pltpu.VMEM((A_GROUP_W // LANES, tm, LANES), F32)],
        compiler_params=_cparams(("arbitrary",)),
        name="proj_qkv",
    )(x2d, g1, wqkv, tabs, bd)


def _attn_a_kernel(a0_ref, a1_ref, a2_ref, bw_ref, bn_ref, ya_ref, acc_ref, m_ref, l_ref):
    tq = 128
    lane = lax.broadcasted_iota(I32, (tq, LANES), 1)
    low = lane < HEAD_DIM

    def tile(g, ref, col0, length, i, rows):
        win = min(2 * tq, length)
        if isinstance(i, int):
            q0 = i * tq
            ws = max(0, min(q0 - RADIUS, length - win))
        else:
            q0 = pl.multiple_of(i * tq, tq)
            ws = pl.multiple_of(jnp.clip(q0 - RADIUS, 0, length - win), RADIUS)
        if win == length:
            bias = bn_ref[...]
        elif isinstance(i, int):
            bias = bw_ref[(q0 - ws) // RADIUS]
        else:
            bias = bw_ref[lax.shift_right_logical(q0 - ws, 6)]
        for hp in range(2):
            q2 = ref[0, pl.ds(q0, tq), col0 + hp * LANES:col0 + (hp + 1) * LANES]
            k2 = ref[0, pl.ds(ws, win), col0 + 256 + hp * LANES:col0 + 256 + (hp + 1) * LANES]
            v2 = ref[0, pl.ds(ws, win), col0 + 512 + hp * LANES:col0 + 512 + (hp + 1) * LANES]
            zero = jnp.zeros_like(q2)
            qs = jnp.concatenate([jnp.where(low, q2, zero), jnp.where(low, zero, q2)], axis=0)
            s = lax.dot_general(qs, k2, (((1,), (1,)), ((), ())), preferred_element_type=F32)
            s = s + bias
            m = jnp.max(s, axis=-1, keepdims=True)
            p = jnp.exp2(s - m)
            l = jnp.sum(p, axis=-1, keepdims=True)
            pv = jnp.dot(p.astype(BF16), v2, preferred_element_type=F32)
            slot = 2 * g + hp
            acc_ref[slot, rows, :] = jnp.where(low, pv[:tq], pv[tq:])
            m_ref[slot, rows, :] = jnp.where(low, m[:tq], m[tq:])
            l_ref[slot, rows, :] = jnp.where(low, l[:tq], l[tq:])

    seq = a0_ref.shape[1]

    def g0_body(i, carry):
        tile(0, a0_ref, 0, seq, i, pl.ds(pl.multiple_of(i * tq, tq), tq))
        return carry
    lax.fori_loop(0, seq // tq, g0_body, 0, unroll=4)

    len1 = seq // 4
    for res in range(4):
        def g1_body(i, carry, res=res):
            tile(1, a1_ref, res * A_GROUP_W, len1, i, pl.ds(res + 4 * i * tq, tq, stride=4))
            return carry
        lax.fori_loop(0, len1 // tq, g1_body, 0, unroll=4)

    len2 = seq // 16
    for res in range(16):
        tile(2, a2_ref, res * A_GROUP_W, len2, 0, pl.ds(res, tq, stride=16))

    rc = 256

    def merge_body(j, carry):
        rows = pl.ds(pl.multiple_of(j * rc, rc), rc)
        for hp in range(2):
            ms = [m_ref[2 * g + hp, rows, :] for g in range(N_GROUPS_A)]
            mx = jnp.maximum(jnp.maximum(ms[0], ms[1]), ms[2])
            num = jnp.zeros((rc, LANES), F32)
            den = jnp.zeros((rc, LANES), F32)
            for g in range(N_GROUPS_A):
                e = jnp.exp2(ms[g] - mx)
                num = num + e * acc_ref[2 * g + hp, rows, :]
                den = den + e * l_ref[2 * g + hp, rows, :]
            ya_ref[0, rows, hp * LANES:(hp + 1) * LANES] = (num / den).astype(BF16)
        return carry
    lax.fori_loop(0, seq // rc, merge_body, 0)


def _band_bias(offset, win):
    q = jnp.arange(2 * 128) % 128 + offset
    k = jnp.arange(win)
    return jnp.where(jnp.abs(q[:, None] - k[None, :]) <= RADIUS, 0.0, NEG_BIG).astype(F32)


def _attn_a_call(a0, a1, a2):
    b, seq, _ = a0.shape
    bias_wide = jnp.stack([_band_bias(off, 256) for off in (0, RADIUS, 2 * RADIUS)])
    bias_narrow = _band_bias(0, 128)
    return pl.pallas_call(
        _attn_a_kernel,
        grid=(b,),
        in_specs=[
            pl.BlockSpec((1, seq, A_GROUP_W), lambda i: (i, 0, 0)),
            pl.BlockSpec((1, seq // 4, 4 * A_GROUP_W), lambda i: (i, 0, 0)),
            pl.BlockSpec((1, seq // 16, 16 * A_GROUP_W), lambda i: (i, 0, 0)),
            pl.BlockSpec((3, 256, 256), lambda i: (0, 0, 0)),
            pl.BlockSpec((256, 128), lambda i: (0, 0)),
        ],
        out_specs=pl.BlockSpec((1, seq, A_OUT_W), lambda i: (i, 0, 0)),
        out_shape=jax.ShapeDtypeStruct((b, seq, A_OUT_W), BF16),
        scratch_shapes=[pltpu.VMEM((N_GROUPS_A * A_OUT_W // LANES, seq, LANES), F32)] * 3,
        compiler_params=_cparams(("arbitrary",)),
        name="attn_a",
    )(a0, a1, a2, bias_wide, bias_narrow)


def _attn_b_kernel(q_ref, kv_ref, o_ref, va_ref, vb_ref):
    tq = q_ref.shape[1]
    seq = kv_ref.shape[1]
    low = lax.broadcasted_iota(I32, (tq, LANES), 1) < HEAD_DIM

    @pl.when(pl.program_id(1) == 0)
    def _():
        low_k = lax.broadcasted_iota(I32, (seq, LANES), 1) < HEAD_DIM
        v2 = kv_ref[0, :, LANES:2 * LANES]
        one = jnp.ones_like(v2)
        va_ref[...] = jnp.where(low_k, v2, one)
        vb_ref[...] = jnp.where(low_k, one, v2)

    for pr in range(B_Q_W // LANES):
        q2 = q_ref[0, :, pr * LANES:(pr + 1) * LANES]
        zero = jnp.zeros_like(q2)
        qs = jnp.concatenate([jnp.where(low, q2, zero), jnp.where(low, zero, q2)], axis=0)
        m = jnp.full((2 * tq, 1), NEG_BIG, F32)
        acc_a = jnp.zeros((tq, LANES), F32)
        acc_b = jnp.zeros((tq, LANES), F32)
        for c in range(seq // TK_B):
            k2 = kv_ref[0, c * TK_B:(c + 1) * TK_B, 0:LANES]
            va = va_ref[c * TK_B:(c + 1) * TK_B, :]
            vb = vb_ref[c * TK_B:(c + 1) * TK_B, :]
            s = lax.dot_general(qs, k2, (((1,), (1,)), ((), ())), preferred_element_type=F32)
            m_n = jnp.maximum(m, jnp.max(s, axis=-1, keepdims=True))
            p = jnp.exp2(s - m_n).astype(BF16)
            a = jnp.exp2(m - m_n)
            acc_a = a[:tq] * acc_a + jnp.dot(p[:tq], va, preferred_element_type=F32)
            acc_b = a[tq:] * acc_b + jnp.dot(p[tq:], vb, preferred_element_type=F32)
            m = m_n
        oa = acc_a / pltpu.roll(acc_a, HEAD_DIM, 1)
        ob = acc_b / pltpu.roll(acc_b, HEAD_DIM, 1)
        o_ref[0, :, pr * LANES:(pr + 1) * LANES] = jnp.where(low, oa, ob).astype(BF16)


def _attn_b_call(qb, kvb):
    b, seq, _ = qb.shape
    return pl.pallas_call(
        _attn_b_kernel,
        grid=(b, seq // TQ_B),
        in_specs=[
            pl.BlockSpec((1, TQ_B, B_Q_W), lambda i, j: (i, j, 0)),
            pl.BlockSpec((1, seq, B_KV_W), lambda i, j: (i, 0, 0)),
        ],
        out_specs=pl.BlockSpec((1, TQ_B, B_Q_W), lambda i, j: (i, j, 0)),
        out_shape=jax.ShapeDtypeStruct((b, seq, B_Q_W), BF16),
        scratch_shapes=[pltpu.VMEM((seq, LANES), BF16)] * 2,
        compiler_params=_cparams(("arbitrary",) * 2),
        name="attn_b",
    )(qb, kvb)


def _pack_bf16_pairs(v):
    half = v.shape[1] // 2
    lo = lax.bitcast_convert_type(v[:, :half].astype(BF16).astype(F32), U32)
    hi = lax.bitcast_convert_type(v[:, half:].astype(BF16).astype(F32), U32)
    return (lo >> 16) | (hi & jnp.uint32(0xFFFF0000))


def _unpack_bf16_pairs(p):
    lo = lax.bitcast_convert_type(p << 16, F32)
    hi = lax.bitcast_convert_type(p & jnp.uint32(0xFFFF0000), F32)
    return lo, hi


def _merge_kernel(x_ref, ya_ref, yb_ref, g1_ref, wg_ref, wba_ref, wbb_ref, wo_ref, g2_ref,
                  wrh_ref, wrl_ref, bias_ref, wsgu_ref, wsd_ref, tri_ref,
                  base_ref, xp_ref, eidx_ref, rank_ref, wgt_ref, cnt_ref, carry_ref):
    step = pl.program_id(0)

    @pl.when(step == 0)
    def _():
        carry_ref[...] = jnp.zeros_like(carry_ref)

    x = x_ref[...]
    tm = x.shape[0]
    ms = jnp.mean(x * x, axis=-1, keepdims=True)
    xn = (x * lax.rsqrt(ms + EPS) * g1_ref[...]).astype(BF16)
    gates = jax.nn.sigmoid(jnp.dot(xn, wg_ref[...], preferred_element_type=F32))
    pa = jnp.dot(ya_ref[...], wba_ref[...], preferred_element_type=F32)
    pb = jnp.dot(yb_ref[...], wbb_ref[...], preferred_element_type=F32)
    merged = gates[:, :D_MODEL] * pa + gates[:, D_MODEL:] * pb
    h = x + jnp.dot(merged.astype(BF16), wo_ref[...], preferred_element_type=F32)

    ms2 = jnp.mean(h * h, axis=-1, keepdims=True)
    xn2 = h * lax.rsqrt(ms2 + EPS) * g2_ref[...]
    xn2b = xn2.astype(BF16)
    xp_ref[...] = _pack_bf16_pairs(xn2)

    gu = jnp.dot(xn2b, wsgu_ref[...], preferred_element_type=F32)
    hs = (jax.nn.silu(gu[:, :EXPERT_FF]) * gu[:, EXPERT_FF:]).astype(BF16)
    base_ref[...] = h + jnp.dot(hs, wsd_ref[...], preferred_element_type=F32)

    xlo = (xn2 - xn2b.astype(F32)).astype(BF16)
    logits = (jnp.dot(xn2b, wrh_ref[...], preferred_element_type=F32)
              + jnp.dot(xlo, wrh_ref[...], preferred_element_type=F32)
              + jnp.dot(xn2b, wrl_ref[...], preferred_element_type=F32))
    logits_t = logits.T[:N_EXPERTS, :]
    for h in range(tm // ROUTE_TILE):
        cols = slice(h * ROUTE_TILE, (h + 1) * ROUTE_TILE)
        _route_tile(logits_t[:, cols], bias_ref, tri_ref, carry_ref,
                    eidx_ref, rank_ref, wgt_ref, cols)
    cnt_ref[...] = carry_ref[...]


def _route_tile(logits_t, bias_ref, tri_ref, carry_ref, eidx_ref, rank_ref, wgt_ref, cols):
    tm = logits_t.shape[1]
    scores = jax.nn.sigmoid(logits_t)
    sel = scores + bias_ref[...]

    gsz = N_EXPERTS // N_EXPERT_GROUPS
    iota8 = lax.broadcasted_iota(I32, (gsz, tm), 0).astype(F32)
    ninf = jnp.float32(-jnp.inf)
    blocks, gs = [], []
    for g in range(N_EXPERT_GROUPS):
        blk = sel[g * gsz:(g + 1) * gsz, :]
        m1 = jnp.max(blk, axis=0, keepdims=True)
        first = jnp.min(jnp.where(blk == m1, iota8, float(gsz)), axis=0, keepdims=True)
        m2 = jnp.max(jnp.where(iota8 == first, ninf, blk), axis=0, keepdims=True)
        blocks.append(blk)
        gs.append(m1 + m2)
    cur = jnp.concatenate(gs, axis=0)
    gmask = jnp.zeros((N_EXPERT_GROUPS, tm), F32)
    for _ in range(TOPK_GROUPS):
        mx = jnp.max(cur, axis=0, keepdims=True)
        fi = jnp.min(jnp.where(cur == mx, iota8, float(N_EXPERT_GROUPS)), axis=0, keepdims=True)
        pick = iota8 == fi
        gmask = jnp.where(pick, 1.0, gmask)
        cur = jnp.where(pick, ninf, cur)
    cur = jnp.concatenate(
        [jnp.where(gmask[g:g + 1, :] > 0.5, blocks[g], ninf) for g in range(N_EXPERT_GROUPS)], axis=0)

    iota64 = lax.broadcasted_iota(I32, (N_EXPERTS, tm), 0).astype(F32)
    idxs, wks = [], []
    onehot = jnp.zeros((N_EXPERTS, tm), F32)
    for _ in range(TOP_K):
        mx = jnp.max(cur, axis=0, keepdims=True)
        fi = jnp.min(jnp.where(cur == mx, iota64, float(N_EXPERTS)), axis=0, keepdims=True)
        pick = iota64 == fi
        wks.append(jnp.sum(jnp.where(pick, scores, 0.0), axis=0, keepdims=True))
        idxs.append(fi)
        onehot = jnp.where(pick, 1.0, onehot)
        cur = jnp.where(pick, ninf, cur)
    wk = jnp.concatenate(wks, axis=0)
    wgt_ref[:, cols] = wk / jnp.sum(wk, axis=0, keepdims=True) * ROUTED_SCALE
    eidx_ref[:, cols] = jnp.concatenate(idxs, axis=0).astype(I32)

    before = jnp.dot(onehot.astype(BF16), tri_ref[...], preferred_element_type=F32) + carry_ref[:, 0:1]
    rank_ref[:, cols] = jnp.concatenate(
        [jnp.sum(jnp.where(iota64 == fi, before, 0.0), axis=0, keepdims=True) for fi in idxs],
        axis=0).astype(I32)
    carry_ref[...] = carry_ref[...] + jnp.sum(onehot, axis=1, keepdims=True)


def _merge_call(x2d, ya, yb, g1, wg, wba, wbb, wo, g2, wrh, wrl, bias, wsgu, wsd, tri, row0, n):
    tm = TM_MERGE
    off = row0 // tm
    full = lambda shape: pl.BlockSpec(shape, lambda i: (0,) * len(shape))
    row_in = lambda w: pl.BlockSpec((tm, w), lambda i: (i + off, 0))
    row = lambda w: pl.BlockSpec((tm, w), lambda i: (i, 0))
    col = lambda: pl.BlockSpec((TOP_K, tm), lambda i: (0, i))
    return pl.pallas_call(
        _merge_kernel,
        grid=(n // tm,),
        in_specs=[
            row_in(D_MODEL), row_in(A_OUT_W), row_in(B_Q_W),
            full((1, D_MODEL)), full((D_MODEL, 2 * D_MODEL)),
            full((A_OUT_W, D_MODEL)), full((B_Q_W, D_MODEL)), full((D_MODEL, D_MODEL)),
            full((1, D_MODEL)),
            full((D_MODEL, LANES)), full((D_MODEL, LANES)), full((N_EXPERTS, 1)),
            full((D_MODEL, 2 * EXPERT_FF)), full((EXPERT_FF, D_MODEL)),
            full((ROUTE_TILE, ROUTE_TILE)),
        ],
        out_specs=[row(D_MODEL), row(D_MODEL // 2), col(), col(), col(), full((N_EXPERTS, LANES))],
        out_shape=[
            jax.ShapeDtypeStruct((n, D_MODEL), F32),
            jax.ShapeDtypeStruct((n, D_MODEL // 2), U32),
            jax.ShapeDtypeStruct((TOP_K, n), I32),
            jax.ShapeDtypeStruct((TOP_K, n), I32),
            jax.ShapeDtypeStruct((TOP_K, n), F32),
            jax.ShapeDtypeStruct((N_EXPERTS, LANES), F32),
        ],
        scratch_shapes=[pltpu.VMEM((N_EXPERTS, LANES), F32)],
        compiler_params=_cparams(("arbitrary",)),
        name="merge_router",
    )(x2d, ya, yb, g1, wg, wba, wbb, wo, g2, wrh, wrl, bias, wsgu, wsd, tri)


EXPERT_SUBBLOCKS = 4


X_SLOTS = 3
Y_SLOTS = 2


def _expert_kernel(be_ref, nb_ref, xs_hbm, *refs):
    nw = 3 * EXPERT_SUBBLOCKS
    w_refs = refs[:nw]
    ys_hbm, xbuf, ybuf, xsem, ysem = refs[nw:]
    j = pl.program_id(0)
    nsteps = pl.num_programs(0)
    rows = EXPERT_SUBBLOCKS * BM_EXPERT

    def x_copy(step, slot):
        src = xs_hbm.at[pl.ds(pl.multiple_of(step * rows, rows), rows)]
        return pltpu.make_async_copy(src, xbuf.at[slot], xsem.at[slot])

    def y_copy(step, slot):
        dst = ys_hbm.at[pl.ds(pl.multiple_of(step * rows, rows), rows)]
        return pltpu.make_async_copy(ybuf.at[slot], dst, ysem.at[slot])

    @pl.when(j == 0)
    def _():
        x_copy(0, 0).start()
        x_copy(1, 1).start()

    @pl.when(j + 2 < nsteps)
    def _():
        x_copy(j + 2, lax.rem(j + 2, X_SLOTS)).start()

    xslot = lax.rem(j, X_SLOTS)
    yslot = lax.rem(j, Y_SLOTS)
    x_copy(j, xslot).wait()

    @pl.when(j >= Y_SLOTS)
    def _():
        y_copy(j - Y_SLOTS, yslot).wait()

    first = j * EXPERT_SUBBLOCKS

    @pl.when(first >= nb_ref[0])
    def _():
        ybuf[yslot] = jnp.zeros((rows, D_MODEL // 2), U32)

    @pl.when(first < nb_ref[0])
    def _():
        half = D_MODEL // 2
        for sub in range(EXPERT_SUBBLOCKS):
            wg_ref, wu_ref, wd_ref = w_refs[3 * sub:3 * sub + 3]
            blk = slice(sub * BM_EXPERT, (sub + 1) * BM_EXPERT)
            lo, hi = _unpack_bf16_pairs(xbuf[xslot, blk, :])
            lo = lo.astype(BF16)
            hi = hi.astype(BF16)
            gate = (jnp.dot(lo, wg_ref[0, :half, :], preferred_element_type=F32)
                    + jnp.dot(hi, wg_ref[0, half:, :], preferred_element_type=F32))
            up = (jnp.dot(lo, wu_ref[0, :half, :], preferred_element_type=F32)
                  + jnp.dot(hi, wu_ref[0, half:, :], preferred_element_type=F32))
            hb = (jax.nn.silu(gate) * up).astype(BF16)
            ybuf[yslot, blk, :] = _pack_bf16_pairs(jnp.dot(hb, wd_ref[0], preferred_element_type=F32))

    y_copy(j, yslot).start()

    @pl.when(j == nsteps - 1)
    def _():
        y_copy(j - 1, lax.rem(j - 1, Y_SLOTS)).wait()
        y_copy(j, yslot).wait()


def _expert_call(block_e, n_used, xs, wg, wu, wd):
    p_len = xs.shape[0]
    nsub = EXPERT_SUBBLOCKS
    rows = nsub * BM_EXPERT
    w_specs, w_args = [], []
    for sub in range(nsub):
        pick = lambda j, be, nb, sub=sub: (be[nsub * j + sub], 0, 0)
        w_specs += [
            pl.BlockSpec((1, D_MODEL, EXPERT_FF), pick),
            pl.BlockSpec((1, D_MODEL, EXPERT_FF), pick),
            pl.BlockSpec((1, EXPERT_FF, D_MODEL), pick),
        ]
        w_args += [wg, wu, wd]
    assert p_len // rows >= max(X_SLOTS, Y_SLOTS)
    return pl.pallas_call(
        _expert_kernel,
        grid_spec=pltpu.PrefetchScalarGridSpec(
            num_scalar_prefetch=2,
            grid=(p_len // rows,),
            in_specs=[pl.BlockSpec(memory_space=pl.ANY)] + w_specs,
            out_specs=pl.BlockSpec(memory_space=pl.ANY),
            scratch_shapes=[
                pltpu.VMEM((X_SLOTS, rows, D_MODEL // 2), U32),
                pltpu.VMEM((Y_SLOTS, rows, D_MODEL // 2), U32),
                pltpu.SemaphoreType.DMA((X_SLOTS,)),
                pltpu.SemaphoreType.DMA((Y_SLOTS,)),
            ],
        ),
        out_shape=jax.ShapeDtypeStruct((p_len, D_MODEL // 2), U32),
        compiler_params=_cparams(("arbitrary",)),
        name="routed_experts",
    )(block_e, n_used, xs, *w_args)


SC_CORES = 2
SC_SUBCORES = 16
SC_WORKERS = SC_CORES * SC_SUBCORES
SC_CHUNK = 128


def _sc_mesh():
    return plsc.VectorSubcoreMesh(core_axis_name="c", subcore_axis_name="s",
                                  num_cores=SC_CORES, num_subcores=SC_SUBCORES)


def _dispatch_rows(xp, dest, p_len):
    n, width = xp.shape
    per_worker = n // SC_CHUNK // SC_WORKERS

    @functools.partial(
        pl.kernel, mesh=_sc_mesh(),
        out_type=jax.ShapeDtypeStruct((p_len, width), xp.dtype),
        scratch_types=[pltpu.VMEM((TOP_K, SC_CHUNK), I32), pltpu.VMEM((SC_CHUNK, width), xp.dtype),
                       pltpu.SemaphoreType.DMA],
        name="sc_dispatch")
    def body(xp_hbm, dest_hbm, xs_hbm, idx_v, rows_v, sem):
        wid = lax.axis_index("s") * SC_CORES + lax.axis_index("c")

        @pl.loop(0, per_worker)
        def _(j):
            chunk = wid * per_worker + j
            pltpu.sync_copy(dest_hbm.at[:, pl.ds(chunk * SC_CHUNK, SC_CHUNK)], idx_v)
            pltpu.sync_copy(xp_hbm.at[pl.ds(chunk * SC_CHUNK, SC_CHUNK)], rows_v)
            copies = [pltpu.async_copy(rows_v, xs_hbm.at[idx_v.at[k]], sem) for k in range(TOP_K)]
            for cp in copies:
                cp.wait()

    return body(xp, dest)


SC_LANES = 16
SC_REDUCE_CHUNK = 8


def _reduce_rows(ys, dest, wbits):
    width = ys.shape[1]
    tc = SC_REDUCE_CHUNK
    n = dest.shape[1]
    tokens = n // SC_WORKERS
    per_worker = tokens // tc
    assert per_worker % 2 == 0 and 2 * tc == SC_LANES

    @functools.partial(
        pl.kernel, mesh=_sc_mesh(),
        out_type=jax.ShapeDtypeStruct((n, width), ys.dtype),
        scratch_types=[pltpu.VMEM((TOP_K, tokens), I32),
                       pltpu.VMEM((TOP_K, tokens), ys.dtype),
                       pltpu.VMEM((2, TOP_K, tc, width), ys.dtype),
                       pltpu.VMEM((2, tc, width), ys.dtype),
                       pltpu.SemaphoreType.DMA, pltpu.SemaphoreType.DMA,
                       pltpu.SemaphoreType.DMA, pltpu.SemaphoreType.DMA],
        compiler_params=pltpu.CompilerParams(needs_layout_passes=False),
        name="sc_reduce")
    def body(ys_hbm, dest_hbm, w_hbm, out_hbm, idx_v, w_v, rows_v, r_v, gsem0, gsem1, osem0, osem1):
        wid = lax.axis_index("s") * SC_CORES + lax.axis_index("c")
        first = wid * tokens
        gsems, osems = (gsem0, gsem1), (osem0, osem1)
        pltpu.sync_copy(dest_hbm.at[:, pl.ds(first, tokens)], idx_v)
        pltpu.sync_copy(w_hbm.at[:, pl.ds(first, tokens)], w_v)

        def fetches(j, slot):
            return [pltpu.make_async_copy(ys_hbm.at[idx_v.at[k, pl.ds(j * tc, tc)]], rows_v.at[slot, k],
                                          gsems[slot])
                    for k in range(TOP_K)]

        def write_back(j, slot):
            return pltpu.make_async_copy(r_v.at[slot], out_hbm.at[pl.ds(first + j * tc, tc)],
                                         osems[slot])

        def reduce_item(p, slot):
            wrows = [w_v[k, pl.ds(pl.multiple_of(p * SC_LANES, SC_LANES), SC_LANES)] for k in range(TOP_K)]

            @pl.loop(0, tc)
            def _(t):
                lane = jnp.full((SC_LANES,), slot * tc + t, I32)
                ws = [plsc.bitcast(wrows[k].at[lane].get(mode="promise_in_bounds"), BF16)
                      for k in range(TOP_K)]

                @pl.loop(0, width // SC_LANES)
                def _(jv):
                    col = pl.multiple_of(jv * SC_LANES, SC_LANES)
                    terms = [plsc.bitcast(rows_v[slot, k, t, pl.ds(col, SC_LANES)], BF16) * ws[k]
                             for k in range(TOP_K)]
                    while len(terms) > 1:
                        terms = [terms[i] + terms[i + 1] for i in range(0, len(terms), 2)]
                    r_v[slot, t, pl.ds(col, SC_LANES)] = plsc.bitcast(terms[0], ys.dtype)

        for cp in fetches(0, 0):
            cp.start()

        @pl.loop(0, per_worker // 2)
        def _(p):
            for slot in range(2):
                j = 2 * p + slot
                @pl.when(j + 1 < per_worker)
                def _():
                    for cp in fetches(j + 1, 1 - slot):
                        cp.start()
                for cp in fetches(j, slot):
                    cp.wait()

                @pl.when(p > 0)
                def _():
                    write_back(j - 2, slot).wait()
                reduce_item(p, slot)
                write_back(j, slot).start()

        write_back(per_worker - 2, 0).wait()
        write_back(per_worker - 1, 1).wait()

    return body(ys, dest, wbits)


def _final_add_kernel(base_ref, r_ref, *refs):
    o_ref = refs[-1]
    half = D_MODEL // 2
    lo, hi = _unpack_bf16_pairs(r_ref[...])
    o_ref[:, :half] = base_ref[:, :half] + lo
    o_ref[:, half:] = base_ref[:, half:] + hi


def _final_add_call(base, routed, prev, row0, n_total):
    n = base.shape[0]
    tm = TM_COMBINE
    off = row0 // tm
    in_specs = [pl.BlockSpec((tm, D_MODEL), lambda i: (i, 0)),
                pl.BlockSpec((tm, D_MODEL // 2), lambda i: (i, 0))]
    args, aliases = [base, routed], {}
    if prev is not None:
        in_specs.append(pl.BlockSpec(memory_space=pl.ANY))
        args.append(prev)
        aliases = {2: 0}
    return pl.pallas_call(
        _final_add_kernel,
        grid=(n // tm,),
        in_specs=in_specs,
        out_specs=pl.BlockSpec((tm, D_MODEL), lambda i: (i + off, 0)),
        out_shape=jax.ShapeDtypeStruct((n_total, D_MODEL), F32),
        input_output_aliases=aliases,
        compiler_params=_cparams(("arbitrary",)),
        name="moe_final_add",
    )(*args)


def _fold_gain(tab, gain, half):
    g = jnp.tile(gain.astype(F32), LANES // HEAD_DIM)
    return jnp.stack([tab[0] * g, tab[1] * jnp.roll(g, LANES - half), tab[2] * jnp.roll(g, half)])


def _prep_tables(seq):
    pos = jnp.arange(seq)
    cos1, sin1 = _rope_tables(pos, HEAD_DIM)
    cos_r, sin_r = _rope_tables(pos // GRID_W, HEAD_DIM // 2)
    cos_c, sin_c = _rope_tables(pos % GRID_W, HEAD_DIM // 2)
    taba = _lane_tables(cos1, sin1, HEAD_DIM // 2)
    tabb = _lane_tables(jnp.concatenate([cos_r, cos_c], -1), jnp.concatenate([sin_r, sin_c], -1),
                        HEAD_DIM // 4)
    return taba, tabb


def _layer(h, p, taba, tabb):
    b, seq, d = h.shape
    n = b * seq
    x2d = h.reshape(n, d)
    w_in = p["w_in"]
    o1 = N_GROUPS_A * A_GROUP_W
    o2 = o1 + B_Q_W
    o3 = o2 + B_KV_W
    pair_heads = jnp.array([0, 4, 1, 5, 2, 6, 3, 7])
    pair_cols = (pair_heads[:, None] * HEAD_DIM + jnp.arange(HEAD_DIM)[None, :]).reshape(-1)
    wqkv = jnp.concatenate([w_in[:, :o1], w_in[:, o1:o2][:, pair_cols], w_in[:, o2:o3]], axis=1).astype(BF16)
    tabs = jnp.stack([
        _fold_gain(taba, p["q_norm_a"] * Q_SCALE, HEAD_DIM // 2),
        _fold_gain(taba, p["k_norm_a"], HEAD_DIM // 2),
        _fold_gain(tabb, p["q_norm_b"] * Q_SCALE, HEAD_DIM // 4),
        _fold_gain(tabb, p["k_norm_b"], HEAD_DIM // 4),
    ])
    seg = jnp.arange(2 * LANES) // HEAD_DIM
    bd = jnp.where(seg[:, None] == seg[None, :], 1.0 / HEAD_DIM, 0.0).astype(BF16)
    g1 = p["norm1_g"].reshape(1, d).astype(F32)

    a0, a1, a2, qb, kvb = _proj_call(x2d, g1, wqkv, tabs, bd, seq)
    ya = _attn_a_call(a0.reshape(b, seq, A_GROUP_W), a1.reshape(b, seq // 4, 4 * A_GROUP_W),
                      a2.reshape(b, seq // 16, 16 * A_GROUP_W))
    yb = _attn_b_call(qb.reshape(b, seq, B_Q_W), kvb.reshape(b, seq, B_KV_W))

    wg = w_in[:, o3:].astype(BF16)
    wbb = p["w_branch_b"][pair_cols, :].astype(BF16)
    wr = jnp.pad(p["w_router"], ((0, 0), (0, LANES - N_EXPERTS)))
    wrh = wr.astype(BF16)
    wrl = (wr - wrh.astype(F32)).astype(BF16)
    wsgu = jnp.concatenate([p["ws_gate"], p["ws_up"]], axis=1).astype(BF16)
    tri = (jnp.arange(ROUTE_TILE)[:, None] < jnp.arange(ROUTE_TILE)[None, :]).astype(BF16)
    wba = p["w_branch_a"].astype(BF16)
    wo = p["w_out"].astype(BF16)
    g2 = p["norm2_g"].reshape(1, d).astype(F32)
    bias = p["router_bias"].reshape(N_EXPERTS, 1).astype(F32)
    wsd = p["ws_down"].astype(BF16)
    weg = p["we_gate"].astype(BF16)
    weu = p["we_up"].astype(BF16)
    wd = p["we_down"].astype(BF16)
    ya2d = ya.reshape(n, A_OUT_W)
    yb2d = yb.reshape(n, B_Q_W)

    nc = n // MOE_CHUNKS
    bm = BM_EXPERT
    n_blocks = (nc * TOP_K + N_EXPERTS * (bm - 1)) // bm + 1
    n_blocks = -(-n_blocks // EXPERT_SUBBLOCKS) * EXPERT_SUBBLOCKS
    p_len = n_blocks * bm
    block_start = jnp.arange(n_blocks, dtype=I32) * bm
    out = None
    for c in range(MOE_CHUNKS):
        base, xp, eidx, rank, wgt, cnt = _merge_call(
            x2d, ya2d, yb2d, g1, wg, wba, wbb, wo, g2, wrh, wrl, bias, wsgu, wsd, tri, c * nc, nc)
        counts = cnt[:, 0].astype(I32)
        padded = (counts + bm - 1) // bm * bm
        pend = jnp.cumsum(padded)
        pstart = pend - padded
        onehot = eidx[:, :, None] == jnp.arange(N_EXPERTS)[None, None, :]
        dest = jnp.sum(jnp.where(onehot, pstart[None, None, :], 0), axis=-1) + rank
        block_e = jnp.minimum(jnp.sum(pend[None, :] <= block_start[:, None], axis=1),
                              N_EXPERTS - 1).astype(I32)
        n_used = (pend[-1] // bm).astype(I32).reshape(1)
        wbits = lax.bitcast_convert_type(wgt.astype(BF16), jnp.uint16).astype(U32)
        wbits = wbits | (wbits << 16)

        xs = _dispatch_rows(xp, dest, p_len)
        ys = _expert_call(block_e, n_used, xs, weg, weu, wd)
        routed = _reduce_rows(ys, dest, wbits)
        out = _final_add_call(base, routed, out, c * nc, n)
    return out.reshape(b, seq, d)


def kernel(x, norm1_g, w_in, q_norm_a, k_norm_a, q_norm_b, k_norm_b, w_branch_a, w_branch_b, w_out,
           norm2_g, w_router, router_bias, we_gate, we_up, we_down, ws_gate, ws_up, ws_down):
    params = dict(norm1_g=norm1_g, w_in=w_in, q_norm_a=q_norm_a, k_norm_a=k_norm_a, q_norm_b=q_norm_b,
                  k_norm_b=k_norm_b, w_branch_a=w_branch_a, w_branch_b=w_branch_b, w_out=w_out,
                  norm2_g=norm2_g, w_router=w_router, router_bias=router_bias, we_gate=we_gate,
                  we_up=we_up, we_down=we_down, ws_gate=ws_gate, ws_up=ws_up, ws_down=ws_down)
    taba, tabb = _prep_tables(x.shape[1])
    h = x
    for l in range(norm1_g.shape[0]):
        h = _layer(h, {k: v[l] for k, v in params.items()}, taba, tabb)
    return h
```

```python
import functools

import jax
import jax.numpy as jnp
from jax import lax
from jax.experimental import pallas as pl
from jax.experimental.pallas import tpu as pltpu
from jax.experimental.pallas import tpu_sc as plsc

F32 = jnp.float32
BF16 = jnp.bfloat16
I32 = jnp.int32
U32 = jnp.uint32

D_MODEL = 1024
HEAD_DIM = 64
ROPE_THETA = 10000.0
EPS = 1e-6
GRID_W = 64
DILATIONS = (1, 4, 16)
RADIUS = 64
N_GROUPS_A = 3
A_GROUP_W = 768
A_OUT_W = 256
B_Q_W = 512
B_KV_W = 256
QKV_W = N_GROUPS_A * A_GROUP_W + B_Q_W + B_KV_W
N_EXPERTS = 64
N_EXPERT_GROUPS = 8
TOPK_GROUPS = 4
TOP_K = 8
EXPERT_FF = 256
ROUTED_SCALE = 2.5

LANES = 128
NEG_BIG = -1e30
Q_SCALE = HEAD_DIM ** -0.5 * 1.4426950408889634

TM_PROJ = 1024
TQ_B = 512
TK_B = 512
TM_MERGE = 256
ROUTE_TILE = 256
BM_EXPERT = 512
TM_COMBINE = 512
MOE_CHUNKS = 2
VMEM_LIMIT = 56 * 1024 * 1024


def _cparams(sem):
    return pltpu.CompilerParams(dimension_semantics=sem, vmem_limit_bytes=VMEM_LIMIT)


def _rope_tables(pos, dim):
    inv = ROPE_THETA ** (-jnp.arange(0, dim, 2, dtype=F32) / dim)
    ang = pos.astype(F32)[:, None] * inv[None, :]
    ang = jnp.concatenate([ang, ang], axis=-1)
    return jnp.cos(ang), jnp.sin(ang)


def _lane_tables(cos_h, sin_h, half):
    cos2 = jnp.concatenate([cos_h, cos_h], axis=-1)
    sin2 = jnp.concatenate([sin_h, sin_h], axis=-1)
    first = (jnp.arange(LANES) % (2 * half)) < half
    s_left = jnp.where(first[None, :], -sin2, 0.0)
    s_right = jnp.where(first[None, :], 0.0, sin2)
    return jnp.stack([cos2, s_left, s_right]).astype(F32)


TAB_AQ, TAB_AK, TAB_BQ, TAB_BK = range(4)


def _proj_kernel(x_ref, g1_ref, w_ref, tab_ref, bd_ref,
                 a0_ref, a1_ref, a2_ref, qb_ref, kvb_ref, pbuf_ref):
    x = x_ref[...]
    ms = jnp.mean(x * x, axis=-1, keepdims=True)
    xn = (x * lax.rsqrt(ms + EPS) * g1_ref[...]).astype(BF16)
    tm = x.shape[0]

    def norm_rope(y, t, shift):
        w = y.shape[1]
        ss = jnp.dot((y * y).astype(BF16), bd_ref[0:w, 0:w], preferred_element_type=F32)
        yn = y * lax.rsqrt(ss + EPS)
        out = []
        for i in range(w // LANES):
            z = yn[:, i * LANES:(i + 1) * LANES]
            out.append(z * tab_ref[t, 0] + pltpu.roll(z, LANES - shift, 1) * tab_ref[t, 1]
                       + pltpu.roll(z, shift, 1) * tab_ref[t, 2])
        return out

    for g, d in enumerate(DILATIONS):
        base = g * A_GROUP_W
        r = jnp.dot(xn, w_ref[:, base:base + A_GROUP_W], preferred_element_type=F32)
        chunks = (norm_rope(r[:, 0:2 * LANES], TAB_AQ, HEAD_DIM // 2)
                  + norm_rope(r[:, 2 * LANES:4 * LANES], TAB_AK, HEAD_DIM // 2)
                  + [r[:, 4 * LANES:5 * LANES], r[:, 5 * LANES:6 * LANES]])
        for c, y in enumerate(chunks):
            if d == 1:
                a0_ref[:, c * LANES:(c + 1) * LANES] = y.astype(BF16)
            else:
                pbuf_ref[c] = y
        if d > 1:
            out_ref = a1_ref if g == 1 else a2_ref
            rows = tm // d
            for res in range(d):
                for c in range(A_GROUP_W // LANES):
                    col = res * A_GROUP_W + c * LANES
                    out_ref[:, col:col + LANES] = pbuf_ref[c, pl.ds(res, rows, stride=d), :].astype(BF16)

    base = N_GROUPS_A * A_GROUP_W
    r = jnp.dot(xn, w_ref[:, base:base + B_Q_W + B_KV_W], preferred_element_type=F32)
    chunks = (norm_rope(r[:, 0:2 * LANES], TAB_BQ, HEAD_DIM // 4)
              + norm_rope(r[:, 2 * LANES:4 * LANES], TAB_BQ, HEAD_DIM // 4)
              + norm_rope(r[:, 4 * LANES:5 * LANES], TAB_BK, HEAD_DIM // 4)
              + [r[:, 5 * LANES:6 * LANES]])
    for c, y in enumerate(chunks):
        if c < 4:
            qb_ref[:, c * LANES:(c + 1) * LANES] = y.astype(BF16)
        else:
            kvb_ref[:, (c - 4) * LANES:(c - 3) * LANES] = y.astype(BF16)


def _proj_call(x2d, g1, wqkv, tabs, bd, seq):
    n = x2d.shape[0]
    tm = TM_PROJ
    tiles_per_seq = seq // tm
    full = lambda shape: pl.BlockSpec(shape, lambda i: (0,) * len(shape))
    return pl.pallas_call(
        _proj_kernel,
        grid=(n // tm,),
        in_specs=[
            pl.BlockSpec((tm, D_MODEL), lambda i: (i, 0)),
            full((1, D_MODEL)),
            full((D_MODEL, QKV_W)),
            pl.BlockSpec((4, 3, tm, LANES), lambda i: (0, 0, i % tiles_per_seq, 0)),
            full((2 * LANES, 2 * LANES)),
        ],
        out_specs=[
            pl.BlockSpec((tm, A_GROUP_W), lambda i: (i, 0)),
            pl.BlockSpec((tm // 4, 4 * A_GROUP_W), lambda i: (i, 0)),
            pl.BlockSpec((tm // 16, 16 * A_GROUP_W), lambda i: (i, 0)),
            pl.BlockSpec((tm, B_Q_W), lambda i: (i, 0)),
            pl.BlockSpec((tm, B_KV_W), lambda i: (i, 0)),
        ],
        out_shape=[
            jax.ShapeDtypeStruct((n, A_GROUP_W), BF16),
            jax.ShapeDtypeStruct((n // 4, 4 * A_GROUP_W), BF16),
            jax.ShapeDtypeStruct((n // 16, 16 * A_GROUP_W), BF16),
            jax.ShapeDtypeStruct((n, B_Q_W), BF16),
            jax.ShapeDtypeStruct((n, B_KV_W), BF16),
        ],
        scratch_shapes=[pltpu.VMEM((A_GROUP_W // LANES, tm, LANES), F32)],
        compiler_params=_cparams(("arbitrary",)),
        name="proj_qkv",
    )(x2d, g1, wqkv, tabs, bd)


def _attn_a_kernel(a0_ref, a1_ref, a2_ref, bw_ref, bn_ref, ya_ref, acc_ref, m_ref, l_ref):
    tq = 128
    lane = lax.broadcasted_iota(I32, (tq, LANES), 1)
    low = lane < HEAD_DIM

    def tile(g, ref, col0, length, i, rows):
        win = min(2 * tq, length)
        if isinstance(i, int):
            q0 = i * tq
            ws = max(0, min(q0 - RADIUS, length - win))
        else:
            q0 = pl.multiple_of(i * tq, tq)
            ws = pl.multiple_of(jnp.clip(q0 - RADIUS, 0, length - win), RADIUS)
        if win == length:
            bias = bn_ref[...]
        elif isinstance(i, int):
            bias = bw_ref[(q0 - ws) // RADIUS]
        else:
            bias = bw_ref[lax.shift_right_logical(q0 - ws, 6)]
        for hp in range(2):
            q2 = ref[0, pl.ds(q0, tq), col0 + hp * LANES:col0 + (hp + 1) * LANES]
            k2 = ref[0, pl.ds(ws, win), col0 + 256 + hp * LANES:col0 + 256 + (hp + 1) * LANES]
            v2 = ref[0, pl.ds(ws, win), col0 + 512 + hp * LANES:col0 + 512 + (hp + 1) * LANES]
            zero = jnp.zeros_like(q2)
            qs = jnp.concatenate([jnp.where(low, q2, zero), jnp.where(low, zero, q2)], axis=0)
            s = lax.dot_general(qs, k2, (((1,), (1,)), ((), ())), preferred_element_type=F32)
            s = s + bias
            m = jnp.max(s, axis=-1, keepdims=True)
            p = jnp.exp2(s - m)
            l = jnp.sum(p, axis=-1, keepdims=True)
            pv = jnp.dot(p.astype(BF16), v2, preferred_element_type=F32)
            slot = 2 * g + hp
            acc_ref[slot, rows, :] = jnp.where(low, pv[:tq], pv[tq:])
            m_ref[slot, rows, :] = jnp.where(low, m[:tq], m[tq:])
            l_ref[slot, rows, :] = jnp.where(low, l[:tq], l[tq:])

    seq = a0_ref.shape[1]

    def g0_body(i, carry):
        tile(0, a0_ref, 0, seq, i, pl.ds(pl.multiple_of(i * tq, tq), tq))
        return carry
    lax.fori_loop(0, seq // tq, g0_body, 0, unroll=16)

    len1 = seq // 4
    for res in range(4):
        def g1_body(i, carry, res=res):
            tile(1, a1_ref, res * A_GROUP_W, len1, i, pl.ds(res + 4 * i * tq, tq, stride=4))
            return carry
        lax.fori_loop(0, len1 // tq, g1_body, 0, unroll=4)

    len2 = seq // 16
    for res in range(16):
        tile(2, a2_ref, res * A_GROUP_W, len2, 0, pl.ds(res, tq, stride=16))

    rc = 256

    def merge_body(j, carry):
        rows = pl.ds(pl.multiple_of(j * rc, rc), rc)
        for hp in range(2):
            ms = [m_ref[2 * g + hp, rows, :] for g in range(N_GROUPS_A)]
            mx = jnp.maximum(jnp.maximum(ms[0], ms[1]), ms[2])
            num = jnp.zeros((rc, LANES), F32)
            den = jnp.zeros((rc, LANES), F32)
            for g in range(N_GROUPS_A):
                e = jnp.exp2(ms[g] - mx)
                num = num + e * acc_ref[2 * g + hp, rows, :]
                den = den + e * l_ref[2 * g + hp, rows, :]
            ya_ref[0, rows, hp * LANES:(hp + 1) * LANES] = (num / den).astype(BF16)
        return carry
    lax.fori_loop(0, seq // rc, merge_body, 0)


def _band_bias(offset, win):
    q = jnp.arange(2 * 128) % 128 + offset
    k = jnp.arange(win)
    return jnp.where(jnp.abs(q[:, None] - k[None, :]) <= RADIUS, 0.0, NEG_BIG).astype(F32)


def _attn_a_call(a0, a1, a2):
    b, seq, _ = a0.shape
    bias_wide = jnp.stack([_band_bias(off, 256) for off in (0, RADIUS, 2 * RADIUS)])
    bias_narrow = _band_bias(0, 128)
    return pl.pallas_call(
        _attn_a_kernel,
        grid=(b,),
        in_specs=[
            pl.BlockSpec((1, seq, A_GROUP_W), lambda i: (i, 0, 0)),
            pl.BlockSpec((1, seq // 4, 4 * A_GROUP_W), lambda i: (i, 0, 0)),
            pl.BlockSpec((1, seq // 16, 16 * A_GROUP_W), lambda i: (i, 0, 0)),
            pl.BlockSpec((3, 256, 256), lambda i: (0, 0, 0)),
            pl.BlockSpec((256, 128), lambda i: (0, 0)),
        ],
        out_specs=pl.BlockSpec((1, seq, A_OUT_W), lambda i: (i, 0, 0)),
        out_shape=jax.ShapeDtypeStruct((b, seq, A_OUT_W), BF16),
        scratch_shapes=[pltpu.VMEM((N_GROUPS_A * A_OUT_W // LANES, seq, LANES), F32)] * 3,
        compiler_params=_cparams(("arbitrary",)),
        name="attn_a",
    )(a0, a1, a2, bias_wide, bias_narrow)


def _attn_b_kernel(q_ref, kv_ref, o_ref, va_ref, vb_ref):
    tq = q_ref.shape[1]
    seq = kv_ref.shape[1]
    low = lax.broadcasted_iota(I32, (tq, LANES), 1) < HEAD_DIM

    @pl.when(pl.program_id(1) == 0)
    def _():
        low_k = lax.broadcasted_iota(I32, (seq, LANES), 1) < HEAD_DIM
        v2 = kv_ref[0, :, LANES:2 * LANES]
        one = jnp.ones_like(v2)
        va_ref[...] = jnp.where(low_k, v2, one)
        vb_ref[...] = jnp.where(low_k, one, v2)

    for pr in range(B_Q_W // LANES):
        q2 = q_ref[0, :, pr * LANES:(pr + 1) * LANES]
        zero = jnp.zeros_like(q2)
        qs = jnp.concatenate([jnp.where(low, q2, zero), jnp.where(low, zero, q2)], axis=0)
        m = jnp.full((2 * tq, 1), NEG_BIG, F32)
        acc_a = jnp.zeros((tq, LANES), F32)
        acc_b = jnp.zeros((tq, LANES), F32)
        for c in range(seq // TK_B):
            k2 = kv_ref[0, c * TK_B:(c + 1) * TK_B, 0:LANES]
            va = va_ref[c * TK_B:(c + 1) * TK_B, :]
            vb = vb_ref[c * TK_B:(c + 1) * TK_B, :]
            s = lax.dot_general(qs, k2, (((1,), (1,)), ((), ())), preferred_element_type=F32)
            m_n = jnp.maximum(m, jnp.max(s, axis=-1, keepdims=True))
            p = jnp.exp2(s - m_n).astype(BF16)
            a = jnp.exp2(m - m_n)
            acc_a = a[:tq] * acc_a + jnp.dot(p[:tq], va, preferred_element_type=F32)
            acc_b = a[tq:] * acc_b + jnp.dot(p[tq:], vb, preferred_element_type=F32)
            m = m_n
        oa = acc_a / pltpu.roll(acc_a, HEAD_DIM, 1)
        ob = acc_b / pltpu.roll(acc_b, HEAD_DIM, 1)
        o_ref[0, :, pr * LANES:(pr + 1) * LANES] = jnp.where(low, oa, ob).astype(BF16)


def _attn_b_call(qb, kvb):
    b, seq, _ = qb.shape
    return pl.pallas_call(
        _attn_b_kernel,
        grid=(b, seq // TQ_B),
        in_specs=[
            pl.BlockSpec((1, TQ_B, B_Q_W), lambda i, j: (i, j, 0)),
            pl.BlockSpec((1, seq, B_KV_W), lambda i, j: (i, 0, 0)),
        ],
        out_specs=pl.BlockSpec((1, TQ_B, B_Q_W), lambda i, j: (i, j, 0)),
        out_shape=jax.ShapeDtypeStruct((b, seq, B_Q_W), BF16),
        scratch_shapes=[pltpu.VMEM((seq, LANES), BF16)] * 2,
        compiler_params=_cparams(("arbitrary",) * 2),
        name="attn_b",
    )(qb, kvb)


def _pack_bf16_pairs(v):
    half = v.shape[1] // 2
    lo = lax.bitcast_convert_type(v[:, :half].astype(BF16).astype(F32), U32)
    hi = lax.bitcast_convert_type(v[:, half:].astype(BF16).astype(F32), U32)
    return (lo >> 16) | (hi & jnp.uint32(0xFFFF0000))


def _unpack_bf16_pairs(p):
    lo = lax.bitcast_convert_type(p << 16, F32)
    hi = lax.bitcast_convert_type(p & jnp.uint32(0xFFFF0000), F32)
    return lo, hi


def _merge_kernel(x_ref, ya_ref, yb_ref, g1_ref, wg_ref, wba_ref, wbb_ref, wo_ref, g2_ref,
                  wrh_ref, wrl_ref, bias_ref, wsgu_ref, wsd_ref, tri_ref,
                  base_ref, xp_ref, eidx_ref, rank_ref, wgt_ref, cnt_ref, carry_ref):
    step = pl.program_id(0)

    @pl.when(step == 0)
    def _():
        carry_ref[...] = jnp.zeros_like(carry_ref)

    x = x_ref[...]
    tm = x.shape[0]
    ms = jnp.mean(x * x, axis=-1, keepdims=True)
    xn = (x * lax.rsqrt(ms + EPS) * g1_ref[...]).astype(BF16)
    gates = jax.nn.sigmoid(jnp.dot(xn, wg_ref[...], preferred_element_type=F32))
    pa = jnp.dot(ya_ref[...], wba_ref[...], preferred_element_type=F32)
    pb = jnp.dot(yb_ref[...], wbb_ref[...], preferred_element_type=F32)
    merged = gates[:, :D_MODEL] * pa + gates[:, D_MODEL:] * pb
    h = x + jnp.dot(merged.astype(BF16), wo_ref[...], preferred_element_type=F32)

    ms2 = jnp.mean(h * h, axis=-1, keepdims=True)
    xn2 = h * lax.rsqrt(ms2 + EPS) * g2_ref[...]
    xn2b = xn2.astype(BF16)
    xp_ref[...] = _pack_bf16_pairs(xn2)

    gu = jnp.dot(xn2b, wsgu_ref[...], preferred_element_type=F32)
    hs = (jax.nn.silu(gu[:, :EXPERT_FF]) * gu[:, EXPERT_FF:]).astype(BF16)
    base_ref[...] = h + jnp.dot(hs, wsd_ref[...], preferred_element_type=F32)

    xlo = (xn2 - xn2b.astype(F32)).astype(BF16)
    logits = (jnp.dot(xn2b, wrh_ref[...], preferred_element_type=F32)
              + jnp.dot(xlo, wrh_ref[...], preferred_element_type=F32)
              + jnp.dot(xn2b, wrl_ref[...], preferred_element_type=F32))
    logits_t = logits.T[:N_EXPERTS, :]
    for h in range(tm // ROUTE_TILE):
        cols = slice(h * ROUTE_TILE, (h + 1) * ROUTE_TILE)
        _route_tile(logits_t[:, cols], bias_ref, tri_ref, carry_ref,
                    eidx_ref, rank_ref, wgt_ref, cols)
    cnt_ref[...] = carry_ref[...]


def _route_tile(logits_t, bias_ref, tri_ref, carry_ref, eidx_ref, rank_ref, wgt_ref, cols):
    tm = logits_t.shape[1]
    scores = jax.nn.sigmoid(logits_t)
    sel = scores + bias_ref[...]

    gsz = N_EXPERTS // N_EXPERT_GROUPS
    iota8 = lax.broadcasted_iota(I32, (gsz, tm), 0).astype(F32)
    ninf = jnp.float32(-jnp.inf)
    blocks, gs = [], []
    for g in range(N_EXPERT_GROUPS):
        blk = sel[g * gsz:(g + 1) * gsz, :]
        m1 = jnp.max(blk, axis=0, keepdims=True)
        first = jnp.min(jnp.where(blk == m1, iota8, float(gsz)), axis=0, keepdims=True)
        m2 = jnp.max(jnp.where(iota8 == first, ninf, blk), axis=0, keepdims=True)
        blocks.append(blk)
        gs.append(m1 + m2)
    cur = jnp.concatenate(gs, axis=0)
    gmask = jnp.zeros((N_EXPERT_GROUPS, tm), F32)
    for _ in range(TOPK_GROUPS):
        mx = jnp.max(cur, axis=0, keepdims=True)
        fi = jnp.min(jnp.where(cur == mx, iota8, float(N_EXPERT_GROUPS)), axis=0, keepdims=True)
        pick = iota8 == fi
        gmask = jnp.where(pick, 1.0, gmask)
        cur = jnp.where(pick, ninf, cur)
    cur = jnp.concatenate(
        [jnp.where(gmask[g:g + 1, :] > 0.5, blocks[g], ninf) for g in range(N_EXPERT_GROUPS)], axis=0)

    iota64 = lax.broadcasted_iota(I32, (N_EXPERTS, tm), 0).astype(F32)
    idxs, wks = [], []
    onehot = jnp.zeros((N_EXPERTS, tm), F32)
    for _ in range(TOP_K):
        mx = jnp.max(cur, axis=0, keepdims=True)
        fi = jnp.min(jnp.where(cur == mx, iota64, float(N_EXPERTS)), axis=0, keepdims=True)
        pick = iota64 == fi
        wks.append(jnp.sum(jnp.where(pick, scores, 0.0), axis=0, keepdims=True))
        idxs.append(fi)
        onehot = jnp.where(pick, 1.0, onehot)
        cur = jnp.where(pick, ninf, cur)
    wk = jnp.concatenate(wks, axis=0)
    wgt_ref[:, cols] = wk / jnp.sum(wk, axis=0, keepdims=True) * ROUTED_SCALE
    eidx_ref[:, cols] = jnp.concatenate(idxs, axis=0).astype(I32)

    before = jnp.dot(onehot.astype(BF16), tri_ref[...], preferred_element_type=F32) + carry_ref[:, 0:1]
    rank_ref[:, cols] = jnp.concatenate(
        [jnp.sum(jnp.where(iota64 == fi, before, 0.0), axis=0, keepdims=True) for fi in idxs],
        axis=0).astype(I32)
    carry_ref[...] = carry_ref[...] + jnp.sum(onehot, axis=1, keepdims=True)


def _merge_call(x2d, ya, yb, g1, wg, wba, wbb, wo, g2, wrh, wrl, bias, wsgu, wsd, tri, row0, n):
    tm = TM_MERGE
    off = row0 // tm
    full = lambda shape: pl.BlockSpec(shape, lambda i: (0,) * len(shape))
    row_in = lambda w: pl.BlockSpec((tm, w), lambda i: (i + off, 0))
    row = lambda w: pl.BlockSpec((tm, w), lambda i: (i, 0))
    col = lambda: pl.BlockSpec((TOP_K, tm), lambda i: (0, i))
    return pl.pallas_call(
        _merge_kernel,
        grid=(n // tm,),
        in_specs=[
            row_in(D_MODEL), row_in(A_OUT_W), row_in(B_Q_W),
            full((1, D_MODEL)), full((D_MODEL, 2 * D_MODEL)),
            full((A_OUT_W, D_MODEL)), full((B_Q_W, D_MODEL)), full((D_MODEL, D_MODEL)),
            full((1, D_MODEL)),
            full((D_MODEL, LANES)), full((D_MODEL, LANES)), full((N_EXPERTS, 1)),
            full((D_MODEL, 2 * EXPERT_FF)), full((EXPERT_FF, D_MODEL)),
            full((ROUTE_TILE, ROUTE_TILE)),
        ],
        out_specs=[row(D_MODEL), row(D_MODEL // 2), col(), col(), col(), full((N_EXPERTS, LANES))],
        out_shape=[
            jax.ShapeDtypeStruct((n, D_MODEL), F32),
            jax.ShapeDtypeStruct((n, D_MODEL // 2), U32),
            jax.ShapeDtypeStruct((TOP_K, n), I32),
            jax.ShapeDtypeStruct((TOP_K, n), I32),
            jax.ShapeDtypeStruct((TOP_K, n), F32),
            jax.ShapeDtypeStruct((N_EXPERTS, LANES), F32),
        ],
        scratch_shapes=[pltpu.VMEM((N_EXPERTS, LANES), F32)],
        compiler_params=_cparams(("arbitrary",)),
        name="merge_router",
    )(x2d, ya, yb, g1, wg, wba, wbb, wo, g2, wrh, wrl, bias, wsgu, wsd, tri)


EXPERT_SUBBLOCKS = 4


X_SLOTS = 3
Y_SLOTS = 2


def _expert_kernel(be_ref, nb_ref, xs_hbm, *refs):
    nw = 3 * EXPERT_SUBBLOCKS
    w_refs = refs[:nw]
    ys_hbm, xbuf, ybuf, xsem, ysem = refs[nw:]
    j = pl.program_id(0)
    nsteps = pl.num_programs(0)
    rows = EXPERT_SUBBLOCKS * BM_EXPERT

    def x_copy(step, slot):
        src = xs_hbm.at[pl.ds(pl.multiple_of(step * rows, rows), rows)]
        return pltpu.make_async_copy(src, xbuf.at[slot], xsem.at[slot])

    def y_copy(step, slot):
        dst = ys_hbm.at[pl.ds(pl.multiple_of(step * rows, rows), rows)]
        return pltpu.make_async_copy(ybuf.at[slot], dst, ysem.at[slot])

    @pl.when(j == 0)
    def _():
        x_copy(0, 0).start()
        x_copy(1, 1).start()

    @pl.when(j + 2 < nsteps)
    def _():
        x_copy(j + 2, lax.rem(j + 2, X_SLOTS)).start()

    xslot = lax.rem(j, X_SLOTS)
    yslot = lax.rem(j, Y_SLOTS)
    x_copy(j, xslot).wait()

    @pl.when(j >= Y_SLOTS)
    def _():
        y_copy(j - Y_SLOTS, yslot).wait()

    first = j * EXPERT_SUBBLOCKS

    @pl.when(first >= nb_ref[0])
    def _():
        ybuf[yslot] = jnp.zeros((rows, D_MODEL // 2), U32)

    @pl.when(first < nb_ref[0])
    def _():
        half = D_MODEL // 2
        for sub in range(EXPERT_SUBBLOCKS):
            wg_ref, wu_ref, wd_ref = w_refs[3 * sub:3 * sub + 3]
            blk = slice(sub * BM_EXPERT, (sub + 1) * BM_EXPERT)
            lo, hi = _unpack_bf16_pairs(xbuf[xslot, blk, :])
            lo = lo.astype(BF16)
            hi = hi.astype(BF16)
            gate = (jnp.dot(lo, wg_ref[0, :half, :], preferred_element_type=F32)
                    + jnp.dot(hi, wg_ref[0, half:, :], preferred_element_type=F32))
            up = (jnp.dot(lo, wu_ref[0, :half, :], preferred_element_type=F32)
                  + jnp.dot(hi, wu_ref[0, half:, :], preferred_element_type=F32))
            hb = (jax.nn.silu(gate) * up).astype(BF16)
            ybuf[yslot, blk, :] = _pack_bf16_pairs(jnp.dot(hb, wd_ref[0], preferred_element_type=F32))

    y_copy(j, yslot).start()

    @pl.when(j == nsteps - 1)
    def _():
        y_copy(j - 1, lax.rem(j - 1, Y_SLOTS)).wait()
        y_copy(j, yslot).wait()


def _expert_call(block_e, n_used, xs, wg, wu, wd):
    p_len = xs.shape[0]
    nsub = EXPERT_SUBBLOCKS
    rows = nsub * BM_EXPERT
    w_specs, w_args = [], []
    for sub in range(nsub):
        pick = lambda j, be, nb, sub=sub: (be[nsub * j + sub], 0, 0)
        w_specs += [
            pl.BlockSpec((1, D_MODEL, EXPERT_FF), pick),
            pl.BlockSpec((1, D_MODEL, EXPERT_FF), pick),
            pl.BlockSpec((1, EXPERT_FF, D_MODEL), pick),
        ]
        w_args += [wg, wu, wd]
    assert p_len // rows >= max(X_SLOTS, Y_SLOTS)
    return pl.pallas_call(
        _expert_kernel,
        grid_spec=pltpu.PrefetchScalarGridSpec(
            num_scalar_prefetch=2,
            grid=(p_len // rows,),
            in_specs=[pl.BlockSpec(memory_space=pl.ANY)] + w_specs,
            out_specs=pl.BlockSpec(memory_space=pl.ANY),
            scratch_shapes=[
                pltpu.VMEM((X_SLOTS, rows, D_MODEL // 2), U32),
                pltpu.VMEM((Y_SLOTS, rows, D_MODEL // 2), U32),
                pltpu.SemaphoreType.DMA((X_SLOTS,)),
                pltpu.SemaphoreType.DMA((Y_SLOTS,)),
            ],
        ),
        out_shape=jax.ShapeDtypeStruct((p_len, D_MODEL // 2), U32),
        compiler_params=_cparams(("arbitrary",)),
        name="routed_experts",
    )(block_e, n_used, xs, *w_args)


SC_CORES = 2
SC_SUBCORES = 16
SC_WORKERS = SC_CORES * SC_SUBCORES
SC_CHUNK = 128


def _sc_mesh():
    return plsc.VectorSubcoreMesh(core_axis_name="c", subcore_axis_name="s",
                                  num_cores=SC_CORES, num_subcores=SC_SUBCORES)


def _dispatch_rows(xp, dest, p_len):
    n, width = xp.shape
    per_worker = n // SC_CHUNK // SC_WORKERS

    @functools.partial(
        pl.kernel, mesh=_sc_mesh(),
        out_type=jax.ShapeDtypeStruct((p_len, width), xp.dtype),
        scratch_types=[pltpu.VMEM((TOP_K, SC_CHUNK), I32), pltpu.VMEM((SC_CHUNK, width), xp.dtype),
                       pltpu.SemaphoreType.DMA],
        name="sc_dispatch")
    def body(xp_hbm, dest_hbm, xs_hbm, idx_v, rows_v, sem):
        wid = lax.axis_index("s") * SC_CORES + lax.axis_index("c")

        @pl.loop(0, per_worker)
        def _(j):
            chunk = wid * per_worker + j
            pltpu.sync_copy(dest_hbm.at[:, pl.ds(chunk * SC_CHUNK, SC_CHUNK)], idx_v)
            pltpu.sync_copy(xp_hbm.at[pl.ds(chunk * SC_CHUNK, SC_CHUNK)], rows_v)
            copies = [pltpu.async_copy(rows_v, xs_hbm.at[idx_v.at[k]], sem) for k in range(TOP_K)]
            for cp in copies:
                cp.wait()

    return body(xp, dest)


SC_LANES = 16
SC_REDUCE_CHUNK = 8


def _reduce_rows(ys, dest, wbits):
    width = ys.shape[1]
    tc = SC_REDUCE_CHUNK
    n = dest.shape[1]
    tokens = n // SC_WORKERS
    per_worker = tokens // tc
    assert per_worker % 2 == 0 and 2 * tc == SC_LANES

    @functools.partial(
        pl.kernel, mesh=_sc_mesh(),
        out_type=jax.ShapeDtypeStruct((n, width), ys.dtype),
        scratch_types=[pltpu.VMEM((TOP_K, tokens), I32),
                       pltpu.VMEM((TOP_K, tokens), ys.dtype),
                       pltpu.VMEM((2, TOP_K, tc, width), ys.dtype),
                       pltpu.VMEM((2, tc, width), ys.dtype),
                       pltpu.SemaphoreType.DMA, pltpu.SemaphoreType.DMA,
                       pltpu.SemaphoreType.DMA, pltpu.SemaphoreType.DMA],
        compiler_params=pltpu.CompilerParams(needs_layout_passes=False),
        name="sc_reduce")
    def body(ys_hbm, dest_hbm, w_hbm, out_hbm, idx_v, w_v, rows_v, r_v, gsem0, gsem1, osem0, osem1):
        wid = lax.axis_index("s") * SC_CORES + lax.axis_index("c")
        first = wid * tokens
        gsems, osems = (gsem0, gsem1), (osem0, osem1)
        pltpu.sync_copy(dest_hbm.at[:, pl.ds(first, tokens)], idx_v)
        pltpu.sync_copy(w_hbm.at[:, pl.ds(first, tokens)], w_v)

        def fetches(j, slot):
            return [pltpu.make_async_copy(ys_hbm.at[idx_v.at[k, pl.ds(j * tc, tc)]], rows_v.at[slot, k],
                                          gsems[slot])
                    for k in range(TOP_K)]

        def write_back(j, slot):
            return pltpu.make_async_copy(r_v.at[slot], out_hbm.at[pl.ds(first + j * tc, tc)],
                                         osems[slot])

        def reduce_item(p, slot):
            wrows = [w_v[k, pl.ds(pl.multiple_of(p * SC_LANES, SC_LANES), SC_LANES)] for k in range(TOP_K)]

            @pl.loop(0, tc)
            def _(t):
                lane = jnp.full((SC_LANES,), slot * tc + t, I32)
                ws = [plsc.bitcast(wrows[k].at[lane].get(mode="promise_in_bounds"), BF16)
                      for k in range(TOP_K)]

                @pl.loop(0, width // SC_LANES)
                def _(jv):
                    col = pl.multiple_of(jv * SC_LANES, SC_LANES)
                    terms = [plsc.bitcast(rows_v[slot, k, t, pl.ds(col, SC_LANES)], BF16) * ws[k]
                             for k in range(TOP_K)]
                    while len(terms) > 1:
                        terms = [terms[i] + terms[i + 1] for i in range(0, len(terms), 2)]
                    r_v[slot, t, pl.ds(col, SC_LANES)] = plsc.bitcast(terms[0], ys.dtype)

        for cp in fetches(0, 0):
            cp.start()

        @pl.loop(0, per_worker // 2)
        def _(p):
            for slot in range(2):
                j = 2 * p + slot
                @pl.when(j + 1 < per_worker)
                def _():
                    for cp in fetches(j + 1, 1 - slot):
                        cp.start()
                for cp in fetches(j, slot):
                    cp.wait()

                @pl.when(p > 0)
                def _():
                    write_back(j - 2, slot).wait()
                reduce_item(p, slot)
                write_back(j, slot).start()

        write_back(per_worker - 2, 0).wait()
        write_back(per_worker - 1, 1).wait()

    return body(ys, dest, wbits)


def _final_add_kernel(base_ref, r_ref, *refs):
    o_ref = refs[-1]
    half = D_MODEL // 2
    lo, hi = _unpack_bf16_pairs(r_ref[...])
    o_ref[:, :half] = base_ref[:, :half] + lo
    o_ref[:, half:] = base_ref[:, half:] + hi


def _final_add_call(base, routed, prev, row0, n_total):
    n = base.shape[0]
    tm = TM_COMBINE
    off = row0 // tm
    in_specs = [pl.BlockSpec((tm, D_MODEL), lambda i: (i, 0)),
                pl.BlockSpec((tm, D_MODEL // 2), lambda i: (i, 0))]
    args, aliases = [base, routed], {}
    if prev is not None:
        in_specs.append(pl.BlockSpec(memory_space=pl.ANY))
        args.append(prev)
        aliases = {2: 0}
    return pl.pallas_call(
        _final_add_kernel,
        grid=(n // tm,),
        in_specs=in_specs,
        out_specs=pl.BlockSpec((tm, D_MODEL), lambda i: (i + off, 0)),
        out_shape=jax.ShapeDtypeStruct((n_total, D_MODEL), F32),
        input_output_aliases=aliases,
        compiler_params=_cparams(("arbitrary",)),
        name="moe_final_add",
    )(*args)


def _fold_gain(tab, gain, half):
    g = jnp.tile(gain.astype(F32), LANES // HEAD_DIM)
    return jnp.stack([tab[0] * g, tab[1] * jnp.roll(g, LANES - half), tab[2] * jnp.roll(g, half)])


def _prep_tables(seq):
    pos = jnp.arange(seq)
    cos1, sin1 = _rope_tables(pos, HEAD_DIM)
    cos_r, sin_r = _rope_tables(pos // GRID_W, HEAD_DIM // 2)
    cos_c, sin_c = _rope_tables(pos % GRID_W, HEAD_DIM // 2)
    taba = _lane_tables(cos1, sin1, HEAD_DIM // 2)
    tabb = _lane_tables(jnp.concatenate([cos_r, cos_c], -1), jnp.concatenate([sin_r, sin_c], -1),
                        HEAD_DIM // 4)
    return taba, tabb


def _layer(h, p, taba, tabb):
    b, seq, d = h.shape
    n = b * seq
    x2d = h.reshape(n, d)
    w_in = p["w_in"]
    o1 = N_GROUPS_A * A_GROUP_W
    o2 = o1 + B_Q_W
    o3 = o2 + B_KV_W
    pair_heads = jnp.array([0, 4, 1, 5, 2, 6, 3, 7])
    pair_cols = (pair_heads[:, None] * HEAD_DIM + jnp.arange(HEAD_DIM)[None, :]).reshape(-1)
    wqkv = jnp.concatenate([w_in[:, :o1], w_in[:, o1:o2][:, pair_cols], w_in[:, o2:o3]], axis=1).astype(BF16)
    tabs = jnp.stack([
        _fold_gain(taba, p["q_norm_a"] * Q_SCALE, HEAD_DIM // 2),
        _fold_gain(taba, p["k_norm_a"], HEAD_DIM // 2),
        _fold_gain(tabb, p["q_norm_b"] * Q_SCALE, HEAD_DIM // 4),
        _fold_gain(tabb, p["k_norm_b"], HEAD_DIM // 4),
    ])
    seg = jnp.arange(2 * LANES) // HEAD_DIM
    bd = jnp.where(seg[:, None] == seg[None, :], 1.0 / HEAD_DIM, 0.0).astype(BF16)
    g1 = p["norm1_g"].reshape(1, d).astype(F32)

    a0, a1, a2, qb, kvb = _proj_call(x2d, g1, wqkv, tabs, bd, seq)
    ya = _attn_a_call(a0.reshape(b, seq, A_GROUP_W), a1.reshape(b, seq // 4, 4 * A_GROUP_W),
                      a2.reshape(b, seq // 16, 16 * A_GROUP_W))
    yb = _attn_b_call(qb.reshape(b, seq, B_Q_W), kvb.reshape(b, seq, B_KV_W))

    wg = w_in[:, o3:].astype(BF16)
    wbb = p["w_branch_b"][pair_cols, :].astype(BF16)
    wr = jnp.pad(p["w_router"], ((0, 0), (0, LANES - N_EXPERTS)))
    wrh = wr.astype(BF16)
    wrl = (wr - wrh.astype(F32)).astype(BF16)
    wsgu = jnp.concatenate([p["ws_gate"], p["ws_up"]], axis=1).astype(BF16)
    tri = (jnp.arange(ROUTE_TILE)[:, None] < jnp.arange(ROUTE_TILE)[None, :]).astype(BF16)
    wba = p["w_branch_a"].astype(BF16)
    wo = p["w_out"].astype(BF16)
    g2 = p["norm2_g"].reshape(1, d).astype(F32)
    bias = p["router_bias"].reshape(N_EXPERTS, 1).astype(F32)
    wsd = p["ws_down"].astype(BF16)
    weg = p["we_gate"].astype(BF16)
    weu = p["we_up"].astype(BF16)
    wd = p["we_down"].astype(BF16)
    ya2d = ya.reshape(n, A_OUT_W)
    yb2d = yb.reshape(n, B_Q_W)

    nc = n // MOE_CHUNKS
    bm = BM_EXPERT
    n_blocks = (nc * TOP_K + N_EXPERTS * (bm - 1)) // bm + 1
    n_blocks = -(-n_blocks // EXPERT_SUBBLOCKS) * EXPERT_SUBBLOCKS
    p_len = n_blocks * bm
    block_start = jnp.arange(n_blocks, dtype=I32) * bm
    out = None
    for c in range(MOE_CHUNKS):
        base, xp, eidx, rank, wgt, cnt = _merge_call(
            x2d, ya2d, yb2d, g1, wg, wba, wbb, wo, g2, wrh, wrl, bias, wsgu, wsd, tri, c * nc, nc)
        counts = cnt[:, 0].astype(I32)
        padded = (counts + bm - 1) // bm * bm
        pend = jnp.cumsum(padded)
        pstart = pend - padded
        onehot = eidx[:, :, None] == jnp.arange(N_EXPERTS)[None, None, :]
        dest = jnp.sum(jnp.where(onehot, pstart[None, None, :], 0), axis=-1) + rank
        block_e = jnp.minimum(jnp.sum(pend[None, :] <= block_start[:, None], axis=1),
                              N_EXPERTS - 1).astype(I32)
        n_used = (pend[-1] // bm).astype(I32).reshape(1)
        wbits = lax.bitcast_convert_type(wgt.astype(BF16), jnp.uint16).astype(U32)
        wbits = wbits | (wbits << 16)

        xs = _dispatch_rows(xp, dest, p_len)
        ys = _expert_call(block_e, n_used, xs, weg, weu, wd)
        routed = _reduce_rows(ys, dest, wbits)
        out = _final_add_call(base, routed, out, c * nc, n)
    return out.reshape(b, seq, d)


def kernel(x, norm1_g, w_in, q_norm_a, k_norm_a, q_norm_b, k_norm_b, w_branch_a, w_branch_b, w_out,
           norm2_g, w_router, router_bias, we_gate, we_up, we_down, ws_gate, ws_up, ws_down):
    params = dict(norm1_g=norm1_g, w_in=w_in, q_norm_a=q_norm_a, k_norm_a=k_norm_a, q_norm_b=q_norm_b,
                  k_norm_b=k_norm_b, w_branch_a=w_branch_a, w_branch_b=w_branch_b, w_out=w_out,
                  norm2_g=norm2_g, w_router=w_router, router_bias=router_bias, we_gate=we_gate,
                  we_up=we_up, we_down=we_down, ws_gate=ws_gate, ws_up=ws_up, ws_down=ws_down)
    taba, tabb = _prep_tables(x.shape[1])
    h = x
    for l in range(norm1_g.shape[0]):
        h = _layer(h, {k: v[l] for k, v in params.items()}, taba, tabb)
    return h
```

```python
import functools

import jax
import jax.numpy as jnp
from jax import lax
from jax.experimental import pallas as pl
from jax.experimental.pallas import tpu as pltpu
from jax.experimental.pallas import tpu_sc as plsc

F32 = jnp.float32
BF16 = jnp.bfloat16
I32 = jnp.int32
U32 = jnp.uint32

D_MODEL = 1024
HEAD_DIM = 64
ROPE_THETA = 10000.0
EPS = 1e-6
GRID_W = 64
DILATIONS = (1, 4, 16)
RADIUS = 64
N_GROUPS_A = 3
A_GROUP_W = 768
A_OUT_W = 256
B_Q_W = 512
B_KV_W = 256
QKV_W = N_GROUPS_A * A_GROUP_W + B_Q_W + B_KV_W
N_EXPERTS = 64
N_EXPERT_GROUPS = 8
TOPK_GROUPS = 4
TOP_K = 8
EXPERT_FF = 256
ROUTED_SCALE = 2.5

LANES = 128
NEG_BIG = -1e30
Q_SCALE = HEAD_DIM ** -0.5 * 1.4426950408889634

TM_PROJ = 1024
TQ_B = 512
TK_B = 512
TM_MERGE = 512
ROUTE_TILE = 256
BM_EXPERT = 512
TM_COMBINE = 512
MOE_CHUNKS = 2
VMEM_LIMIT = 56 * 1024 * 1024


def _cparams(sem):
    return pltpu.CompilerParams(dimension_semantics=sem, vmem_limit_bytes=VMEM_LIMIT)


def _rope_tables(pos, dim):
    inv = ROPE_THETA ** (-jnp.arange(0, dim, 2, dtype=F32) / dim)
    ang = pos.astype(F32)[:, None] * inv[None, :]
    ang = jnp.concatenate([ang, ang], axis=-1)
    return jnp.cos(ang), jnp.sin(ang)


def _lane_tables(cos_h, sin_h, half):
    cos2 = jnp.concatenate([cos_h, cos_h], axis=-1)
    sin2 = jnp.concatenate([sin_h, sin_h], axis=-1)
    first = (jnp.arange(LANES) % (2 * half)) < half
    s_left = jnp.where(first[None, :], -sin2, 0.0)
    s_right = jnp.where(first[None, :], 0.0, sin2)
    return jnp.stack([cos2, s_left, s_right]).astype(F32)


TAB_AQ, TAB_AK, TAB_BQ, TAB_BK = range(4)


def _proj_kernel(x_ref, g1_ref, w_ref, tab_ref, bd_ref,
                 a0_ref, a1_ref, a2_ref, qb_ref, kvb_ref, pbuf_ref):
    x = x_ref[...]
    ms = jnp.mean(x * x, axis=-1, keepdims=True)
    xn = (x * lax.rsqrt(ms + EPS) * g1_ref[...]).astype(BF16)
    tm = x.shape[0]

    def norm_rope(y, t, shift):
        w = y.shape[1]
        ss = jnp.dot((y * y).astype(BF16), bd_ref[0:w, 0:w], preferred_element_type=F32)
        yn = y * lax.rsqrt(ss + EPS)
        out = []
        for i in range(w // LANES):
            z = yn[:, i * LANES:(i + 1) * LANES]
            out.append(z * tab_ref[t, 0] + pltpu.roll(z, LANES - shift, 1) * tab_ref[t, 1]
                       + pltpu.roll(z, shift, 1) * tab_ref[t, 2])
        return out

    for g, d in enumerate(DILATIONS):
        base = g * A_GROUP_W
        r = jnp.dot(xn, w_ref[:, base:base + A_GROUP_W], preferred_element_type=F32)
        chunks = (norm_rope(r[:, 0:2 * LANES], TAB_AQ, HEAD_DIM // 2)
                  + norm_rope(r[:, 2 * LANES:4 * LANES], TAB_AK, HEAD_DIM // 2)
                  + [r[:, 4 * LANES:5 * LANES], r[:, 5 * LANES:6 * LANES]])
        for c, y in enumerate(chunks):
            if d == 1:
                a0_ref[:, c * LANES:(c + 1) * LANES] = y.astype(BF16)
            else:
                pbuf_ref[c] = y
        if d > 1:
            out_ref = a1_ref if g == 1 else a2_ref
            rows = tm // d
            for res in range(d):
                for c in range(A_GROUP_W // LANES):
                    col = res * A_GROUP_W + c * LANES
                    out_ref[:, col:col + LANES] = pbuf_ref[c, pl.ds(res, rows, stride=d), :].astype(BF16)

    base = N_GROUPS_A * A_GROUP_W
    r = jnp.dot(xn, w_ref[:, base:base + B_Q_W + B_KV_W], preferred_element_type=F32)
    chunks = (norm_rope(r[:, 0:2 * LANES], TAB_BQ, HEAD_DIM // 4)
              + norm_rope(r[:, 2 * LANES:4 * LANES], TAB_BQ, HEAD_DIM // 4)
              + norm_rope(r[:, 4 * LANES:5 * LANES], TAB_BK, HEAD_DIM // 4)
              + [r[:, 5 * LANES:6 * LANES]])
    for c, y in enumerate(chunks):
        if c < 4:
            qb_ref[:, c * LANES:(c + 1) * LANES] = y.astype(BF16)
        else:
            kvb_ref[:, (c - 4) * LANES:(c - 3) * LANES] = y.astype(BF16)


def _proj_call(x2d, g1, wqkv, tabs, bd, seq):
    n = x2d.shape[0]
    tm = TM_PROJ
    tiles_per_seq = seq // tm
    full = lambda shape: pl.BlockSpec(shape, lambda i: (0,) * len(shape))
    return pl.pallas_call(
        _proj_kernel,
        grid=(n // tm,),
        in_specs=[
            pl.BlockSpec((tm, D_MODEL), lambda i: (i, 0)),
            full((1, D_MODEL)),
            full((D_MODEL, QKV_W)),
            pl.BlockSpec((4, 3, tm, LANES), lambda i: (0, 0, i % tiles_per_seq, 0)),
            full((2 * LANES, 2 * LANES)),
        ],
        out_specs=[
            pl.BlockSpec((tm, A_GROUP_W), lambda i: (i, 0)),
            pl.BlockSpec((tm // 4, 4 * A_GROUP_W), lambda i: (i, 0)),
            pl.BlockSpec((tm // 16, 16 * A_GROUP_W), lambda i: (i, 0)),
            pl.BlockSpec((tm, B_Q_W), lambda i: (i, 0)),
            pl.BlockSpec((tm, B_KV_W), lambda i: (i, 0)),
        ],
        out_shape=[
            jax.ShapeDtypeStruct((n, A_GROUP_W), BF16),
            jax.ShapeDtypeStruct((n // 4, 4 * A_GROUP_W), BF16),
            jax.ShapeDtypeStruct((n // 16, 16 * A_GROUP_W), BF16),
            jax.ShapeDtypeStruct((n, B_Q_W), BF16),
            jax.ShapeDtypeStruct((n, B_KV_W), BF16),
        ],
        scratch_shapes=[pltpu.VMEM((A_GROUP_W // LANES, tm, LANES), F32)],
        compiler_params=_cparams(("arbitrary",)),
        name="proj_qkv",
    )(x2d, g1, wqkv, tabs, bd)


def _attn_a_kernel(a0_ref, a1_ref, a2_ref, bw_ref, bn_ref, ya_ref, acc_ref, m_ref, l_ref):
    tq = 128
    lane = lax.broadcasted_iota(I32, (tq, LANES), 1)
    low = lane < HEAD_DIM

    def tile(g, ref, col0, length, i, rows):
        win = min(2 * tq, length)
        if isinstance(i, int):
            q0 = i * tq
            ws = max(0, min(q0 - RADIUS, length - win))
        else:
            q0 = pl.multiple_of(i * tq, tq)
            ws = pl.multiple_of(jnp.clip(q0 - RADIUS, 0, length - win), RADIUS)
        if win == length:
            bias = bn_ref[...]
        elif isinstance(i, int):
            bias = bw_ref[(q0 - ws) // RADIUS]
        else:
            bias = bw_ref[lax.shift_right_logical(q0 - ws, 6)]
        for hp in range(2):
            q2 = ref[0, pl.ds(q0, tq), col0 + hp * LANES:col0 + (hp + 1) * LANES]
            k2 = ref[0, pl.ds(ws, win), col0 + 256 + hp * LANES:col0 + 256 + (hp + 1) * LANES]
            v2 = ref[0, pl.ds(ws, win), col0 + 512 + hp * LANES:col0 + 512 + (hp + 1) * LANES]
            zero = jnp.zeros_like(q2)
            qs = jnp.concatenate([jnp.where(low, q2, zero), jnp.where(low, zero, q2)], axis=0)
            s = lax.dot_general(qs, k2, (((1,), (1,)), ((), ())), preferred_element_type=F32)
            s = s + bias
            m = jnp.max(s, axis=-1, keepdims=True)
            p = jnp.exp2(s - m)
            l = jnp.sum(p, axis=-1, keepdims=True)
            pv = jnp.dot(p.astype(BF16), v2, preferred_element_type=F32)
            slot = 2 * g + hp
            acc_ref[slot, rows, :] = jnp.where(low, pv[:tq], pv[tq:])
            m_ref[slot, rows, :] = jnp.where(low, m[:tq], m[tq:])
            l_ref[slot, rows, :] = jnp.where(low, l[:tq], l[tq:])

    seq = a0_ref.shape[1]

    def g0_body(i, carry):
        tile(0, a0_ref, 0, seq, i, pl.ds(pl.multiple_of(i * tq, tq), tq))
        return carry
    lax.fori_loop(0, seq // tq, g0_body, 0, unroll=16)

    len1 = seq // 4
    for res in range(4):
        def g1_body(i, carry, res=res):
            tile(1, a1_ref, res * A_GROUP_W, len1, i, pl.ds(res + 4 * i * tq, tq, stride=4))
            return carry
        lax.fori_loop(0, len1 // tq, g1_body, 0, unroll=4)

    len2 = seq // 16
    for res in range(16):
        tile(2, a2_ref, res * A_GROUP_W, len2, 0, pl.ds(res, tq, stride=16))

    rc = 256

    def merge_body(j, carry):
        rows = pl.ds(pl.multiple_of(j * rc, rc), rc)
        for hp in range(2):
            ms = [m_ref[2 * g + hp, rows, :] for g in range(N_GROUPS_A)]
            mx = jnp.maximum(jnp.maximum(ms[0], ms[1]), ms[2])
            num = jnp.zeros((rc, LANES), F32)
            den = jnp.zeros((rc, LANES), F32)
            for g in range(N_GROUPS_A):
                e = jnp.exp2(ms[g] - mx)
                num = num + e * acc_ref[2 * g + hp, rows, :]
                den = den + e * l_ref[2 * g + hp, rows, :]
            ya_ref[0, rows, hp * LANES:(hp + 1) * LANES] = (num / den).astype(BF16)
        return carry
    lax.fori_loop(0, seq // rc, merge_body, 0)


def _band_bias(offset, win):
    q = jnp.arange(2 * 128) % 128 + offset
    k = jnp.arange(win)
    return jnp.where(jnp.abs(q[:, None] - k[None, :]) <= RADIUS, 0.0, NEG_BIG).astype(F32)


def _attn_a_call(a0, a1, a2):
    b, seq, _ = a0.shape
    bias_wide = jnp.stack([_band_bias(off, 256) for off in (0, RADIUS, 2 * RADIUS)])
    bias_narrow = _band_bias(0, 128)
    return pl.pallas_call(
        _attn_a_kernel,
        grid=(b,),
        in_specs=[
            pl.BlockSpec((1, seq, A_GROUP_W), lambda i: (i, 0, 0)),
            pl.BlockSpec((1, seq // 4, 4 * A_GROUP_W), lambda i: (i, 0, 0)),
            pl.BlockSpec((1, seq // 16, 16 * A_GROUP_W), lambda i: (i, 0, 0)),
            pl.BlockSpec((3, 256, 256), lambda i: (0, 0, 0)),
            pl.BlockSpec((256, 128), lambda i: (0, 0)),
        ],
        out_specs=pl.BlockSpec((1, seq, A_OUT_W), lambda i: (i, 0, 0)),
        out_shape=jax.ShapeDtypeStruct((b, seq, A_OUT_W), BF16),
        scratch_shapes=[pltpu.VMEM((N_GROUPS_A * A_OUT_W // LANES, seq, LANES), F32)] * 3,
        compiler_params=_cparams(("arbitrary",)),
        name="attn_a",
    )(a0, a1, a2, bias_wide, bias_narrow)


def _attn_b_kernel(q_ref, kv_ref, o_ref, va_ref, vb_ref):
    tq = q_ref.shape[1]
    seq = kv_ref.shape[1]
    low = lax.broadcasted_iota(I32, (tq, LANES), 1) < HEAD_DIM

    @pl.when(pl.program_id(1) == 0)
    def _():
        low_k = lax.broadcasted_iota(I32, (seq, LANES), 1) < HEAD_DIM
        v2 = kv_ref[0, :, LANES:2 * LANES]
        one = jnp.ones_like(v2)
        va_ref[...] = jnp.where(low_k, v2, one)
        vb_ref[...] = jnp.where(low_k, one, v2)

    for pr in range(B_Q_W // LANES):
        q2 = q_ref[0, :, pr * LANES:(pr + 1) * LANES]
        zero = jnp.zeros_like(q2)
        qs = jnp.concatenate([jnp.where(low, q2, zero), jnp.where(low, zero, q2)], axis=0)
        m = jnp.full((2 * tq, 1), NEG_BIG, F32)
        acc_a = jnp.zeros((tq, LANES), F32)
        acc_b = jnp.zeros((tq, LANES), F32)
        for c in range(seq // TK_B):
            k2 = kv_ref[0, c * TK_B:(c + 1) * TK_B, 0:LANES]
            va = va_ref[c * TK_B:(c + 1) * TK_B, :]
            vb = vb_ref[c * TK_B:(c + 1) * TK_B, :]
            s = lax.dot_general(qs, k2, (((1,), (1,)), ((), ())), preferred_element_type=F32)
            m_n = jnp.maximum(m, jnp.max(s, axis=-1, keepdims=True))
            p = jnp.exp2(s - m_n).astype(BF16)
            a = jnp.exp2(m - m_n)
            acc_a = a[:tq] * acc_a + jnp.dot(p[:tq], va, preferred_element_type=F32)
            acc_b = a[tq:] * acc_b + jnp.dot(p[tq:], vb, preferred_element_type=F32)
            m = m_n
        oa = acc_a / pltpu.roll(acc_a, HEAD_DIM, 1)
        ob = acc_b / pltpu.roll(acc_b, HEAD_DIM, 1)
        o_ref[0, :, pr * LANES:(pr + 1) * LANES] = jnp.where(low, oa, ob).astype(BF16)


def _attn_b_call(qb, kvb):
    b, seq, _ = qb.shape
    return pl.pallas_call(
        _attn_b_kernel,
        grid=(b, seq // TQ_B),
        in_specs=[
            pl.BlockSpec((1, TQ_B, B_Q_W), lambda i, j: (i, j, 0)),
            pl.BlockSpec((1, seq, B_KV_W), lambda i, j: (i, 0, 0)),
        ],
        out_specs=pl.BlockSpec((1, TQ_B, B_Q_W), lambda i, j: (i, j, 0)),
        out_shape=jax.ShapeDtypeStruct((b, seq, B_Q_W), BF16),
        scratch_shapes=[pltpu.VMEM((seq, LANES), BF16)] * 2,
        compiler_params=_cparams(("arbitrary",) * 2),
        name="attn_b",
    )(qb, kvb)


def _pack_bf16_pairs(v):
    half = v.shape[1] // 2
    lo = lax.bitcast_convert_type(v[:, :half].astype(BF16).astype(F32), U32)
    hi = lax.bitcast_convert_type(v[:, half:].astype(BF16).astype(F32), U32)
    return (lo >> 16) | (hi & jnp.uint32(0xFFFF0000))


def _unpack_bf16_pairs(p):
    lo = lax.bitcast_convert_type(p << 16, F32)
    hi = lax.bitcast_convert_type(p & jnp.uint32(0xFFFF0000), F32)
    return lo, hi


def _merge_kernel(x_ref, ya_ref, yb_ref, g1_ref, wg_ref, wba_ref, wbb_ref, wo_ref, g2_ref,
                  wrh_ref, wrl_ref, bias_ref, wsgu_ref, wsd_ref, tri_ref,
                  base_ref, xp_ref, eidx_ref, rank_ref, wgt_ref, cnt_ref, carry_ref):
    step = pl.program_id(0)

    @pl.when(step == 0)
    def _():
        carry_ref[...] = jnp.zeros_like(carry_ref)

    for part in range(x_ref.shape[0] // ROUTE_TILE):
        rows = slice(part * ROUTE_TILE, (part + 1) * ROUTE_TILE)
        x = x_ref[rows, :]
        ms = jnp.mean(x * x, axis=-1, keepdims=True)
        xn = (x * lax.rsqrt(ms + EPS) * g1_ref[...]).astype(BF16)
        gates = jax.nn.sigmoid(jnp.dot(xn, wg_ref[...], preferred_element_type=F32))
        pa = jnp.dot(ya_ref[rows, :], wba_ref[...], preferred_element_type=F32)
        pb = jnp.dot(yb_ref[rows, :], wbb_ref[...], preferred_element_type=F32)
        merged = gates[:, :D_MODEL] * pa + gates[:, D_MODEL:] * pb
        h = x + jnp.dot(merged.astype(BF16), wo_ref[...], preferred_element_type=F32)

        ms2 = jnp.mean(h * h, axis=-1, keepdims=True)
        xn2 = h * lax.rsqrt(ms2 + EPS) * g2_ref[...]
        xn2b = xn2.astype(BF16)
        xp_ref[rows, :] = _pack_bf16_pairs(xn2)

        gu = jnp.dot(xn2b, wsgu_ref[...], preferred_element_type=F32)
        hs = (jax.nn.silu(gu[:, :EXPERT_FF]) * gu[:, EXPERT_FF:]).astype(BF16)
        base_ref[rows, :] = h + jnp.dot(hs, wsd_ref[...], preferred_element_type=F32)

        xlo = (xn2 - xn2b.astype(F32)).astype(BF16)
        logits = (jnp.dot(xn2b, wrh_ref[...], preferred_element_type=F32)
                  + jnp.dot(xlo, wrh_ref[...], preferred_element_type=F32)
                  + jnp.dot(xn2b, wrl_ref[...], preferred_element_type=F32))
        _route_tile(logits.T[:N_EXPERTS, :], bias_ref, tri_ref, carry_ref,
                    eidx_ref, rank_ref, wgt_ref, rows)
    cnt_ref[...] = carry_ref[...]


def _route_tile(logits_t, bias_ref, tri_ref, carry_ref, eidx_ref, rank_ref, wgt_ref, cols):
    tm = logits_t.shape[1]
    scores = jax.nn.sigmoid(logits_t)
    sel = scores + bias_ref[...]

    gsz = N_EXPERTS // N_EXPERT_GROUPS
    iota8 = lax.broadcasted_iota(I32, (gsz, tm), 0).astype(F32)
    ninf = jnp.float32(-jnp.inf)
    blocks, gs = [], []
    for g in range(N_EXPERT_GROUPS):
        blk = sel[g * gsz:(g + 1) * gsz, :]
        m1 = jnp.max(blk, axis=0, keepdims=True)
        first = jnp.min(jnp.where(blk == m1, iota8, float(gsz)), axis=0, keepdims=True)
        m2 = jnp.max(jnp.where(iota8 == first, ninf, blk), axis=0, keepdims=True)
        blocks.append(blk)
        gs.append(m1 + m2)
    cur = jnp.concatenate(gs, axis=0)
    gmask = jnp.zeros((N_EXPERT_GROUPS, tm), F32)
    for _ in range(TOPK_GROUPS):
        mx = jnp.max(cur, axis=0, keepdims=True)
        fi = jnp.min(jnp.where(cur == mx, iota8, float(N_EXPERT_GROUPS)), axis=0, keepdims=True)
        pick = iota8 == fi
        gmask = jnp.where(pick, 1.0, gmask)
        cur = jnp.where(pick, ninf, cur)
    cur = jnp.concatenate(
        [jnp.where(gmask[g:g + 1, :] > 0.5, blocks[g], ninf) for g in range(N_EXPERT_GROUPS)], axis=0)

    iota64 = lax.broadcasted_iota(I32, (N_EXPERTS, tm), 0).astype(F32)
    idxs, wks = [], []
    onehot = jnp.zeros((N_EXPERTS, tm), F32)
    for _ in range(TOP_K):
        mx = jnp.max(cur, axis=0, keepdims=True)
        fi = jnp.min(jnp.where(cur == mx, iota64, float(N_EXPERTS)), axis=0, keepdims=True)
        pick = iota64 == fi
        wks.append(jnp.sum(jnp.where(pick, scores, 0.0), axis=0, keepdims=True))
        idxs.append(fi)
        onehot = jnp.where(pick, 1.0, onehot)
        cur = jnp.where(pick, ninf, cur)
    wk = jnp.concatenate(wks, axis=0)
    wgt_ref[:, cols] = wk / jnp.sum(wk, axis=0, keepdims=True) * ROUTED_SCALE
    eidx_ref[:, cols] = jnp.concatenate(idxs, axis=0).astype(I32)

    before = jnp.dot(onehot.astype(BF16), tri_ref[...], preferred_element_type=F32) + carry_ref[:, 0:1]
    rank_ref[:, cols] = jnp.concatenate(
        [jnp.sum(jnp.where(iota64 == fi, before, 0.0), axis=0, keepdims=True) for fi in idxs],
        axis=0).astype(I32)
    carry_ref[...] = carry_ref[...] + jnp.sum(onehot, axis=1, keepdims=True)


def _merge_call(x2d, ya, yb, g1, wg, wba, wbb, wo, g2, wrh, wrl, bias, wsgu, wsd, tri, row0, n):
    tm = TM_MERGE
    off = row0 // tm
    full = lambda shape: pl.BlockSpec(shape, lambda i: (0,) * len(shape))
    row_in = lambda w: pl.BlockSpec((tm, w), lambda i: (i + off, 0))
    row = lambda w: pl.BlockSpec((tm, w), lambda i: (i, 0))
    col = lambda: pl.BlockSpec((TOP_K, tm), lambda i: (0, i))
    return pl.pallas_call(
        _merge_kernel,
        grid=(n // tm,),
        in_specs=[
            row_in(D_MODEL), row_in(A_OUT_W), row_in(B_Q_W),
            full((1, D_MODEL)), full((D_MODEL, 2 * D_MODEL)),
            full((A_OUT_W, D_MODEL)), full((B_Q_W, D_MODEL)), full((D_MODEL, D_MODEL)),
            full((1, D_MODEL)),
            full((D_MODEL, LANES)), full((D_MODEL, LANES)), full((N_EXPERTS, 1)),
            full((D_MODEL, 2 * EXPERT_FF)), full((EXPERT_FF, D_MODEL)),
            full((ROUTE_TILE, ROUTE_TILE)),
        ],
        out_specs=[row(D_MODEL), row(D_MODEL // 2), col(), col(), col(), full((N_EXPERTS, LANES))],
        out_shape=[
            jax.ShapeDtypeStruct((n, D_MODEL), F32),
            jax.ShapeDtypeStruct((n, D_MODEL // 2), U32),
            jax.ShapeDtypeStruct((TOP_K, n), I32),
            jax.ShapeDtypeStruct((TOP_K, n), I32),
            jax.ShapeDtypeStruct((TOP_K, n), F32),
            jax.ShapeDtypeStruct((N_EXPERTS, LANES), F32),
        ],
        scratch_shapes=[pltpu.VMEM((N_EXPERTS, LANES), F32)],
        compiler_params=_cparams(("arbitrary",)),
        name="merge_router",
    )(x2d, ya, yb, g1, wg, wba, wbb, wo, g2, wrh, wrl, bias, wsgu, wsd, tri)


EXPERT_SUBBLOCKS = 4


X_SLOTS = 3
Y_SLOTS = 2


def _expert_kernel(be_ref, nb_ref, xs_hbm, *refs):
    nw = 3 * EXPERT_SUBBLOCKS
    w_refs = refs[:nw]
    ys_hbm, xbuf, ybuf, xsem, ysem = refs[nw:]
    j = pl.program_id(0)
    nsteps = pl.num_programs(0)
    rows = EXPERT_SUBBLOCKS * BM_EXPERT

    def x_copy(step, slot):
        src = xs_hbm.at[pl.ds(pl.multiple_of(step * rows, rows), rows)]
        return pltpu.make_async_copy(src, xbuf.at[slot], xsem.at[slot])

    def y_copy(step, slot):
        dst = ys_hbm.at[pl.ds(pl.multiple_of(step * rows, rows), rows)]
        return pltpu.make_async_copy(ybuf.at[slot], dst, ysem.at[slot])

    @pl.when(j == 0)
    def _():
        x_copy(0, 0).start()
        x_copy(1, 1).start()

    @pl.when(j + 2 < nsteps)
    def _():
        x_copy(j + 2, lax.rem(j + 2, X_SLOTS)).start()

    xslot = lax.rem(j, X_SLOTS)
    yslot = lax.rem(j, Y_SLOTS)
    x_copy(j, xslot).wait()

    @pl.when(j >= Y_SLOTS)
    def _():
        y_copy(j - Y_SLOTS, yslot).wait()

    first = j * EXPERT_SUBBLOCKS

    @pl.when(first >= nb_ref[0])
    def _():
        ybuf[yslot] = jnp.zeros((rows, D_MODEL // 2), U32)

    @pl.when(first < nb_ref[0])
    def _():
        half = D_MODEL // 2
        for sub in range(EXPERT_SUBBLOCKS):
            wg_ref, wu_ref, wd_ref = w_refs[3 * sub:3 * sub + 3]
            blk = slice(sub * BM_EXPERT, (sub + 1) * BM_EXPERT)
            lo, hi = _unpack_bf16_pairs(xbuf[xslot, blk, :])
            lo = lo.astype(BF16)
            hi = hi.astype(BF16)
            gate = (jnp.dot(lo, wg_ref[0, :half, :], preferred_element_type=F32)
                    + jnp.dot(hi, wg_ref[0, half:, :], preferred_element_type=F32))
            up = (jnp.dot(lo, wu_ref[0, :half, :], preferred_element_type=F32)
                  + jnp.dot(hi, wu_ref[0, half:, :], preferred_element_type=F32))
            hb = (jax.nn.silu(gate) * up).astype(BF16)
            ybuf[yslot, blk, :] = _pack_bf16_pairs(jnp.dot(hb, wd_ref[0], preferred_element_type=F32))

    y_copy(j, yslot).start()

    @pl.when(j == nsteps - 1)
    def _():
        y_copy(j - 1, lax.rem(j - 1, Y_SLOTS)).wait()
        y_copy(j, yslot).wait()


def _expert_call(block_e, n_used, xs, wg, wu, wd):
    p_len = xs.shape[0]
    nsub = EXPERT_SUBBLOCKS
    rows = nsub * BM_EXPERT
    w_specs, w_args = [], []
    for sub in range(nsub):
        pick = lambda j, be, nb, sub=sub: (be[nsub * j + sub], 0, 0)
        w_specs += [
            pl.BlockSpec((1, D_MODEL, EXPERT_FF), pick),
            pl.BlockSpec((1, D_MODEL, EXPERT_FF), pick),
            pl.BlockSpec((1, EXPERT_FF, D_MODEL), pick),
        ]
        w_args += [wg, wu, wd]
    assert p_len // rows >= max(X_SLOTS, Y_SLOTS)
    return pl.pallas_call(
        _expert_kernel,
        grid_spec=pltpu.PrefetchScalarGridSpec(
            num_scalar_prefetch=2,
            grid=(p_len // rows,),
            in_specs=[pl.BlockSpec(memory_space=pl.ANY)] + w_specs,
            out_specs=pl.BlockSpec(memory_space=pl.ANY),
            scratch_shapes=[
                pltpu.VMEM((X_SLOTS, rows, D_MODEL // 2), U32),
                pltpu.VMEM((Y_SLOTS, rows, D_MODEL // 2), U32),
                pltpu.SemaphoreType.DMA((X_SLOTS,)),
                pltpu.SemaphoreType.DMA((Y_SLOTS,)),
            ],
        ),
        out_shape=jax.ShapeDtypeStruct((p_len, D_MODEL // 2), U32),
        compiler_params=_cparams(("arbitrary",)),
        name="routed_experts",
    )(block_e, n_used, xs, *w_args)


SC_CORES = 2
SC_SUBCORES = 16
SC_WORKERS = SC_CORES * SC_SUBCORES
SC_CHUNK = 128


def _sc_mesh():
    return plsc.VectorSubcoreMesh(core_axis_name="c", subcore_axis_name="s",
                                  num_cores=SC_CORES, num_subcores=SC_SUBCORES)


def _dispatch_rows(xp, dest, p_len):
    n, width = xp.shape
    per_worker = n // SC_CHUNK // SC_WORKERS

    @functools.partial(
        pl.kernel, mesh=_sc_mesh(),
        out_type=jax.ShapeDtypeStruct((p_len, width), xp.dtype),
        scratch_types=[pltpu.VMEM((TOP_K, SC_CHUNK), I32), pltpu.VMEM((SC_CHUNK, width), xp.dtype),
                       pltpu.SemaphoreType.DMA],
        name="sc_dispatch")
    def body(xp_hbm, dest_hbm, xs_hbm, idx_v, rows_v, sem):
        wid = lax.axis_index("s") * SC_CORES + lax.axis_index("c")

        @pl.loop(0, per_worker)
        def _(j):
            chunk = wid * per_worker + j
            pltpu.sync_copy(dest_hbm.at[:, pl.ds(chunk * SC_CHUNK, SC_CHUNK)], idx_v)
            pltpu.sync_copy(xp_hbm.at[pl.ds(chunk * SC_CHUNK, SC_CHUNK)], rows_v)
            copies = [pltpu.async_copy(rows_v, xs_hbm.at[idx_v.at[k]], sem) for k in range(TOP_K)]
            for cp in copies:
                cp.wait()

    return body(xp, dest)


SC_LANES = 16
SC_REDUCE_CHUNK = 8


def _reduce_rows(ys, dest, wbits):
    width = ys.shape[1]
    tc = SC_REDUCE_CHUNK
    n = dest.shape[1]
    tokens = n // SC_WORKERS
    per_worker = tokens // tc
    assert per_worker % 2 == 0 and 2 * tc == SC_LANES

    @functools.partial(
        pl.kernel, mesh=_sc_mesh(),
        out_type=jax.ShapeDtypeStruct((n, width), ys.dtype),
        scratch_types=[pltpu.VMEM((TOP_K, tokens), I32),
                       pltpu.VMEM((TOP_K, tokens), ys.dtype),
                       pltpu.VMEM((2, TOP_K, tc, width), ys.dtype),
                       pltpu.VMEM((2, tc, width), ys.dtype),
                       pltpu.SemaphoreType.DMA, pltpu.SemaphoreType.DMA,
                       pltpu.SemaphoreType.DMA, pltpu.SemaphoreType.DMA],
        compiler_params=pltpu.CompilerParams(needs_layout_passes=False),
        name="sc_reduce")
    def body(ys_hbm, dest_hbm, w_hbm, out_hbm, idx_v, w_v, rows_v, r_v, gsem0, gsem1, osem0, osem1):
        wid = lax.axis_index("s") * SC_CORES + lax.axis_index("c")
        first = wid * tokens
        gsems, osems = (gsem0, gsem1), (osem0, osem1)
        pltpu.sync_copy(dest_hbm.at[:, pl.ds(first, tokens)], idx_v)
        pltpu.sync_copy(w_hbm.at[:, pl.ds(first, tokens)], w_v)

        def fetches(j, slot):
            return [pltpu.make_async_copy(ys_hbm.at[idx_v.at[k, pl.ds(j * tc, tc)]], rows_v.at[slot, k],
                                          gsems[slot])
                    for k in range(TOP_K)]

        def write_back(j, slot):
            return pltpu.make_async_copy(r_v.at[slot], out_hbm.at[pl.ds(first + j * tc, tc)],
                                         osems[slot])

        def reduce_item(p, slot):
            wrows = [w_v[k, pl.ds(pl.multiple_of(p * SC_LANES, SC_LANES), SC_LANES)] for k in range(TOP_K)]

            @pl.loop(0, tc)
            def _(t):
                lane = jnp.full((SC_LANES,), slot * tc + t, I32)
                ws = [plsc.bitcast(wrows[k].at[lane].get(mode="promise_in_bounds"), BF16)
                      for k in range(TOP_K)]

                @pl.loop(0, width // SC_LANES)
                def _(jv):
                    col = pl.multiple_of(jv * SC_LANES, SC_LANES)
                    terms = [plsc.bitcast(rows_v[slot, k, t, pl.ds(col, SC_LANES)], BF16) * ws[k]
                             for k in range(TOP_K)]
                    while len(terms) > 1:
                        terms = [terms[i] + terms[i + 1] for i in range(0, len(terms), 2)]
                    r_v[slot, t, pl.ds(col, SC_LANES)] = plsc.bitcast(terms[0], ys.dtype)

        for cp in fetches(0, 0):
            cp.start()

        @pl.loop(0, per_worker // 2)
        def _(p):
            for slot in range(2):
                j = 2 * p + slot
                @pl.when(j + 1 < per_worker)
                def _():
                    for cp in fetches(j + 1, 1 - slot):
                        cp.start()
                for cp in fetches(j, slot):
                    cp.wait()

                @pl.when(p > 0)
                def _():
                    write_back(j - 2, slot).wait()
                reduce_item(p, slot)
                write_back(j, slot).start()

        write_back(per_worker - 2, 0).wait()
        write_back(per_worker - 1, 1).wait()

    return body(ys, dest, wbits)


def _final_add_kernel(base_ref, r_ref, *refs):
    o_ref = refs[-1]
    half = D_MODEL // 2
    lo, hi = _unpack_bf16_pairs(r_ref[...])
    o_ref[:, :half] = base_ref[:, :half] + lo
    o_ref[:, half:] = base_ref[:, half:] + hi


def _final_add_call(base, routed, prev, row0, n_total):
    n = base.shape[0]
    tm = TM_COMBINE
    off = row0 // tm
    in_specs = [pl.BlockSpec((tm, D_MODEL), lambda i: (i, 0)),
                pl.BlockSpec((tm, D_MODEL // 2), lambda i: (i, 0))]
    args, aliases = [base, routed], {}
    if prev is not None:
        in_specs.append(pl.BlockSpec(memory_space=pl.ANY))
        args.append(prev)
        aliases = {2: 0}
    return pl.pallas_call(
        _final_add_kernel,
        grid=(n // tm,),
        in_specs=in_specs,
        out_specs=pl.BlockSpec((tm, D_MODEL), lambda i: (i + off, 0)),
        out_shape=jax.ShapeDtypeStruct((n_total, D_MODEL), F32),
        input_output_aliases=aliases,
        compiler_params=_cparams(("arbitrary",)),
        name="moe_final_add",
    )(*args)


def _fold_gain(tab, gain, half):
    g = jnp.tile(gain.astype(F32), LANES // HEAD_DIM)
    return jnp.stack([tab[0] * g, tab[1] * jnp.roll(g, LANES - half), tab[2] * jnp.roll(g, half)])


def _prep_tables(seq):
    pos = jnp.arange(seq)
    cos1, sin1 = _rope_tables(pos, HEAD_DIM)
    cos_r, sin_r = _rope_tables(pos // GRID_W, HEAD_DIM // 2)
    cos_c, sin_c = _rope_tables(pos % GRID_W, HEAD_DIM // 2)
    taba = _lane_tables(cos1, sin1, HEAD_DIM // 2)
    tabb = _lane_tables(jnp.concatenate([cos_r, cos_c], -1), jnp.concatenate([sin_r, sin_c], -1),
                        HEAD_DIM // 4)
    return taba, tabb


def _layer(h, p, taba, tabb):
    b, seq, d = h.shape
    n = b * seq
    x2d = h.reshape(n, d)
    w_in = p["w_in"]
    o1 = N_GROUPS_A * A_GROUP_W
    o2 = o1 + B_Q_W
    o3 = o2 + B_KV_W
    pair_heads = jnp.array([0, 4, 1, 5, 2, 6, 3, 7])
    pair_cols = (pair_heads[:, None] * HEAD_DIM + jnp.arange(HEAD_DIM)[None, :]).reshape(-1)
    wqkv = jnp.concatenate([w_in[:, :o1], w_in[:, o1:o2][:, pair_cols], w_in[:, o2:o3]], axis=1).astype(BF16)
    tabs = jnp.stack([
        _fold_gain(taba, p["q_norm_a"] * Q_SCALE, HEAD_DIM // 2),
        _fold_gain(taba, p["k_norm_a"], HEAD_DIM // 2),
        _fold_gain(tabb, p["q_norm_b"] * Q_SCALE, HEAD_DIM // 4),
        _fold_gain(tabb, p["k_norm_b"], HEAD_DIM // 4),
    ])
    seg = jnp.arange(2 * LANES) // HEAD_DIM
    bd = jnp.where(seg[:, None] == seg[None, :], 1.0 / HEAD_DIM, 0.0).astype(BF16)
    g1 = p["norm1_g"].reshape(1, d).astype(F32)

    a0, a1, a2, qb, kvb = _proj_call(x2d, g1, wqkv, tabs, bd, seq)
    ya = _attn_a_call(a0.reshape(b, seq, A_GROUP_W), a1.reshape(b, seq // 4, 4 * A_GROUP_W),
                      a2.reshape(b, seq // 16, 16 * A_GROUP_W))
    yb = _attn_b_call(qb.reshape(b, seq, B_Q_W), kvb.reshape(b, seq, B_KV_W))

    wg = w_in[:, o3:].astype(BF16)
    wbb = p["w_branch_b"][pair_cols, :].astype(BF16)
    wr = jnp.pad(p["w_router"], ((0, 0), (0, LANES - N_EXPERTS)))
    wrh = wr.astype(BF16)
    wrl = (wr - wrh.astype(F32)).astype(BF16)
    wsgu = jnp.concatenate([p["ws_gate"], p["ws_up"]], axis=1).astype(BF16)
    tri = (jnp.arange(ROUTE_TILE)[:, None] < jnp.arange(ROUTE_TILE)[None, :]).astype(BF16)
    wba = p["w_branch_a"].astype(BF16)
    wo = p["w_out"].astype(BF16)
    g2 = p["norm2_g"].reshape(1, d).astype(F32)
    bias = p["router_bias"].reshape(N_EXPERTS, 1).astype(F32)
    wsd = p["ws_down"].astype(BF16)
    weg = p["we_gate"].astype(BF16)
    weu = p["we_up"].astype(BF16)
    wd = p["we_down"].astype(BF16)
    ya2d = ya.reshape(n, A_OUT_W)
    yb2d = yb.reshape(n, B_Q_W)

    nc = n // MOE_CHUNKS
    bm = BM_EXPERT
    n_blocks = (nc * TOP_K + N_EXPERTS * (bm - 1)) // bm + 1
    n_blocks = -(-n_blocks // EXPERT_SUBBLOCKS) * EXPERT_SUBBLOCKS
    p_len = n_blocks * bm
    block_start = jnp.arange(n_blocks, dtype=I32) * bm
    out = None
    for c in range(MOE_CHUNKS):
        base, xp, eidx, rank, wgt, cnt = _merge_call(
            x2d, ya2d, yb2d, g1, wg, wba, wbb, wo, g2, wrh, wrl, bias, wsgu, wsd, tri, c * nc, nc)
        counts = cnt[:, 0].astype(I32)
        padded = (counts + bm - 1) // bm * bm
        pend = jnp.cumsum(padded)
        pstart = pend - padded
        onehot = eidx[:, :, None] == jnp.arange(N_EXPERTS)[None, None, :]
        dest = jnp.sum(jnp.where(onehot, pstart[None, None, :], 0), axis=-1) + rank
        block_e = jnp.minimum(jnp.sum(pend[None, :] <= block_start[:, None], axis=1),
                              N_EXPERTS - 1).astype(I32)
        n_used = (pend[-1] // bm).astype(I32).reshape(1)
        wbits = lax.bitcast_convert_type(wgt.astype(BF16), jnp.uint16).astype(U32)
        wbits = wbits | (wbits << 16)

        xs = _dispatch_rows(xp, dest, p_len)
        ys = _expert_call(block_e, n_used, xs, weg, weu, wd)
        routed = _reduce_rows(ys, dest, wbits)
        out = _final_add_call(base, routed, out, c * nc, n)
    return out.reshape(b, seq, d)


def kernel(x, norm1_g, w_in, q_norm_a, k_norm_a, q_norm_b, k_norm_b, w_branch_a, w_branch_b, w_out,
           norm2_g, w_router, router_bias, we_gate, we_up, we_down, ws_gate, ws_up, ws_down):
    params = dict(norm1_g=norm1_g, w_in=w_in, q_norm_a=q_norm_a, k_norm_a=k_norm_a, q_norm_b=q_norm_b,
                  k_norm_b=k_norm_b, w_branch_a=w_branch_a, w_branch_b=w_branch_b, w_out=w_out,
                  norm2_g=norm2_g, w_router=w_router, router_bias=router_bias, we_gate=we_gate,
                  we_up=we_up, we_down=we_down, ws_gate=ws_gate, ws_up=ws_up, ws_down=ws_down)
    taba, tabb = _prep_tables(x.shape[1])
    h = x
    for l in range(norm1_g.shape[0]):
        h = _layer(h, {k: v[l] for k, v in params.items()}, taba, tabb)
    return h
```

```python
import functools

import jax
import jax.numpy as jnp
from jax import lax
from jax.experimental import pallas as pl
from jax.experimental.pallas import tpu as pltpu
from jax.experimental.pallas import tpu_sc as plsc

F32 = jnp.float32
BF16 = jnp.bfloat16
I32 = jnp.int32
U32 = jnp.uint32

D_MODEL = 1024
HEAD_DIM = 64
ROPE_THETA = 10000.0
EPS = 1e-6
GRID_W = 64
DILATIONS = (1, 4, 16)
RADIUS = 64
N_GROUPS_A = 3
A_GROUP_W = 768
A_OUT_W = 256
B_Q_W = 512
B_KV_W = 256
QKV_W = N_GROUPS_A * A_GROUP_W + B_Q_W + B_KV_W
N_EXPERTS = 64
N_EXPERT_GROUPS = 8
TOPK_GROUPS = 4
TOP_K = 8
EXPERT_FF = 256
ROUTED_SCALE = 2.5

LANES = 128
NEG_BIG = -1e30
Q_SCALE = HEAD_DIM ** -0.5 * 1.4426950408889634

TM_PROJ = 1024
PROJ_SUB = 512
TQ_B = 512
TK_B = 512
TM_MERGE = 512
ROUTE_TILE = 256
BM_EXPERT = 512
TM_COMBINE = 512
MOE_CHUNKS = 2
VMEM_LIMIT = 56 * 1024 * 1024


def _cparams(sem):
    return pltpu.CompilerParams(dimension_semantics=sem, vmem_limit_bytes=VMEM_LIMIT)


def _rope_tables(pos, dim):
    inv = ROPE_THETA ** (-jnp.arange(0, dim, 2, dtype=F32) / dim)
    ang = pos.astype(F32)[:, None] * inv[None, :]
    ang = jnp.concatenate([ang, ang], axis=-1)
    return jnp.cos(ang), jnp.sin(ang)


def _lane_tables(cos_h, sin_h, half):
    cos2 = jnp.concatenate([cos_h, cos_h], axis=-1)
    sin2 = jnp.concatenate([sin_h, sin_h], axis=-1)
    first = (jnp.arange(LANES) % (2 * half)) < half
    s_left = jnp.where(first[None, :], -sin2, 0.0)
    s_right = jnp.where(first[None, :], 0.0, sin2)
    return jnp.stack([cos2, s_left, s_right]).astype(F32)


TAB_AQ, TAB_AK, TAB_BQ, TAB_BK = range(4)


def _proj_kernel(x_ref, g1_ref, w_ref, tab_ref, bd_ref,
                 a0_ref, a1_ref, a2_ref, qb_ref, kvb_ref, pbuf_ref):
    for part in range(x_ref.shape[0] // PROJ_SUB):
        _proj_sub_tile(part, x_ref, g1_ref, w_ref, tab_ref, bd_ref,
                       a0_ref, a1_ref, a2_ref, qb_ref, kvb_ref, pbuf_ref)


def _proj_sub_tile(part, x_ref, g1_ref, w_ref, tab_ref, bd_ref,
                   a0_ref, a1_ref, a2_ref, qb_ref, kvb_ref, pbuf_ref):
    tm = PROJ_SUB
    rows = slice(part * tm, (part + 1) * tm)
    x = x_ref[rows, :]
    ms = jnp.mean(x * x, axis=-1, keepdims=True)
    xn = (x * lax.rsqrt(ms + EPS) * g1_ref[...]).astype(BF16)

    def norm_rope(y, t, shift):
        w = y.shape[1]
        ss = jnp.dot((y * y).astype(BF16), bd_ref[0:w, 0:w], preferred_element_type=F32)
        yn = y * lax.rsqrt(ss + EPS)
        out = []
        for i in range(w // LANES):
            z = yn[:, i * LANES:(i + 1) * LANES]
            out.append(z * tab_ref[t, 0, rows, :] + pltpu.roll(z, LANES - shift, 1) * tab_ref[t, 1, rows, :]
                       + pltpu.roll(z, shift, 1) * tab_ref[t, 2, rows, :])
        return out

    for g, d in enumerate(DILATIONS):
        base = g * A_GROUP_W
        r = jnp.dot(xn, w_ref[:, base:base + A_GROUP_W], preferred_element_type=F32)
        chunks = (norm_rope(r[:, 0:2 * LANES], TAB_AQ, HEAD_DIM // 2)
                  + norm_rope(r[:, 2 * LANES:4 * LANES], TAB_AK, HEAD_DIM // 2)
                  + [r[:, 4 * LANES:5 * LANES], r[:, 5 * LANES:6 * LANES]])
        for c, y in enumerate(chunks):
            if d == 1:
                a0_ref[rows, c * LANES:(c + 1) * LANES] = y.astype(BF16)
            else:
                pbuf_ref[part, c] = y
        if d > 1:
            out_ref = a1_ref if g == 1 else a2_ref
            vrows = tm // d
            out_rows = slice(part * vrows, (part + 1) * vrows)
            for res in range(d):
                for c in range(A_GROUP_W // LANES):
                    col = res * A_GROUP_W + c * LANES
                    out_ref[out_rows, col:col + LANES] = (
                        pbuf_ref[part, c, pl.ds(res, vrows, stride=d), :].astype(BF16))

    base = N_GROUPS_A * A_GROUP_W
    r = jnp.dot(xn, w_ref[:, base:base + B_Q_W + B_KV_W], preferred_element_type=F32)
    chunks = (norm_rope(r[:, 0:2 * LANES], TAB_BQ, HEAD_DIM // 4)
              + norm_rope(r[:, 2 * LANES:4 * LANES], TAB_BQ, HEAD_DIM // 4)
              + norm_rope(r[:, 4 * LANES:5 * LANES], TAB_BK, HEAD_DIM // 4)
              + [r[:, 5 * LANES:6 * LANES]])
    for c, y in enumerate(chunks):
        if c < 4:
            qb_ref[rows, c * LANES:(c + 1) * LANES] = y.astype(BF16)
        else:
            kvb_ref[rows, (c - 4) * LANES:(c - 3) * LANES] = y.astype(BF16)


def _proj_call(x2d, g1, wqkv, tabs, bd, seq):
    n = x2d.shape[0]
    tm = TM_PROJ
    tiles_per_seq = seq // tm
    full = lambda shape: pl.BlockSpec(shape, lambda i: (0,) * len(shape))
    return pl.pallas_call(
        _proj_kernel,
        grid=(n // tm,),
        in_specs=[
            pl.BlockSpec((tm, D_MODEL), lambda i: (i, 0)),
            full((1, D_MODEL)),
            full((D_MODEL, QKV_W)),
            pl.BlockSpec((4, 3, tm, LANES), lambda i: (0, 0, i % tiles_per_seq, 0)),
            full((2 * LANES, 2 * LANES)),
        ],
        out_specs=[
            pl.BlockSpec((tm, A_GROUP_W), lambda i: (i, 0)),
            pl.BlockSpec((tm // 4, 4 * A_GROUP_W), lambda i: (i, 0)),
            pl.BlockSpec((tm // 16, 16 * A_GROUP_W), lambda i: (i, 0)),
            pl.BlockSpec((tm, B_Q_W), lambda i: (i, 0)),
            pl.BlockSpec((tm, B_KV_W), lambda i: (i, 0)),
        ],
        out_shape=[
            jax.ShapeDtypeStruct((n, A_GROUP_W), BF16),
            jax.ShapeDtypeStruct((n // 4, 4 * A_GROUP_W), BF16),
            jax.ShapeDtypeStruct((n // 16, 16 * A_GROUP_W), BF16),
            jax.ShapeDtypeStruct((n, B_Q_W), BF16),
            jax.ShapeDtypeStruct((n, B_KV_W), BF16),
        ],
        scratch_shapes=[pltpu.VMEM((tm // PROJ_SUB, A_GROUP_W // LANES, PROJ_SUB, LANES), F32)],
        compiler_params=_cparams(("arbitrary",)),
        name="proj_qkv",
    )(x2d, g1, wqkv, tabs, bd)


def _attn_a_kernel(a0_ref, a1_ref, a2_ref, bw_ref, bn_ref, ya_ref, acc_ref, m_ref, l_ref):
    tq = 128
    lane = lax.broadcasted_iota(I32, (tq, LANES), 1)
    low = lane < HEAD_DIM

    def tile(g, ref, col0, length, i, rows):
        win = min(2 * tq, length)
        if isinstance(i, int):
            q0 = i * tq
            ws = max(0, min(q0 - RADIUS, length - win))
        else:
            q0 = pl.multiple_of(i * tq, tq)
            ws = pl.multiple_of(jnp.clip(q0 - RADIUS, 0, length - win), RADIUS)
        if win == length:
            bias = bn_ref[...]
        elif isinstance(i, int):
            bias = bw_ref[(q0 - ws) // RADIUS]
        else:
            bias = bw_ref[lax.shift_right_logical(q0 - ws, 6)]
        for hp in range(2):
            q2 = ref[0, pl.ds(q0, tq), col0 + hp * LANES:col0 + (hp + 1) * LANES]
            k2 = ref[0, pl.ds(ws, win), col0 + 256 + hp * LANES:col0 + 256 + (hp + 1) * LANES]
            v2 = ref[0, pl.ds(ws, win), col0 + 512 + hp * LANES:col0 + 512 + (hp + 1) * LANES]
            zero = jnp.zeros_like(q2)
            qs = jnp.concatenate([jnp.where(low, q2, zero), jnp.where(low, zero, q2)], axis=0)
            s = lax.dot_general(qs, k2, (((1,), (1,)), ((), ())), preferred_element_type=F32)
            s = s + bias
            m = jnp.max(s, axis=-1, keepdims=True)
            p = jnp.exp2(s - m)
            l = jnp.sum(p, axis=-1, keepdims=True)
            pv = jnp.dot(p.astype(BF16), v2, preferred_element_type=F32)
            slot = 2 * g + hp
            acc_ref[slot, rows, :] = jnp.where(low, pv[:tq], pv[tq:])
            m_ref[slot, rows, :] = jnp.where(low, m[:tq], m[tq:])
            l_ref[slot, rows, :] = jnp.where(low, l[:tq], l[tq:])

    seq = a0_ref.shape[1]

    def g0_body(i, carry):
        tile(0, a0_ref, 0, seq, i, pl.ds(pl.multiple_of(i * tq, tq), tq))
        return carry
    lax.fori_loop(0, seq // tq, g0_body, 0, unroll=16)

    len1 = seq // 4
    for res in range(4):
        def g1_body(i, carry, res=res):
            tile(1, a1_ref, res * A_GROUP_W, len1, i, pl.ds(res + 4 * i * tq, tq, stride=4))
            return carry
        lax.fori_loop(0, len1 // tq, g1_body, 0, unroll=4)

    len2 = seq // 16
    for res in range(16):
        tile(2, a2_ref, res * A_GROUP_W, len2, 0, pl.ds(res, tq, stride=16))

    rc = 256

    def merge_body(j, carry):
        rows = pl.ds(pl.multiple_of(j * rc, rc), rc)
        for hp in range(2):
            ms = [m_ref[2 * g + hp, rows, :] for g in range(N_GROUPS_A)]
            mx = jnp.maximum(jnp.maximum(ms[0], ms[1]), ms[2])
            num = jnp.zeros((rc, LANES), F32)
            den = jnp.zeros((rc, LANES), F32)
            for g in range(N_GROUPS_A):
                e = jnp.exp2(ms[g] - mx)
                num = num + e * acc_ref[2 * g + hp, rows, :]
                den = den + e * l_ref[2 * g + hp, rows, :]
            ya_ref[0, rows, hp * LANES:(hp + 1) * LANES] = (num / den).astype(BF16)
        return carry
    lax.fori_loop(0, seq // rc, merge_body, 0)


def _band_bias(offset, win):
    q = jnp.arange(2 * 128) % 128 + offset
    k = jnp.arange(win)
    return jnp.where(jnp.abs(q[:, None] - k[None, :]) <= RADIUS, 0.0, NEG_BIG).astype(F32)


def _attn_a_call(a0, a1, a2):
    b, seq, _ = a0.shape
    bias_wide = jnp.stack([_band_bias(off, 256) for off in (0, RADIUS, 2 * RADIUS)])
    bias_narrow = _band_bias(0, 128)
    return pl.pallas_call(
        _attn_a_kernel,
        grid=(b,),
        in_specs=[
            pl.BlockSpec((1, seq, A_GROUP_W), lambda i: (i, 0, 0)),
            pl.BlockSpec((1, seq // 4, 4 * A_GROUP_W), lambda i: (i, 0, 0)),
            pl.BlockSpec((1, seq // 16, 16 * A_GROUP_W), lambda i: (i, 0, 0)),
            pl.BlockSpec((3, 256, 256), lambda i: (0, 0, 0)),
            pl.BlockSpec((256, 128), lambda i: (0, 0)),
        ],
        out_specs=pl.BlockSpec((1, seq, A_OUT_W), lambda i: (i, 0, 0)),
        out_shape=jax.ShapeDtypeStruct((b, seq, A_OUT_W), BF16),
        scratch_shapes=[pltpu.VMEM((N_GROUPS_A * A_OUT_W // LANES, seq, LANES), F32)] * 3,
        compiler_params=_cparams(("arbitrary",)),
        name="attn_a",
    )(a0, a1, a2, bias_wide, bias_narrow)


def _attn_b_kernel(q_ref, kv_ref, o_ref, va_ref, vb_ref):
    tq = q_ref.shape[1]
    seq = kv_ref.shape[1]
    low = lax.broadcasted_iota(I32, (tq, LANES), 1) < HEAD_DIM

    @pl.when(pl.program_id(1) == 0)
    def _():
        low_k = lax.broadcasted_iota(I32, (seq, LANES), 1) < HEAD_DIM
        v2 = kv_ref[0, :, LANES:2 * LANES]
        one = jnp.ones_like(v2)
        va_ref[...] = jnp.where(low_k, v2, one)
        vb_ref[...] = jnp.where(low_k, one, v2)

    for pr in range(B_Q_W // LANES):
        q2 = q_ref[0, :, pr * LANES:(pr + 1) * LANES]
        zero = jnp.zeros_like(q2)
        qs = jnp.concatenate([jnp.where(low, q2, zero), jnp.where(low, zero, q2)], axis=0)
        m = jnp.full((2 * tq, 1), NEG_BIG, F32)
        acc_a = jnp.zeros((tq, LANES), F32)
        acc_b = jnp.zeros((tq, LANES), F32)
        for c in range(seq // TK_B):
            k2 = kv_ref[0, c * TK_B:(c + 1) * TK_B, 0:LANES]
            va = va_ref[c * TK_B:(c + 1) * TK_B, :]
            vb = vb_ref[c * TK_B:(c + 1) * TK_B, :]
            s = lax.dot_general(qs, k2, (((1,), (1,)), ((), ())), preferred_element_type=F32)
            m_n = jnp.maximum(m, jnp.max(s, axis=-1, keepdims=True))
            p = jnp.exp2(s - m_n).astype(BF16)
            a = jnp.exp2(m - m_n)
            acc_a = a[:tq] * acc_a + jnp.dot(p[:tq], va, preferred_element_type=F32)
            acc_b = a[tq:] * acc_b + jnp.dot(p[tq:], vb, preferred_element_type=F32)
            m = m_n
        oa = acc_a / pltpu.roll(acc_a, HEAD_DIM, 1)
        ob = acc_b / pltpu.roll(acc_b, HEAD_DIM, 1)
        o_ref[0, :, pr * LANES:(pr + 1) * LANES] = jnp.where(low, oa, ob).astype(BF16)


def _attn_b_call(qb, kvb):
    b, seq, _ = qb.shape
    return pl.pallas_call(
        _attn_b_kernel,
        grid=(b, seq // TQ_B),
        in_specs=[
            pl.BlockSpec((1, TQ_B, B_Q_W), lambda i, j: (i, j, 0)),
            pl.BlockSpec((1, seq, B_KV_W), lambda i, j: (i, 0, 0)),
        ],
        out_specs=pl.BlockSpec((1, TQ_B, B_Q_W), lambda i, j: (i, j, 0)),
        out_shape=jax.ShapeDtypeStruct((b, seq, B_Q_W), BF16),
        scratch_shapes=[pltpu.VMEM((seq, LANES), BF16)] * 2,
        compiler_params=_cparams(("arbitrary",) * 2),
        name="attn_b",
    )(qb, kvb)


def _pack_bf16_pairs(v):
    half = v.shape[1] // 2
    lo = lax.bitcast_convert_type(v[:, :half].astype(BF16).astype(F32), U32)
    hi = lax.bitcast_convert_type(v[:, half:].astype(BF16).astype(F32), U32)
    return (lo >> 16) | (hi & jnp.uint32(0xFFFF0000))


def _unpack_bf16_pairs(p):
    lo = lax.bitcast_convert_type(p << 16, F32)
    hi = lax.bitcast_convert_type(p & jnp.uint32(0xFFFF0000), F32)
    return lo, hi


def _merge_kernel(x_ref, ya_ref, yb_ref, g1_ref, wg_ref, wba_ref, wbb_ref, wo_ref, g2_ref,
                  wrh_ref, wrl_ref, bias_ref, wsgu_ref, wsd_ref, tri_ref,
                  base_ref, xp_ref, eidx_ref, rank_ref, wgt_ref, cnt_ref, carry_ref):
    step = pl.program_id(0)

    @pl.when(step == 0)
    def _():
        carry_ref[...] = jnp.zeros_like(carry_ref)

    for part in range(x_ref.shape[0] // ROUTE_TILE):
        rows = slice(part * ROUTE_TILE, (part + 1) * ROUTE_TILE)
        x = x_ref[rows, :]
        ms = jnp.mean(x * x, axis=-1, keepdims=True)
        xn = (x * lax.rsqrt(ms + EPS) * g1_ref[...]).astype(BF16)
        gates = jax.nn.sigmoid(jnp.dot(xn, wg_ref[...], preferred_element_type=F32))
        pa = jnp.dot(ya_ref[rows, :], wba_ref[...], preferred_element_type=F32)
        pb = jnp.dot(yb_ref[rows, :], wbb_ref[...], preferred_element_type=F32)
        merged = gates[:, :D_MODEL] * pa + gates[:, D_MODEL:] * pb
        h = x + jnp.dot(merged.astype(BF16), wo_ref[...], preferred_element_type=F32)

        ms2 = jnp.mean(h * h, axis=-1, keepdims=True)
        xn2 = h * lax.rsqrt(ms2 + EPS) * g2_ref[...]
        xn2b = xn2.astype(BF16)
        xp_ref[rows, :] = _pack_bf16_pairs(xn2)

        gu = jnp.dot(xn2b, wsgu_ref[...], preferred_element_type=F32)
        hs = (jax.nn.silu(gu[:, :EXPERT_FF]) * gu[:, EXPERT_FF:]).astype(BF16)
        base_ref[rows, :] = h + jnp.dot(hs, wsd_ref[...], preferred_element_type=F32)

        xlo = (xn2 - xn2b.astype(F32)).astype(BF16)
        logits = (jnp.dot(xn2b, wrh_ref[...], preferred_element_type=F32)
                  + jnp.dot(xlo, wrh_ref[...], preferred_element_type=F32)
                  + jnp.dot(xn2b, wrl_ref[...], preferred_element_type=F32))
        _route_tile(logits.T[:N_EXPERTS, :], bias_ref, tri_ref, carry_ref,
                    eidx_ref, rank_ref, wgt_ref, rows)
    cnt_ref[...] = carry_ref[...]


def _route_tile(logits_t, bias_ref, tri_ref, carry_ref, eidx_ref, rank_ref, wgt_ref, cols):
    tm = logits_t.shape[1]
    scores = jax.nn.sigmoid(logits_t)
    sel = scores + bias_ref[...]

    gsz = N_EXPERTS // N_EXPERT_GROUPS
    iota8 = lax.broadcasted_iota(I32, (gsz, tm), 0).astype(F32)
    ninf = jnp.float32(-jnp.inf)
    blocks, gs = [], []
    for g in range(N_EXPERT_GROUPS):
        blk = sel[g * gsz:(g + 1) * gsz, :]
        m1 = jnp.max(blk, axis=0, keepdims=True)
        first = jnp.min(jnp.where(blk == m1, iota8, float(gsz)), axis=0, keepdims=True)
        m2 = jnp.max(jnp.where(iota8 == first, ninf, blk), axis=0, keepdims=True)
        blocks.append(blk)
        gs.append(m1 + m2)
    cur = jnp.concatenate(gs, axis=0)
    gmask = jnp.zeros((N_EXPERT_GROUPS, tm), F32)
    for _ in range(TOPK_GROUPS):
        mx = jnp.max(cur, axis=0, keepdims=True)
        fi = jnp.min(jnp.where(cur == mx, iota8, float(N_EXPERT_GROUPS)), axis=0, keepdims=True)
        pick = iota8 == fi
        gmask = jnp.where(pick, 1.0, gmask)
        cur = jnp.where(pick, ninf, cur)
    cur = jnp.concatenate(
        [jnp.where(gmask[g:g + 1, :] > 0.5, blocks[g], ninf) for g in range(N_EXPERT_GROUPS)], axis=0)

    iota64 = lax.broadcasted_iota(I32, (N_EXPERTS, tm), 0).astype(F32)
    idxs, wks = [], []
    onehot = jnp.zeros((N_EXPERTS, tm), F32)
    for _ in range(TOP_K):
        mx = jnp.max(cur, axis=0, keepdims=True)
        fi = jnp.min(jnp.where(cur == mx, iota64, float(N_EXPERTS)), axis=0, keepdims=True)
        pick = iota64 == fi
        wks.append(jnp.sum(jnp.where(pick, scores, 0.0), axis=0, keepdims=True))
        idxs.append(fi)
        onehot = jnp.where(pick, 1.0, onehot)
        cur = jnp.where(pick, ninf, cur)
    wk = jnp.concatenate(wks, axis=0)
    wgt_ref[:, cols] = wk / jnp.sum(wk, axis=0, keepdims=True) * ROUTED_SCALE
    eidx_ref[:, cols] = jnp.concatenate(idxs, axis=0).astype(I32)

    before = jnp.dot(onehot.astype(BF16), tri_ref[...], preferred_element_type=F32) + carry_ref[:, 0:1]
    rank_ref[:, cols] = jnp.concatenate(
        [jnp.sum(jnp.where(iota64 == fi, before, 0.0), axis=0, keepdims=True) for fi in idxs],
        axis=0).astype(I32)
    carry_ref[...] = carry_ref[...] + jnp.sum(onehot, axis=1, keepdims=True)


def _merge_call(x2d, ya, yb, g1, wg, wba, wbb, wo, g2, wrh, wrl, bias, wsgu, wsd, tri, row0, n):
    tm = TM_MERGE
    off = row0 // tm
    full = lambda shape: pl.BlockSpec(shape, lambda i: (0,) * len(shape))
    row_in = lambda w: pl.BlockSpec((tm, w), lambda i: (i + off, 0))
    row = lambda w: pl.BlockSpec((tm, w), lambda i: (i, 0))
    col = lambda: pl.BlockSpec((TOP_K, tm), lambda i: (0, i))
    return pl.pallas_call(
        _merge_kernel,
        grid=(n // tm,),
        in_specs=[
            row_in(D_MODEL), row_in(A_OUT_W), row_in(B_Q_W),
            full((1, D_MODEL)), full((D_MODEL, 2 * D_MODEL)),
            full((A_OUT_W, D_MODEL)), full((B_Q_W, D_MODEL)), full((D_MODEL, D_MODEL)),
            full((1, D_MODEL)),
            full((D_MODEL, LANES)), full((D_MODEL, LANES)), full((N_EXPERTS, 1)),
            full((D_MODEL, 2 * EXPERT_FF)), full((EXPERT_FF, D_MODEL)),
            full((ROUTE_TILE, ROUTE_TILE)),
        ],
        out_specs=[row(D_MODEL), row(D_MODEL // 2), col(), col(), col(), full((N_EXPERTS, LANES))],
        out_shape=[
            jax.ShapeDtypeStruct((n, D_MODEL), F32),
            jax.ShapeDtypeStruct((n, D_MODEL // 2), U32),
            jax.ShapeDtypeStruct((TOP_K, n), I32),
            jax.ShapeDtypeStruct((TOP_K, n), I32),
            jax.ShapeDtypeStruct((TOP_K, n), F32),
            jax.ShapeDtypeStruct((N_EXPERTS, LANES), F32),
        ],
        scratch_shapes=[pltpu.VMEM((N_EXPERTS, LANES), F32)],
        compiler_params=_cparams(("arbitrary",)),
        name="merge_router",
    )(x2d, ya, yb, g1, wg, wba, wbb, wo, g2, wrh, wrl, bias, wsgu, wsd, tri)


EXPERT_SUBBLOCKS = 4


X_SLOTS = 3
Y_SLOTS = 2


def _expert_kernel(be_ref, nb_ref, xs_hbm, *refs):
    nw = 3 * EXPERT_SUBBLOCKS
    w_refs = refs[:nw]
    ys_hbm, xbuf, ybuf, xsem, ysem = refs[nw:]
    j = pl.program_id(0)
    nsteps = pl.num_programs(0)
    rows = EXPERT_SUBBLOCKS * BM_EXPERT

    def x_copy(step, slot):
        src = xs_hbm.at[pl.ds(pl.multiple_of(step * rows, rows), rows)]
        return pltpu.make_async_copy(src, xbuf.at[slot], xsem.at[slot])

    def y_copy(step, slot):
        dst = ys_hbm.at[pl.ds(pl.multiple_of(step * rows, rows), rows)]
        return pltpu.make_async_copy(ybuf.at[slot], dst, ysem.at[slot])

    @pl.when(j == 0)
    def _():
        x_copy(0, 0).start()
        x_copy(1, 1).start()

    @pl.when(j + 2 < nsteps)
    def _():
        x_copy(j + 2, lax.rem(j + 2, X_SLOTS)).start()

    xslot = lax.rem(j, X_SLOTS)
    yslot = lax.rem(j, Y_SLOTS)
    x_copy(j, xslot).wait()

    @pl.when(j >= Y_SLOTS)
    def _():
        y_copy(j - Y_SLOTS, yslot).wait()

    first = j * EXPERT_SUBBLOCKS

    @pl.when(first >= nb_ref[0])
    def _():
        ybuf[yslot] = jnp.zeros((rows, D_MODEL // 2), U32)

    @pl.when(first < nb_ref[0])
    def _():
        half = D_MODEL // 2
        for sub in range(EXPERT_SUBBLOCKS):
            wg_ref, wu_ref, wd_ref = w_refs[3 * sub:3 * sub + 3]
            blk = slice(sub * BM_EXPERT, (sub + 1) * BM_EXPERT)
            lo, hi = _unpack_bf16_pairs(xbuf[xslot, blk, :])
            lo = lo.astype(BF16)
            hi = hi.astype(BF16)
            gate = (jnp.dot(lo, wg_ref[0, :half, :], preferred_element_type=F32)
                    + jnp.dot(hi, wg_ref[0, half:, :], preferred_element_type=F32))
            up = (jnp.dot(lo, wu_ref[0, :half, :], preferred_element_type=F32)
                  + jnp.dot(hi, wu_ref[0, half:, :], preferred_element_type=F32))
            hb = (jax.nn.silu(gate) * up).astype(BF16)
            ybuf[yslot, blk, :] = _pack_bf16_pairs(jnp.dot(hb, wd_ref[0], preferred_element_type=F32))

    y_copy(j, yslot).start()

    @pl.when(j == nsteps - 1)
    def _():
        y_copy(j - 1, lax.rem(j - 1, Y_SLOTS)).wait()
        y_copy(j, yslot).wait()


def _expert_call(block_e, n_used, xs, wg, wu, wd):
    p_len = xs.shape[0]
    nsub = EXPERT_SUBBLOCKS
    rows = nsub * BM_EXPERT
    w_specs, w_args = [], []
    for sub in range(nsub):
        pick = lambda j, be, nb, sub=sub: (be[nsub * j + sub], 0, 0)
        w_specs += [
            pl.BlockSpec((1, D_MODEL, EXPERT_FF), pick),
            pl.BlockSpec((1, D_MODEL, EXPERT_FF), pick),
            pl.BlockSpec((1, EXPERT_FF, D_MODEL), pick),
        ]
        w_args += [wg, wu, wd]
    assert p_len // rows >= max(X_SLOTS, Y_SLOTS)
    return pl.pallas_call(
        _expert_kernel,
        grid_spec=pltpu.PrefetchScalarGridSpec(
            num_scalar_prefetch=2,
            grid=(p_len // rows,),
            in_specs=[pl.BlockSpec(memory_space=pl.ANY)] + w_specs,
            out_specs=pl.BlockSpec(memory_space=pl.ANY),
            scratch_shapes=[
                pltpu.VMEM((X_SLOTS, rows, D_MODEL // 2), U32),
                pltpu.VMEM((Y_SLOTS, rows, D_MODEL // 2), U32),
                pltpu.SemaphoreType.DMA((X_SLOTS,)),
                pltpu.SemaphoreType.DMA((Y_SLOTS,)),
            ],
        ),
        out_shape=jax.ShapeDtypeStruct((p_len, D_MODEL // 2), U32),
        compiler_params=_cparams(("arbitrary",)),
        name="routed_experts",
    )(block_e, n_used, xs, *w_args)


SC_CORES = 2
SC_SUBCORES = 16
SC_WORKERS = SC_CORES * SC_SUBCORES
SC_CHUNK = 128


def _sc_mesh():
    return plsc.VectorSubcoreMesh(core_axis_name="c", subcore_axis_name="s",
                                  num_cores=SC_CORES, num_subcores=SC_SUBCORES)


def _dispatch_rows(xp, dest, p_len):
    n, width = xp.shape
    per_worker = n // SC_CHUNK // SC_WORKERS

    @functools.partial(
        pl.kernel, mesh=_sc_mesh(),
        out_type=jax.ShapeDtypeStruct((p_len, width), xp.dtype),
        scratch_types=[pltpu.VMEM((TOP_K, SC_CHUNK), I32), pltpu.VMEM((SC_CHUNK, width), xp.dtype),
                       pltpu.SemaphoreType.DMA],
        name="sc_dispatch")
    def body(xp_hbm, dest_hbm, xs_hbm, idx_v, rows_v, sem):
        wid = lax.axis_index("s") * SC_CORES + lax.axis_index("c")

        @pl.loop(0, per_worker)
        def _(j):
            chunk = wid * per_worker + j
            pltpu.sync_copy(dest_hbm.at[:, pl.ds(chunk * SC_CHUNK, SC_CHUNK)], idx_v)
            pltpu.sync_copy(xp_hbm.at[pl.ds(chunk * SC_CHUNK, SC_CHUNK)], rows_v)
            copies = [pltpu.async_copy(rows_v, xs_hbm.at[idx_v.at[k]], sem) for k in range(TOP_K)]
            for cp in copies:
                cp.wait()

    return body(xp, dest)


SC_LANES = 16
SC_REDUCE_CHUNK = 8


def _reduce_rows(ys, dest, wbits):
    width = ys.shape[1]
    tc = SC_REDUCE_CHUNK
    n = dest.shape[1]
    tokens = n // SC_WORKERS
    per_worker = tokens // tc
    assert per_worker % 2 == 0 and 2 * tc == SC_LANES

    @functools.partial(
        pl.kernel, mesh=_sc_mesh(),
        out_type=jax.ShapeDtypeStruct((n, width), ys.dtype),
        scratch_types=[pltpu.VMEM((TOP_K, tokens), I32),
                       pltpu.VMEM((TOP_K, tokens), ys.dtype),
                       pltpu.VMEM((2, TOP_K, tc, width), ys.dtype),
                       pltpu.VMEM((2, tc, width), ys.dtype),
                       pltpu.SemaphoreType.DMA, pltpu.SemaphoreType.DMA,
                       pltpu.SemaphoreType.DMA, pltpu.SemaphoreType.DMA],
        compiler_params=pltpu.CompilerParams(needs_layout_passes=False),
        name="sc_reduce")
    def body(ys_hbm, dest_hbm, w_hbm, out_hbm, idx_v, w_v, rows_v, r_v, gsem0, gsem1, osem0, osem1):
        wid = lax.axis_index("s") * SC_CORES + lax.axis_index("c")
        first = wid * tokens
        gsems, osems = (gsem0, gsem1), (osem0, osem1)
        pltpu.sync_copy(dest_hbm.at[:, pl.ds(first, tokens)], idx_v)
        pltpu.sync_copy(w_hbm.at[:, pl.ds(first, tokens)], w_v)

        def fetches(j, slot):
            return [pltpu.make_async_copy(ys_hbm.at[idx_v.at[k, pl.ds(j * tc, tc)]], rows_v.at[slot, k],
                                          gsems[slot])
                    for k in range(TOP_K)]

        def write_back(j, slot):
            return pltpu.make_async_copy(r_v.at[slot], out_hbm.at[pl.ds(first + j * tc, tc)],
                                         osems[slot])

        def reduce_item(p, slot):
            wrows = [w_v[k, pl.ds(pl.multiple_of(p * SC_LANES, SC_LANES), SC_LANES)] for k in range(TOP_K)]

            @pl.loop(0, tc)
            def _(t):
                lane = jnp.full((SC_LANES,), slot * tc + t, I32)
                ws = [plsc.bitcast(wrows[k].at[lane].get(mode="promise_in_bounds"), BF16)
                      for k in range(TOP_K)]

                @pl.loop(0, width // SC_LANES)
                def _(jv):
                    col = pl.multiple_of(jv * SC_LANES, SC_LANES)
                    terms = [plsc.bitcast(rows_v[slot, k, t, pl.ds(col, SC_LANES)], BF16) * ws[k]
                             for k in range(TOP_K)]
                    while len(terms) > 1:
                        terms = [terms[i] + terms[i + 1] for i in range(0, len(terms), 2)]
                    r_v[slot, t, pl.ds(col, SC_LANES)] = plsc.bitcast(terms[0], ys.dtype)

        for cp in fetches(0, 0):
            cp.start()

        @pl.loop(0, per_worker // 2)
        def _(p):
            for slot in range(2):
                j = 2 * p + slot
                @pl.when(j + 1 < per_worker)
                def _():
                    for cp in fetches(j + 1, 1 - slot):
                        cp.start()
                for cp in fetches(j, slot):
                    cp.wait()

                @pl.when(p > 0)
                def _():
                    write_back(j - 2, slot).wait()
                reduce_item(p, slot)
                write_back(j, slot).start()

        write_back(per_worker - 2, 0).wait()
        write_back(per_worker - 1, 1).wait()

    return body(ys, dest, wbits)


def _final_add_kernel(base_ref, r_ref, *refs):
    o_ref = refs[-1]
    half = D_MODEL // 2
    lo, hi = _unpack_bf16_pairs(r_ref[...])
    o_ref[:, :half] = base_ref[:, :half] + lo
    o_ref[:, half:] = base_ref[:, half:] + hi


def _final_add_call(base, routed, prev, row0, n_total):
    n = base.shape[0]
    tm = TM_COMBINE
    off = row0 // tm
    in_specs = [pl.BlockSpec((tm, D_MODEL), lambda i: (i, 0)),
                pl.BlockSpec((tm, D_MODEL // 2), lambda i: (i, 0))]
    args, aliases = [base, routed], {}
    if prev is not None:
        in_specs.append(pl.BlockSpec(memory_space=pl.ANY))
        args.append(prev)
        aliases = {2: 0}
    return pl.pallas_call(
        _final_add_kernel,
        grid=(n // tm,),
        in_specs=in_specs,
        out_specs=pl.BlockSpec((tm, D_MODEL), lambda i: (i + off, 0)),
        out_shape=jax.ShapeDtypeStruct((n_total, D_MODEL), F32),
        input_output_aliases=aliases,
        compiler_params=_cparams(("arbitrary",)),
        name="moe_final_add",
    )(*args)


def _fold_gain(tab, gain, half):
    g = jnp.tile(gain.astype(F32), LANES // HEAD_DIM)
    return jnp.stack([tab[0] * g, tab[1] * jnp.roll(g, LANES - half), tab[2] * jnp.roll(g, half)])


def _prep_tables(seq):
    pos = jnp.arange(seq)
    cos1, sin1 = _rope_tables(pos, HEAD_DIM)
    cos_r, sin_r = _rope_tables(pos // GRID_W, HEAD_DIM // 2)
    cos_c, sin_c = _rope_tables(pos % GRID_W, HEAD_DIM // 2)
    taba = _lane_tables(cos1, sin1, HEAD_DIM // 2)
    tabb = _lane_tables(jnp.concatenate([cos_r, cos_c], -1), jnp.concatenate([sin_r, sin_c], -1),
                        HEAD_DIM // 4)
    return taba, tabb


def _layer(h, p, taba, tabb):
    b, seq, d = h.shape
    n = b * seq
    x2d = h.reshape(n, d)
    w_in = p["w_in"]
    o1 = N_GROUPS_A * A_GROUP_W
    o2 = o1 + B_Q_W
    o3 = o2 + B_KV_W
    pair_heads = jnp.array([0, 4, 1, 5, 2, 6, 3, 7])
    pair_cols = (pair_heads[:, None] * HEAD_DIM + jnp.arange(HEAD_DIM)[None, :]).reshape(-1)
    wqkv = jnp.concatenate([w_in[:, :o1], w_in[:, o1:o2][:, pair_cols], w_in[:, o2:o3]], axis=1).astype(BF16)
    tabs = jnp.stack([
        _fold_gain(taba, p["q_norm_a"] * Q_SCALE, HEAD_DIM // 2),
        _fold_gain(taba, p["k_norm_a"], HEAD_DIM // 2),
        _fold_gain(tabb, p["q_norm_b"] * Q_SCALE, HEAD_DIM // 4),
        _fold_gain(tabb, p["k_norm_b"], HEAD_DIM // 4),
    ])
    seg = jnp.arange(2 * LANES) // HEAD_DIM
    bd = jnp.where(seg[:, None] == seg[None, :], 1.0 / HEAD_DIM, 0.0).astype(BF16)
    g1 = p["norm1_g"].reshape(1, d).astype(F32)

    a0, a1, a2, qb, kvb = _proj_call(x2d, g1, wqkv, tabs, bd, seq)
    ya = _attn_a_call(a0.reshape(b, seq, A_GROUP_W), a1.reshape(b, seq // 4, 4 * A_GROUP_W),
                      a2.reshape(b, seq // 16, 16 * A_GROUP_W))
    yb = _attn_b_call(qb.reshape(b, seq, B_Q_W), kvb.reshape(b, seq, B_KV_W))

    wg = w_in[:, o3:].astype(BF16)
    wbb = p["w_branch_b"][pair_cols, :].astype(BF16)
    wr = jnp.pad(p["w_router"], ((0, 0), (0, LANES - N_EXPERTS)))
    wrh = wr.astype(BF16)
    wrl = (wr - wrh.astype(F32)).astype(BF16)
    wsgu = jnp.concatenate([p["ws_gate"], p["ws_up"]], axis=1).astype(BF16)
    tri = (jnp.arange(ROUTE_TILE)[:, None] < jnp.arange(ROUTE_TILE)[None, :]).astype(BF16)
    wba = p["w_branch_a"].astype(BF16)
    wo = p["w_out"].astype(BF16)
    g2 = p["norm2_g"].reshape(1, d).astype(F32)
    bias = p["router_bias"].reshape(N_EXPERTS, 1).astype(F32)
    wsd = p["ws_down"].astype(BF16)
    weg = p["we_gate"].astype(BF16)
    weu = p["we_up"].astype(BF16)
    wd = p["we_down"].astype(BF16)
    ya2d = ya.reshape(n, A_OUT_W)
    yb2d = yb.reshape(n, B_Q_W)

    nc = n // MOE_CHUNKS
    bm = BM_EXPERT
    n_blocks = (nc * TOP_K + N_EXPERTS * (bm - 1)) // bm + 1
    n_blocks = -(-n_blocks // EXPERT_SUBBLOCKS) * EXPERT_SUBBLOCKS
    p_len = n_blocks * bm
    block_start = jnp.arange(n_blocks, dtype=I32) * bm
    out = None
    for c in range(MOE_CHUNKS):
        base, xp, eidx, rank, wgt, cnt = _merge_call(
            x2d, ya2d, yb2d, g1, wg, wba, wbb, wo, g2, wrh, wrl, bias, wsgu, wsd, tri, c * nc, nc)
        counts = cnt[:, 0].astype(I32)
        padded = (counts + bm - 1) // bm * bm
        pend = jnp.cumsum(padded)
        pstart = pend - padded
        onehot = eidx[:, :, None] == jnp.arange(N_EXPERTS)[None, None, :]
        dest = jnp.sum(jnp.where(onehot, pstart[None, None, :], 0), axis=-1) + rank
        block_e = jnp.minimum(jnp.sum(pend[None, :] <= block_start[:, None], axis=1),
                              N_EXPERTS - 1).astype(I32)
        n_used = (pend[-1] // bm).astype(I32).reshape(1)
        wbits = lax.bitcast_convert_type(wgt.astype(BF16), jnp.uint16).astype(U32)
        wbits = wbits | (wbits << 16)

        xs = _dispatch_rows(xp, dest, p_len)
        ys = _expert_call(block_e, n_used, xs, weg, weu, wd)
        routed = _reduce_rows(ys, dest, wbits)
        out = _final_add_call(base, routed, out, c * nc, n)
    return out.reshape(b, seq, d)


def kernel(x, norm1_g, w_in, q_norm_a, k_norm_a, q_norm_b, k_norm_b, w_branch_a, w_branch_b, w_out,
           norm2_g, w_router, router_bias, we_gate, we_up, we_down, ws_gate, ws_up, ws_down):
    params = dict(norm1_g=norm1_g, w_in=w_in, q_norm_a=q_norm_a, k_norm_a=k_norm_a, q_norm_b=q_norm_b,
                  k_norm_b=k_norm_b, w_branch_a=w_branch_a, w_branch_b=w_branch_b, w_out=w_out,
                  norm2_g=norm2_g, w_router=w_router, router_bias=router_bias, we_gate=we_gate,
                  we_up=we_up, we_down=we_down, ws_gate=ws_gate, ws_up=ws_up, ws_down=ws_down)
    taba, tabb = _prep_tables(x.shape[1])
    h = x
    for l in range(norm1_g.shape[0]):
        h = _layer(h, {k: v[l] for k, v in params.items()}, taba, tabb)
    return h
```
